```python
import math
import jax, jax.numpy as jnp
from jax import lax
import numpy as np

D_MODEL = 1024
BATCH = 8
SEQ = 4096
DEPTH = 4

CHUNK = 64
Q_BLOCK = 128
N_MIXERS = 4
HEAD_DIM = 64
GW = D_MODEL // N_MIXERS
HG = GW // HEAD_DIM
CG = GW // HG
DQ = HEAD_DIM // 2
RW_DECAY_RANK = 32
RW_A_RANK = 32
RW_GATE_RANK = 64
RW_GN_EPS = 64e-5
SG_CHUNK = 128
D_FF = 2816
FFN_CONV = 3
ROPE_THETA = 10000.0
EPS = 1e-6
NEG_INF = -1e30

A_SIZES = [GW, GW, GW, RW_DECAY_RANK, RW_A_RANK, RW_GATE_RANK]
B_SIZES = [GW, GW, GW]
C_SIZES = [GW, GW]
D_SIZES = [GW, GW, GW, HG]
GROUP_COLS = [sum(A_SIZES), sum(B_SIZES), sum(C_SIZES), sum(D_SIZES)]
N_IN = sum(GROUP_COLS)

kernel_name = 'hybrid_parallel_heads_streaming_encoder'


def _split(t, sizes):
    idx = [int(i) for i in np.cumsum(sizes)[:-1]]
    return jnp.split(t, idx, axis=-1)


def _rms_norm(x, g):
    xf = x.astype(jnp.float32)
    y = xf * lax.rsqrt(jnp.mean(xf * xf, axis=-1, keepdims=True) + EPS)
    return (y * g).astype(x.dtype)


def _layer_norm(x, g, b):
    xf = x.astype(jnp.float32)
    mu = jnp.mean(xf, axis=-1, keepdims=True)
    xc = xf - mu
    var = jnp.mean(xc * xc, axis=-1, keepdims=True)
    return (xc * lax.rsqrt(var + EPS) * g + b).astype(x.dtype)


def _rope_tables(seq, dim):
    inv = 1.0 / (ROPE_THETA ** (jnp.arange(0, dim, 2, dtype=jnp.float32) / dim))
    ang = jnp.arange(seq, dtype=jnp.float32)[:, None] * inv[None, :]
    return jnp.cos(ang), jnp.sin(ang)


def _apply_rope(x, cos, sin):
    x1, x2 = jnp.split(x, 2, axis=-1)
    c = cos.astype(x.dtype)
    s = sin.astype(x.dtype)
    return jnp.concatenate([x1 * c - x2 * s, x2 * c + x1 * s], axis=-1)


def _causal_dwconv(x, w, b):
    K, C = w.shape
    y = lax.conv_general_dilated(x, w[:, None, :].astype(x.dtype), window_strides=(1,),
                                 padding=[(K - 1, 0)],
                                 dimension_numbers=('NWC', 'WIO', 'NWC'),
                                 feature_group_count=C)
    return y + b


def _rwkv7_scan(r, w, k, v, kk, b):
    Bn, S, H, N = r.shape

    def step(state, inp):
        r_t, w_t, k_t, v_t, kk_t, b_t = inp
        sa = jnp.einsum('bhvk,bhk->bhv', state, -kk_t)
        state = (state * w_t[:, :, None, :]
                 + sa[..., None] * b_t[:, :, None, :]
                 + v_t[..., None] * k_t[:, :, None, :])
        return state, jnp.einsum('bhvk,bhk->bhv', state, r_t)

    xs = tuple(jnp.swapaxes(t, 0, 1) for t in (r, w, k, v, kk, b))
    state0 = jnp.zeros((Bn, H, N, N), jnp.float32)
    _, ys = lax.scan(step, state0, xs)
    return jnp.swapaxes(ys, 0, 1)


def _rwkv7_mixer(p, mu, w0, w_up, a0, a_up, g_up, k_k, k_a, r_k, ln_g, ln_b):
    Bn, S, _ = p.shape
    f32 = jnp.float32
    p_prev = jnp.pad(p, ((0, 0), (1, 0), (0, 0)))[:, :-1]
    p = p + (p_prev - p) * mu
    r, k, v, wd, ad, gd = _split(p, A_SIZES)
    w_log = -jax.nn.softplus(-(w0 + jnp.tanh(wd) @ w_up).astype(f32)) - 0.5
    decay = jnp.exp(-jnp.exp(w_log))
    a = jax.nn.sigmoid((a0 + ad @ a_up).astype(f32))
    g = jax.nn.sigmoid(gd) @ g_up
    heads = lambda t: t.reshape(Bn, S, HG, HEAD_DIM)
    k = k.astype(f32)
    kk = heads(k * k_k)
    kk = kk / jnp.maximum(jnp.sqrt(jnp.sum(kk * kk, axis=-1, keepdims=True)), 1e-12)
    k = heads(k * (1.0 + (a - 1.0) * k_a))
    r = heads(r.astype(f32))
    v = heads(v.astype(f32))
    ah = heads(a)
    y = _rwkv7_scan(r, heads(decay), k, v, kk, kk * ah)
    mean = jnp.mean(y, axis=-1, keepdims=True)
    yc = y - mean
    y = yc * lax.rsqrt(jnp.mean(yc * yc, axis=-1, keepdims=True) + RW_GN_EPS)
    y = y.reshape(Bn, S, GW) * ln_g + ln_b
    bonus = jnp.sum(r * k * r_k, axis=-1, keepdims=True) * v
    return (y + bonus.reshape(Bn, S, GW)) * g


def _diff_attention(q1, q2, k1, k2, v, lam):
    Bn, H, S, dq = q1.shape
    scale = dq ** -0.5
    key_chunk = jnp.arange(S) // CHUNK

    def block(i):
        start = i * Q_BLOCK
        q1b = lax.dynamic_slice_in_dim(q1, start, Q_BLOCK, axis=2)
        q2b = lax.dynamic_slice_in_dim(q2, start, Q_BLOCK, axis=2)
        q_chunk = (start + jnp.arange(Q_BLOCK)) // CHUNK
        mask = key_chunk[None, :] <= q_chunk[:, None]
        s1 = jnp.einsum('bhqd,bhkd->bhqk', q1b, k1).astype(jnp.float32) * scale
        s2 = jnp.einsum('bhqd,bhkd->bhqk', q2b, k2).astype(jnp.float32) * scale
        p1 = jax.nn.softmax(jnp.where(mask, s1, NEG_INF), axis=-1)
        p2 = jax.nn.softmax(jnp.where(mask, s2, NEG_INF), axis=-1)
        return jnp.einsum('bhqk,bhkd->bhqd', (p1 - lam * p2).astype(v.dtype), v)

    out = lax.map(block, jnp.arange(S // Q_BLOCK))
    return jnp.moveaxis(out, 0, 2).reshape(Bn, H, S, v.shape[-1])


def _diff_attn_mixer(p, lam_q1, lam_k1, lam_q2, lam_k2, q_g, k_g, sub_g, cos, sin, lambda_init):
    Bn, S, _ = p.shape
    q, k, v = _split(p, B_SIZES)
    q = _rms_norm(q.reshape(Bn, S, HG, 2, DQ), q_g).transpose(0, 3, 2, 1, 4)
    k = _rms_norm(k.reshape(Bn, S, HG, 2, DQ), k_g).transpose(0, 3, 2, 1, 4)
    q = _apply_rope(q, cos, sin)
    k = _apply_rope(k, cos, sin)
    v = v.reshape(Bn, S, HG, HEAD_DIM).transpose(0, 2, 1, 3)
    f32 = jnp.float32
    lam = (jnp.exp(jnp.sum(lam_q1.astype(f32) * lam_k1.astype(f32)))
           - jnp.exp(jnp.sum(lam_q2.astype(f32) * lam_k2.astype(f32))) + lambda_init)
    o = _diff_attention(q[:, 0], q[:, 1], k[:, 0], k[:, 1], v, lam)
    o = _rms_norm(o, sub_g) * (1.0 - lambda_init)
    return o.transpose(0, 2, 1, 3).reshape(Bn, S, GW)


def _spatial_gating_mixer(p, sg_w, sg_b, ln_g, ln_b):
    Bn, S, _ = p.shape
    u, v = _split(jax.nn.gelu(p, approximate=False), C_SIZES)
    v = _layer_norm(v, ln_g, ln_b)
    v = v.reshape(Bn, S // SG_CHUNK, SG_CHUNK, HG, CG)
    w = sg_w * jnp.tril(jnp.ones((SG_CHUNK, SG_CHUNK), sg_w.dtype))
    sv = jnp.einsum('gij,bnjgc->bnigc', w, v) + sg_b.T[None, None, :, :, None]
    return u * sv.reshape(Bn, S, GW)


def _forgetting_attention(q, k, v, log_f):
    Bn, H, S, d = q.shape
    scale = d ** -0.5
    F = lax.cumsum(log_f, axis=2)
    key_pos = jnp.arange(S)

    def block(i):
        start = i * Q_BLOCK
        qb = lax.dynamic_slice_in_dim(q, start, Q_BLOCK, axis=2)
        Fq = lax.dynamic_slice_in_dim(F, start, Q_BLOCK, axis=2)
        q_pos = start + jnp.arange(Q_BLOCK)
        mask = key_pos[None, :] <= q_pos[:, None]
        s = (jnp.einsum('bhqd,bhkd->bhqk', qb, k).astype(jnp.float32) * scale
             + (Fq[..., :, None] - F[..., None, :]))
        pr = jax.nn.softmax(jnp.where(mask, s, NEG_INF), axis=-1)
        return jnp.einsum('bhqk,bhkd->bhqd', pr.astype(v.dtype), v)

    out = lax.map(block, jnp.arange(S // Q_BLOCK))
    return jnp.moveaxis(out, 0, 2).reshape(Bn, H, S, d)


def _forgetting_attn_mixer(p, q_g, k_g, f_b):
    Bn, S, _ = p.shape
    q, k, v, fl = _split(p, D_SIZES)
    q = _rms_norm(q.reshape(Bn, S, HG, HEAD_DIM), q_g).transpose(0, 2, 1, 3)
    k = _rms_norm(k.reshape(Bn, S, HG, HEAD_DIM), k_g).transpose(0, 2, 1, 3)
    v = v.reshape(Bn, S, HG, HEAD_DIM).transpose(0, 2, 1, 3)
    log_f = jax.nn.log_sigmoid((fl + f_b).astype(jnp.float32)).transpose(0, 2, 1)
    o = _forgetting_attention(q, k, v, log_f)
    return o.transpose(0, 2, 1, 3).reshape(Bn, S, GW)


def _conv_glu_ffn(h, w_up, w_conv, b_conv, w_down):
    u = _causal_dwconv(h @ w_up, w_conv, b_conv)
    a, g = jnp.split(u, 2, axis=-1)
    return (a * jax.nn.silu(g)) @ w_down


def setup_inputs(seed: int = 0) -> dict:
    key = jax.random.key(seed)
    ks = iter(list(jax.random.split(key, 48)))
    L, D = DEPTH, D_MODEL
    f32 = jnp.float32

    def nrm(shape, scale):
        return jax.random.normal(next(ks), shape, f32) * scale

    def gain(shape):
        return 1.0 + 0.05 * jax.random.normal(next(ks), shape, f32)

    return {
        'x': nrm((BATCH, SEQ, D), 1.0),
        'c': nrm((BATCH, D), 1.0),
        'ada_w': nrm((L, D, 6 * D), 0.5 * D ** -0.5),
        'ada_b': nrm((L, 6 * D), 0.02),
        'norm1_g': gain((L, D)),
        'norm2_g': gain((L, D)),
        'w_in': nrm((L, D, N_IN), D ** -0.5),
        'w_out': nrm((L, D, D), D ** -0.5),
        'rw_mu': jax.random.uniform(next(ks), (L, GROUP_COLS[0]), f32),
        'rw_w0': nrm((L, GW), 0.5),
        'rw_w_up': nrm((L, RW_DECAY_RANK, GW), RW_DECAY_RANK ** -0.5),
        'rw_a0': nrm((L, GW), 0.1),
        'rw_a_up': nrm((L, RW_A_RANK, GW), RW_A_RANK ** -0.5),
        'rw_g_up': nrm((L, RW_GATE_RANK, GW), RW_GATE_RANK ** -0.5),
        'rw_k_k': 0.85 + 0.05 * jax.random.normal(next(ks), (L, GW), f32),
        'rw_k_a': gain((L, GW)),
        'rw_r_k': nrm((L, HG, HEAD_DIM), 0.1),
        'rw_ln_g': gain((L, GW)),
        'rw_ln_b': nrm((L, GW), 0.02),
        'df_lam_q1': nrm((L, DQ), 0.1),
        'df_lam_k1': nrm((L, DQ), 0.1),
        'df_lam_q2': nrm((L, DQ), 0.1),
        'df_lam_k2': nrm((L, DQ), 0.1),
        'df_q_g': gain((L, DQ)),
        'df_k_g': gain((L, DQ)),
        'df_sub_g': gain((L, HEAD_DIM)),
        'sg_w': nrm((L, HG, SG_CHUNK, SG_CHUNK), SG_CHUNK ** -0.5),
        'sg_b': gain((L, HG, SG_CHUNK)),
        'sg_ln_g': gain((L, GW)),
        'sg_ln_b': nrm((L, GW), 0.02),
        'fx_q_g': gain((L, HEAD_DIM)),
        'fx_k_g': gain((L, HEAD_DIM)),
        'fx_f_b': 2.0 + 0.5 * jax.random.normal(next(ks), (L, HG), f32),
        'ffn_up': nrm((L, D, 2 * D_FF), D ** -0.5),
        'ffn_conv': nrm((L, FFN_CONV, 2 * D_FF), FFN_CONV ** -0.5),
        'ffn_conv_b': nrm((L, 2 * D_FF), 0.02),
        'ffn_down': nrm((L, D_FF, D), D_FF ** -0.5),
    }


def reference(x, c, ada_w, ada_b, norm1_g, norm2_g, w_in, w_out,
              rw_mu, rw_w0, rw_w_up, rw_a0, rw_a_up, rw_g_up, rw_k_k, rw_k_a, rw_r_k,
              rw_ln_g, rw_ln_b,
              df_lam_q1, df_lam_k1, df_lam_q2, df_lam_k2, df_q_g, df_k_g, df_sub_g,
              sg_w, sg_b, sg_ln_g, sg_ln_b,
              fx_q_g, fx_k_g, fx_f_b,
              ffn_up, ffn_conv, ffn_conv_b, ffn_down):
    Bn, S, D = x.shape
    dt = x.dtype
    cos, sin = _rope_tables(S, DQ)
    cond = jax.nn.silu(c)
    for l in range(DEPTH):
        lambda_init = 0.8 - 0.6 * math.exp(-0.3 * l)
        mod = cond @ ada_w[l] + ada_b[l]
        sh1, sc1, g1, sh2, sc2, g2 = [m[:, None, :] for m in jnp.split(mod, 6, axis=-1)]
        h = _rms_norm(x, norm1_g[l]) * (1.0 + sc1) + sh1
        pa, pb, pc, pd = _split(h @ w_in[l], GROUP_COLS)
        ya = _rwkv7_mixer(pa, rw_mu[l], rw_w0[l], rw_w_up[l], rw_a0[l], rw_a_up[l], rw_g_up[l],
                          rw_k_k[l], rw_k_a[l], rw_r_k[l], rw_ln_g[l], rw_ln_b[l])
        yb = _diff_attn_mixer(pb, df_lam_q1[l], df_lam_k1[l], df_lam_q2[l], df_lam_k2[l],
                              df_q_g[l], df_k_g[l], df_sub_g[l], cos, sin, lambda_init)
        yc = _spatial_gating_mixer(pc, sg_w[l], sg_b[l], sg_ln_g[l], sg_ln_b[l])
        yd = _forgetting_attn_mixer(pd, fx_q_g[l], fx_k_g[l], fx_f_b[l])
        y = jnp.concatenate([ya.astype(dt), yb.astype(dt), yc.astype(dt), yd.astype(dt)], axis=-1)
        x = x + g1 * (y @ w_out[l])
        h = _rms_norm(x, norm2_g[l]) * (1.0 + sc2) + sh2
        x = x + g2 * _conv_glu_ffn(h, ffn_up[l], ffn_conv[l], ffn_conv_b[l], ffn_down[l])
    return x
```

```python
import functools
import math

import jax
import jax.numpy as jnp
from jax import lax
from jax.experimental import pallas as pl
from jax.experimental.pallas import tpu as pltpu

F32 = jnp.float32
BF16 = jnp.bfloat16
HIGHEST = lax.Precision.HIGHEST

N_MIXERS = 4
HEAD_DIM = 64
HG = 4
GW = HG * HEAD_DIM
DQ = HEAD_DIM // 2
RW_DECAY_RANK = 32
RW_A_RANK = 32
RW_GATE_RANK = 64
RW_COLS = 3 * GW + RW_DECAY_RANK + RW_A_RANK + RW_GATE_RANK
RW_GN_EPS = 64e-5
SG_CHUNK = 128
ATT_CHUNK = 64
ROPE_THETA = 10000.0
EPS = 1e-6
NEG_INF = -1e30

LANES = 128
V7X_VMEM_BYTES = 64 * 1024 * 1024
VMEM_LIMIT = 56 * 1024 * 1024

ROW_TILE = 512
RW_CHUNK = 64
ATT_TILE = 256
SLAB = 128

NN = (((1,), (0,)), ((), ()))
NT = (((1,), (1,)), ((), ()))
TN = (((0,), (0,)), ((), ()))


def _params(sem):
    return pltpu.CompilerParams(dimension_semantics=sem, vmem_limit_bytes=VMEM_LIMIT)


def _mm(a, b, dims=NN, mode="hi"):
    if mode == "hi":
        return lax.dot_general(a.astype(F32), b.astype(F32), dims, precision=HIGHEST,
                               preferred_element_type=F32)
    if mode == "lo":
        return lax.dot_general(a.astype(BF16), b.astype(BF16), dims, preferred_element_type=F32)
    ah = a.astype(BF16)
    al = (a - ah.astype(F32)).astype(BF16)
    bh = b.astype(BF16)
    bl = (b - bh.astype(F32)).astype(BF16)
    dg = functools.partial(lax.dot_general, dimension_numbers=dims, preferred_element_type=F32)
    return dg(ah, bh) + (dg(ah, bl) + dg(al, bh))


def _iota(shape, dim):
    return lax.broadcasted_iota(jnp.int32, shape, dim)


def _group_matrix(n, group, value=1.0):
    same = (_iota((n, n), 0) // group) == (_iota((n, n), 1) // group)
    return jnp.where(same, value, 0.0).astype(F32)


def _softplus(x):
    return jnp.maximum(x, 0.0) + jnp.log1p(jnp.exp(-jnp.abs(x)))


def _sigmoid(x):
    return 1.0 / (1.0 + jnp.exp(-x))


def _ada_kernel(c_ref, w_ref, b_ref, o_ref):
    c = c_ref[...]
    cond = c * _sigmoid(c)
    o_ref[...] = _mm(cond, w_ref[...], NN, "hi") + b_ref[...]


def _ada_mod(c, ada_w, ada_b):
    L, D, D6 = ada_w.shape
    Bn = c.shape[0]
    tn = D6 // 4
    return pl.pallas_call(
        _ada_kernel,
        grid=(L, D6 // tn),
        in_specs=[
            pl.BlockSpec((Bn, D), lambda l, j: (0, 0)),
            pl.BlockSpec((None, D, tn), lambda l, j: (l, 0, j)),
            pl.BlockSpec((None, 1, tn), lambda l, j: (l, 0, j)),
        ],
        out_specs=pl.BlockSpec((None, Bn, tn), lambda l, j: (l, 0, j)),
        out_shape=jax.ShapeDtypeStruct((L, Bn, D6), F32),
        compiler_params=_params(("arbitrary", "arbitrary")),
        name="ada_mod",
    )(c, ada_w, ada_b.reshape(L, 1, D6))


def _modulated_norm(x, g, shift, scale):
    y = x * lax.rsqrt(jnp.mean(x * x, axis=-1, keepdims=True) + EPS) * g
    return y * (1.0 + scale) + shift


def _norm_kernel(x_ref, mod_ref, g_ref, o_ref, *, row0):
    mod = mod_ref[...]
    h = _modulated_norm(x_ref[...], g_ref[...], mod[row0:row0 + 1], mod[row0 + 1:row0 + 2])
    o_ref[...] = h.astype(BF16)


def _norm_mod(x2, mod_l, g, *, seq, row0, tm):
    T, D = x2.shape
    per_seq = seq // tm
    return pl.pallas_call(
        functools.partial(_norm_kernel, row0=row0),
        grid=(T // tm,),
        in_specs=[
            pl.BlockSpec((tm, D), lambda i: (i, 0)),
            pl.BlockSpec((None, 6, D), lambda i: (i // per_seq, 0, 0)),
            pl.BlockSpec((1, D), lambda i: (0, 0)),
        ],
        out_specs=pl.BlockSpec((tm, D), lambda i: (i, 0)),
        out_shape=jax.ShapeDtypeStruct((T, D), BF16),
        compiler_params=_params(("arbitrary",)),
        name="norm_mod",
    )(x2, mod_l, g.reshape(1, D))


def _blockdiag(mp, head_lane_masks):
    return jnp.concatenate([jnp.where(m, mp, 0.0) for m in head_lane_masks], axis=0)


def _apply_packed(mp, rhs, head_lane_masks, C, mode):
    full = _mm(mp, rhs, TN, mode)
    out = jnp.where(head_lane_masks[0], full[0:C], 0.0)
    for h in range(1, HG):
        out = out + jnp.where(head_lane_masks[h], full[h * C:(h + 1) * C], 0.0)
    return out


def _rwkv_kernel(h_ref, wa_ref, mu_ref, lora_ref, vec_ref, o_ref,
                 carry_ref, st_ref, r_s, lw_s, k_s, v_s, a_s, b_s, y_s, *, C, mode):
    TT = h_ref.shape[0]

    @pl.when(pl.program_id(1) == 0)
    def _():
        carry_ref[...] = jnp.zeros_like(carry_ref)
        st_ref[...] = jnp.zeros_like(st_ref)

    vec = vec_ref[...]
    w0, a0, k_k, k_a, r_k, ln_g, ln_b = (vec[i:i + 1] for i in range(7))

    pa = jnp.dot(h_ref[...], wa_ref[...], preferred_element_type=F32)
    prev = pltpu.roll(pa, 1, axis=0)
    prev = jnp.where(_iota((TT, 1), 0) == 0, carry_ref[...], prev)
    carry_ref[...] = pa[TT - 1:TT]
    pa = pa + (prev - pa) * mu_ref[...]

    r = pa[:, 0:GW]
    k = pa[:, GW:2 * GW]
    v = pa[:, 2 * GW:3 * GW]
    lo = pa[:, 3 * GW:3 * GW + LANES]
    lane = _iota((1, LANES), 1)
    act = jnp.where(lane < RW_DECAY_RANK, jnp.tanh(lo),
                    jnp.where(lane < RW_DECAY_RANK + RW_A_RANK, lo, _sigmoid(lo)))
    lora = _mm(act, lora_ref[...], NN, mode)
    w_log = -_softplus(-(w0 + lora[:, 0:GW])) - 0.5
    lw = -jnp.exp(w_log)
    a = _sigmoid(a0 + lora[:, GW:2 * GW])
    gate = lora[:, 2 * GW:3 * GW]

    gsum = _group_matrix(GW, HEAD_DIM)
    kk = k * k_k
    kk = kk / jnp.maximum(jnp.sqrt(_mm(kk * kk, gsum, NN, mode)), 1e-12)
    k2 = k * (1.0 + (a - 1.0) * k_a)
    bonus = _mm(r * k2 * r_k, gsum, NN, mode) * v

    r_s[...] = r
    lw_s[...] = lw
    k_s[...] = k2
    v_s[...] = v
    a_s[...] = -kk
    b_s[...] = kk * a

    lane_head = _iota((1, GW), 1) // HEAD_DIM
    hmask = [lane_head == h for h in range(HG)]
    s_idx = _iota((C, GW), 0)
    t_idx = _iota((C, GW), 1) % C
    strict = s_idx < t_idx
    incl = s_idx <= t_idx
    eye_p = jnp.where(s_idx == t_idx, 1.0, 0.0).astype(F32)
    l_incl = jnp.where(_iota((C, C), 0) >= _iota((C, C), 1), 1.0, 0.0).astype(F32)
    bd = (_iota((GW, GW), 0) // HEAD_DIM) == (_iota((GW, GW), 1) // HEAD_DIM)

    def chunk(c, carry):
        sl = pl.ds(pl.multiple_of(c * C, C), C)
        lwc = lw_s[sl, :]
        cl = _mm(l_incl, lwc, NN, "x3" if mode == "lo" else mode)
        g_in = jnp.exp(cl)
        g_inv = jnp.exp(-cl)
        at = a_s[sl, :] * jnp.exp(cl - lwc)
        rt = r_s[sl, :] * g_in
        bt = b_s[sl, :] * g_inv
        kt = k_s[sl, :] * g_inv
        vc = v_s[sl, :]

        ar = jnp.concatenate([_blockdiag(at, hmask), _blockdiag(rt, hmask)], axis=0)
        bk = jnp.concatenate([bt, kt], axis=0)
        x = _mm(bk, ar, NT, mode)
        n_p = jnp.where(strict, x[0:C, 0:GW], 0.0)
        ak_p = jnp.where(strict, x[C:2 * C, 0:GW], 0.0)
        rb_p = jnp.where(incl, x[0:C, GW:2 * GW], 0.0)
        rk_p = jnp.where(incl, x[C:2 * C, GW:2 * GW], 0.0)

        p_p = eye_p + n_p
        q_p = n_p
        for _ in range(int(math.log2(C)) - 1):
            q_p = _mm(q_p, _blockdiag(q_p, hmask), NN, mode)
            p_p = p_p + _mm(p_p, _blockdiag(q_p, hmask), NN, mode)

        ht = st_ref[...]
        sh = _mm(jnp.concatenate([at, rt], axis=0), ht, NT, mode)
        rhs = sh[0:C] + _apply_packed(ak_p, vc, hmask, C, mode)
        u = _apply_packed(p_p, rhs, hmask, C, mode)
        uv = jnp.concatenate([u, vc], axis=0)
        y = sh[C:2 * C] + _apply_packed(jnp.concatenate([rb_p, rk_p], axis=0), uv, hmask, C, mode)
        y_s[sl, :] = y
        upd = _mm(uv, bk, TN, mode)
        st_ref[...] = (ht + jnp.where(bd, upd, 0.0)) * g_in[C - 1:C, :]
        return carry

    lax.fori_loop(0, TT // C, chunk, 0)

    gmean = _group_matrix(GW, HEAD_DIM, 1.0 / HEAD_DIM)
    y = y_s[...]
    yc = y - _mm(y, gmean, NN, mode)
    yn = yc * lax.rsqrt(_mm(yc * yc, gmean, NN, mode) + RW_GN_EPS) * ln_g + ln_b
    o_ref[...] = ((yn + bonus) * gate).astype(BF16)


def _rwkv_mixer(h, wa, mu, lora_w, vec, *, batch, seq, tt, mode):
    T, D = h.shape
    per_seq = seq // tt
    C = RW_CHUNK
    big = pltpu.VMEM((tt, GW), F32)
    return pl.pallas_call(
        functools.partial(_rwkv_kernel, C=C, mode=mode),
        grid=(batch, per_seq),
        in_specs=[
            pl.BlockSpec((tt, D), lambda b, i: (b * per_seq + i, 0)),
            pl.BlockSpec((D, RW_COLS), lambda b, i: (0, 0)),
            pl.BlockSpec((1, RW_COLS), lambda b, i: (0, 0)),
            pl.BlockSpec((LANES, 3 * GW), lambda b, i: (0, 0)),
            pl.BlockSpec((8, GW), lambda b, i: (0, 0)),
        ],
        out_specs=pl.BlockSpec((tt, GW), lambda b, i: (b * per_seq + i, 0)),
        out_shape=jax.ShapeDtypeStruct((T, GW), BF16),
        scratch_shapes=[pltpu.VMEM((1, RW_COLS), F32), pltpu.VMEM((GW, GW), F32)] + [big] * 7,
        compiler_params=_params(("arbitrary", "arbitrary")),
        name="rwkv7_mixer",
    )(h, wa, mu, lora_w, vec)


def _place_heads_lanes(n_in, width):
    src = _iota((n_in, HG * SLAB), 0)
    dst = _iota((n_in, HG * SLAB), 1)
    ok = ((dst // SLAB) == (src // width)) & ((dst % SLAB) == (src % width)) & ((dst % SLAB) < width)
    return jnp.where(ok, 1.0, 0.0).astype(BF16)


def _slab_rows(xt, extra):
    parts = []
    for h in range(HG):
        parts.append(xt[h * HEAD_DIM:(h + 1) * HEAD_DIM])
        parts.append(extra)
    return jnp.concatenate(parts, axis=0)


def _diff_prep_kernel(h_ref, wk_ref, wqt_ref, wvt_ref, gk_ref, gqt_ref,
                      cos_ref, sin_ref, cost_ref, sint_ref, q_out, k_out, v_out, *, tile):
    tm = h_ref.shape[0]
    hb = h_ref[...]
    k = jnp.dot(hb, wk_ref[...], preferred_element_type=F32)
    qt = lax.dot_general(wqt_ref[...], hb, NT, preferred_element_type=F32)
    vt = lax.dot_general(wvt_ref[...], hb, NT, preferred_element_type=F32)
    gmean = _group_matrix(GW, DQ, 1.0 / DQ)

    kn = k * lax.rsqrt(_mm(k * k, gmean, NN, "x3") + EPS) * gk_ref[...]
    first_half = (_iota((1, GW), 1) % DQ) < (DQ // 2)
    partner = jnp.where(first_half, pltpu.roll(kn, GW - DQ // 2, axis=1), pltpu.roll(kn, DQ // 2, axis=1))
    kr = kn * cos_ref[...] + partner * sin_ref[...]
    k_out[...] = jnp.dot(kr.astype(BF16), _place_heads_lanes(GW, HEAD_DIM),
                         preferred_element_type=F32).astype(BF16)

    qn = qt * lax.rsqrt(_mm(gmean, qt * qt, NN, "x3") + EPS) * gqt_ref[...]
    first_half_t = (_iota((GW, 1), 0) % DQ) < (DQ // 2)
    partner_t = jnp.where(first_half_t, pltpu.roll(qn, GW - DQ // 2, axis=0), pltpu.roll(qn, DQ // 2, axis=0))
    qr = (qn * cost_ref[...] + partner_t * sint_ref[...]).astype(BF16)
    qs = _slab_rows(qr, jnp.zeros((SLAB - HEAD_DIM, tm), BF16))
    vb = vt.astype(BF16)
    for c in range(tm // tile):
        q_out[c] = qs[:, c * tile:(c + 1) * tile]
        v_out[c] = vb[:, c * tile:(c + 1) * tile]


def _diff_prep(h, wk, wqt, wvt, gk, gqt, cos, sin, cost, sint, *, seq, tm, tile):
    T, D = h.shape
    per_seq = seq // tm
    nt = tm // tile
    const = lambda i: (0, 0)
    return pl.pallas_call(
        functools.partial(_diff_prep_kernel, tile=tile),
        grid=(T // tm,),
        in_specs=[
            pl.BlockSpec((tm, D), lambda i: (i, 0)),
            pl.BlockSpec((D, GW), const),
            pl.BlockSpec((GW, D), const),
            pl.BlockSpec((GW, D), const),
            pl.BlockSpec((1, GW), const),
            pl.BlockSpec((GW, tm), const),
            pl.BlockSpec((tm, GW), lambda i: (i % per_seq, 0)),
            pl.BlockSpec((tm, GW), lambda i: (i % per_seq, 0)),
            pl.BlockSpec((GW, tm), lambda i: (0, i % per_seq)),
            pl.BlockSpec((GW, tm), lambda i: (0, i % per_seq)),
        ],
        out_specs=[
            pl.BlockSpec((nt, HG * SLAB, tile), lambda i: (i, 0, 0)),
            pl.BlockSpec((tm, HG * SLAB), lambda i: (i, 0)),
            pl.BlockSpec((nt, GW, tile), lambda i: (i, 0, 0)),
        ],
        out_shape=[
            jax.ShapeDtypeStruct((T // tile, HG * SLAB, tile), BF16),
            jax.ShapeDtypeStruct((T, HG * SLAB), BF16),
            jax.ShapeDtypeStruct((T // tile, GW, tile), BF16),
        ],
        compiler_params=_params(("arbitrary",)),
        name="diff_prep",
    )(h, wk, wqt, wvt, gk, gqt, cos, sin, cost, sint)


def _fox_prep_kernel(h_ref, wk_ref, wf_ref, wqt_ref, wvt_ref, gk_ref, gqt_ref, fb_ref,
                     q_out, k_out, v_out, fcarry_ref, *, tile):
    tm = h_ref.shape[0]

    @pl.when(pl.program_id(1) == 0)
    def _():
        fcarry_ref[...] = jnp.zeros_like(fcarry_ref)

    hb = h_ref[...]
    k = jnp.dot(hb, wk_ref[...], preferred_element_type=F32)
    fl = jnp.dot(hb, wf_ref[...], preferred_element_type=F32)
    qt = lax.dot_general(wqt_ref[...], hb, NT, preferred_element_type=F32)
    vt = lax.dot_general(wvt_ref[...], hb, NT, preferred_element_type=F32)
    gmean = _group_matrix(GW, HEAD_DIM, 1.0 / HEAD_DIM)

    kn = (k * lax.rsqrt(_mm(k * k, gmean, NN, "x3") + EPS) * gk_ref[...]).astype(BF16)

    log_f = -_softplus(-(fl + fb_ref[...]))
    l_incl = jnp.where(_iota((tm, tm), 0) >= _iota((tm, tm), 1), 1.0, 0.0).astype(F32)
    cum = _mm(l_incl, log_f, NN, "x3") + fcarry_ref[...]
    fcarry_ref[...] = cum[tm - 1:tm]
    nf = -cum
    pieces = []
    for _ in range(3):
        p = nf.astype(BF16)
        pieces.append(p)
        nf = nf - p.astype(F32)
    src = _iota((LANES, HG * SLAB), 0)
    dst = _iota((LANES, HG * SLAB), 1)
    ks = jnp.dot(kn, _place_heads_lanes(GW, HEAD_DIM), preferred_element_type=F32)
    for i, p in enumerate(pieces):
        place = jnp.where((src < HG) & (dst == src * SLAB + HEAD_DIM + i), 1.0, 0.0).astype(BF16)
        ks = ks + jnp.dot(p, place, preferred_element_type=F32)
    k_out[...] = ks.astype(BF16)

    qn = (qt * lax.rsqrt(_mm(gmean, qt * qt, NN, "x3") + EPS) * gqt_ref[...]).astype(BF16)
    ones_rows = jnp.where(_iota((SLAB - HEAD_DIM, tm), 0) < 3, 1.0, 0.0).astype(BF16)
    qs = _slab_rows(qn, ones_rows)
    vb = vt.astype(BF16)
    for c in range(tm // tile):
        q_out[c] = qs[:, c * tile:(c + 1) * tile]
        v_out[c] = vb[:, c * tile:(c + 1) * tile]


def _fox_prep(h, wk, wf, wqt, wvt, gk, gqt, fb, *, batch, seq, tm, tile):
    T, D = h.shape
    per_seq = seq // tm
    nt = tm // tile
    const = lambda b, i: (0, 0)
    row = lambda b, i: (b * per_seq + i, 0)
    return pl.pallas_call(
        functools.partial(_fox_prep_kernel, tile=tile),
        grid=(batch, per_seq),
        in_specs=[
            pl.BlockSpec((tm, D), row),
            pl.BlockSpec((D, GW), const),
            pl.BlockSpec((D, LANES), const),
            pl.BlockSpec((GW, D), const),
            pl.BlockSpec((GW, D), const),
            pl.BlockSpec((1, GW), const),
            pl.BlockSpec((GW, tm), const),
            pl.BlockSpec((1, LANES), const),
        ],
        out_specs=[
            pl.BlockSpec((nt, HG * SLAB, tile), lambda b, i: (b * per_seq + i, 0, 0)),
            pl.BlockSpec((tm, HG * SLAB), row),
            pl.BlockSpec((nt, GW, tile), lambda b, i: (b * per_seq + i, 0, 0)),
        ],
        out_shape=[
            jax.ShapeDtypeStruct((T // tile, HG * SLAB, tile), BF16),
            jax.ShapeDtypeStruct((T, HG * SLAB), BF16),
            jax.ShapeDtypeStruct((T // tile, GW, tile), BF16),
        ],
        scratch_shapes=[pltpu.VMEM((1, LANES), F32)],
        compiler_params=_params(("arbitrary", "arbitrary")),
        name="fox_prep",
    )(h, wk, wf, wqt, wvt, gk, gqt, fb)


def _attn_kernel(q_ref, k_ref, v_ref, aux_ref, gain_ref, o_ref, *, n_maps, tile, lambda_init):
    qi = pl.program_id(1)
    row = _iota((tile, tile), 0)
    col = _iota((tile, tile), 1)
    if n_maps == 2:
        diag_ok = (row // ATT_CHUNK) <= (col // ATT_CHUNK)
        aux = aux_ref[...]
        lam = (jnp.exp(jnp.sum(aux[0:1] * aux[1:2], axis=1, keepdims=True))
               - jnp.exp(jnp.sum(aux[2:3] * aux[3:4], axis=1, keepdims=True)) + lambda_init)
    else:
        diag_ok = row <= col
    slab_row = _iota((SLAB, 1), 0)

    outs = []
    for h in range(HG):
        qh = q_ref[h * SLAB:(h + 1) * SLAB, :]
        if n_maps == 2:
            qms = [jnp.where(slab_row < DQ, qh, jnp.zeros_like(qh)),
                   jnp.where((slab_row >= DQ) & (slab_row < 2 * DQ), qh, jnp.zeros_like(qh))]
        else:
            qms = [qh]

        def step(j, carry, masked):
            kj = k_ref[pl.ds(pl.multiple_of(j * tile, tile), tile), h * SLAB:(h + 1) * SLAB]
            vj = v_ref[j, h * HEAD_DIM:(h + 1) * HEAD_DIM, :]
            new = []
            for qm, (m, l, acc) in zip(qms, carry):
                s = jnp.dot(kj, qm, preferred_element_type=F32)
                if masked:
                    s = jnp.where(diag_ok, s, NEG_INF)
                m_new = jnp.maximum(m, jnp.max(s, axis=0, keepdims=True))
                alpha = jnp.exp(m - m_new)
                p = jnp.exp(s - m_new)
                l_new = alpha * l + jnp.sum(p, axis=0, keepdims=True)
                acc_new = alpha * acc + jnp.dot(vj, p.astype(BF16), preferred_element_type=F32)
                new.append((m_new, l_new, acc_new))
            return tuple(new)

        init = tuple((jnp.full((1, tile), NEG_INF, F32), jnp.zeros((1, tile), F32),
                      jnp.zeros((HEAD_DIM, tile), F32)) for _ in qms)
        carry = lax.fori_loop(0, qi, functools.partial(step, masked=False), init)
        carry = step(qi, carry, True)
        o = carry[0][2] / carry[0][1]
        if n_maps == 2:
            o = o - lam * (carry[1][2] / carry[1][1])
            o = o * lax.rsqrt(jnp.mean(o * o, axis=0, keepdims=True) + EPS)
        outs.append(o)
    ot = jnp.concatenate(outs, axis=0)
    if n_maps == 2:
        ot = ot * gain_ref[...]
    o_ref[...] = ot.T.astype(BF16)


def _attention(q3, k2, v3, aux, gain, *, batch, seq, tile, n_maps, lambda_init):
    nq = seq // tile
    T = batch * seq
    return pl.pallas_call(
        functools.partial(_attn_kernel, n_maps=n_maps, tile=tile, lambda_init=lambda_init),
        grid=(batch, nq),
        in_specs=[
            pl.BlockSpec((None, HG * SLAB, tile), lambda b, i: (b * nq + i, 0, 0)),
            pl.BlockSpec((seq, HG * SLAB), lambda b, i: (b, 0)),
            pl.BlockSpec((nq, GW, tile), lambda b, i: (b, 0, 0)),
            pl.BlockSpec(aux.shape, lambda b, i: (0, 0)),
            pl.BlockSpec(gain.shape, lambda b, i: (0, 0)),
        ],
        out_specs=pl.BlockSpec((tile, GW), lambda b, i: (b * nq + i, 0)),
        out_shape=jax.ShapeDtypeStruct((T, GW), BF16),
        compiler_params=_params(("arbitrary", "arbitrary")),
        name="diff_attention" if n_maps == 2 else "forgetting_attention",
    )(q3, k2, v3, aux, gain)


def _erf_gelu(x):
    return 0.5 * x * (1.0 + lax.erf(x * (1.0 / math.sqrt(2.0))))


def _gmlp_kernel(h_ref, wc_ref, sgw_ref, bias_ref, ln_ref, o_ref):
    tm = h_ref.shape[0]
    pc = _erf_gelu(jnp.dot(h_ref[...], wc_ref[...], preferred_element_type=F32))
    u = pc[:, 0:GW]
    v = pc[:, GW:2 * GW]
    mu = jnp.mean(v, axis=-1, keepdims=True)
    vc = v - mu
    var = jnp.mean(vc * vc, axis=-1, keepdims=True)
    ln = ln_ref[...]
    vn = (vc * lax.rsqrt(var + EPS) * ln[0:1] + ln[1:2]).astype(BF16)

    causal = _iota((SG_CHUNK, SG_CHUNK), 0) >= _iota((SG_CHUNK, SG_CHUNK), 1)
    w = jnp.concatenate([jnp.where(causal, sgw_ref[g], 0.0) for g in range(HG)], axis=0).astype(BF16)
    lane_head = _iota((1, GW), 1) // HEAD_DIM
    bias = bias_ref[...]
    for n in range(tm // SG_CHUNK):
        rows = slice(n * SG_CHUNK, (n + 1) * SG_CHUNK)
        full = jnp.dot(w, vn[rows], preferred_element_type=F32)
        sv = bias
        for g in range(HG):
            sv = sv + jnp.where(lane_head == g, full[g * SG_CHUNK:(g + 1) * SG_CHUNK], 0.0)
        o_ref[rows, :] = (u[rows] * sv).astype(BF16)


def _gmlp_mixer(h, wc, sg_w, bias, ln, *, tm):
    T, D = h.shape
    return pl.pallas_call(
        _gmlp_kernel,
        grid=(T // tm,),
        in_specs=[
            pl.BlockSpec((tm, D), lambda i: (i, 0)),
            pl.BlockSpec((D, 2 * GW), lambda i: (0, 0)),
            pl.BlockSpec((HG, SG_CHUNK, SG_CHUNK), lambda i: (0, 0, 0)),
            pl.BlockSpec((SG_CHUNK, GW), lambda i: (0, 0)),
            pl.BlockSpec((2, GW), lambda i: (0, 0)),
        ],
        out_specs=pl.BlockSpec((tm, GW), lambda i: (i, 0)),
        out_shape=jax.ShapeDtypeStruct((T, GW), BF16),
        compiler_params=_params(("arbitrary",)),
        name="gmlp_mixer",
    )(h, wc, sg_w, bias, ln)


def _out_proj_kernel(ya_ref, yb_ref, yc_ref, yd_ref, w_ref, x_ref, mod_ref, g_ref, xo_ref, h_ref):
    acc = jnp.dot(ya_ref[...], w_ref[0], preferred_element_type=F32)
    acc = acc + jnp.dot(yb_ref[...], w_ref[1], preferred_element_type=F32)
    acc = acc + jnp.dot(yc_ref[...], w_ref[2], preferred_element_type=F32)
    acc = acc + jnp.dot(yd_ref[...], w_ref[3], preferred_element_type=F32)
    mod = mod_ref[...]
    x = x_ref[...] + mod[2:3] * acc
    xo_ref[...] = x
    h_ref[...] = _modulated_norm(x, g_ref[...], mod[3:4], mod[4:5]).astype(BF16)


def _out_proj(ys, w_out4, x2, mod_l, g2, *, seq, tm):
    T, D = x2.shape
    per_seq = seq // tm
    ymap = pl.BlockSpec((tm, GW), lambda i: (i, 0))
    return pl.pallas_call(
        _out_proj_kernel,
        grid=(T // tm,),
        in_specs=[ymap, ymap, ymap, ymap,
                  pl.BlockSpec((N_MIXERS, GW, D), lambda i: (0, 0, 0)),
                  pl.BlockSpec((tm, D), lambda i: (i, 0)),
                  pl.BlockSpec((None, 6, D), lambda i: (i // per_seq, 0, 0)),
                  pl.BlockSpec((1, D), lambda i: (0, 0))],
        out_specs=[pl.BlockSpec((tm, D), lambda i: (i, 0)), pl.BlockSpec((tm, D), lambda i: (i, 0))],
        out_shape=[jax.ShapeDtypeStruct((T, D), F32), jax.ShapeDtypeStruct((T, D), BF16)],
        compiler_params=_params(("arbitrary",)),
        name="out_proj",
    )(*ys, w_out4, x2, mod_l, g2.reshape(1, D))


HALO = 16


def _ffn_kernel(hh_ref, h_ref, x_ref, wup_ref, cw_ref, cb_ref, wdn_ref, mod_ref, o_ref, acc_ref,
                *, per_seq, n_fc):
    i = pl.program_id(0)
    j = pl.program_id(1)
    tm = h_ref.shape[0]
    fc = wdn_ref.shape[0]

    hx = jnp.concatenate([hh_ref[...], h_ref[...]], axis=0)
    seq_start = (i % per_seq) == 0
    halo_row = _iota((HALO + tm, 1), 0) < HALO

    def conv_half(half):
        u = jnp.dot(hx, wup_ref[half], preferred_element_type=F32)
        u = jnp.where(halo_row & seq_start, 0.0, u)
        cw = cw_ref[half]
        y = (cw[0:1] * pltpu.roll(u, 2, axis=0) + cw[1:2] * pltpu.roll(u, 1, axis=0) + cw[2:3] * u)
        return y[HALO:] + cb_ref[half]

    a = conv_half(0)
    g = conv_half(1)
    act = (a * (g * _sigmoid(g))).astype(BF16)
    part = jnp.dot(act, wdn_ref[...], preferred_element_type=F32)

    @pl.when(j == 0)
    def _():
        acc_ref[...] = part

    @pl.when(j > 0)
    def _():
        acc_ref[...] += part

    @pl.when(j == n_fc - 1)
    def _():
        o_ref[...] = x_ref[...] + mod_ref[...][5:6] * acc_ref[...]


def _ffn(h2, x2, wup, cw, cb, wdn, mod_l, *, seq, tm, fc):
    T, D = x2.shape
    dff = wdn.shape[0]
    n_fc = dff // fc
    per_seq = seq // tm
    halo_blocks = tm // HALO
    return pl.pallas_call(
        functools.partial(_ffn_kernel, per_seq=per_seq, n_fc=n_fc),
        grid=(T // tm, n_fc),
        in_specs=[
            pl.BlockSpec((HALO, D), lambda i, j: (jnp.maximum(i * halo_blocks - 1, 0), 0)),
            pl.BlockSpec((tm, D), lambda i, j: (i, 0)),
            pl.BlockSpec((tm, D), lambda i, j: (i, 0)),
            pl.BlockSpec((2, D, fc), lambda i, j: (0, 0, j)),
            pl.BlockSpec((2, 3, fc), lambda i, j: (0, 0, j)),
            pl.BlockSpec((2, 1, fc), lambda i, j: (0, 0, j)),
            pl.BlockSpec((fc, D), lambda i, j: (j, 0)),
            pl.BlockSpec((None, 6, D), lambda i, j: (i // per_seq, 0, 0)),
        ],
        out_specs=pl.BlockSpec((tm, D), lambda i, j: (i, 0)),
        out_shape=jax.ShapeDtypeStruct((T, D), F32),
        scratch_shapes=[pltpu.VMEM((tm, D), F32)],
        compiler_params=_params(("arbitrary", "arbitrary")),
        name="conv_glu_ffn",
    )(h2, h2, x2, wup, cw, cb, wdn, mod_l)


def _rope_tables(seq):
    inv = 1.0 / (ROPE_THETA ** (jnp.arange(0, DQ, 2, dtype=F32) / DQ))
    ang = jnp.arange(seq, dtype=F32)[:, None] * inv[None, :]
    cos = jnp.cos(ang)
    sin = jnp.sin(ang)
    cos_map = jnp.concatenate([cos, cos], axis=-1)
    sin_map = jnp.concatenate([-sin, sin], axis=-1)
    reps = GW // DQ
    return jnp.tile(cos_map, (1, reps)), jnp.tile(sin_map, (1, reps))


def kernel(x, c, ada_w, ada_b, norm1_g, norm2_g, w_in, w_out, rw_mu, rw_w0, rw_w_up, rw_a0, rw_a_up, rw_g_up, rw_k_k, rw_k_a, rw_r_k, rw_ln_g, rw_ln_b, df_lam_q1, df_lam_k1, df_lam_q2, df_lam_k2, df_q_g, df_k_g, df_sub_g, sg_w, sg_b, sg_ln_g, sg_ln_b, fx_q_g, fx_k_g, fx_f_b, ffn_up, ffn_conv, ffn_conv_b, ffn_down):
    Bn, S, D = x.shape
    L = ada_w.shape[0]
    T = Bn * S
    dff = ffn_down.shape[1]
    tm = min(ROW_TILE, S)
    tile = min(ATT_TILE, S)
    fc = dff // 2 if (dff // 2) % LANES == 0 else dff

    mod = _ada_mod(c, ada_w, ada_b).reshape(L, Bn, 6, D)
    cos, sin = _rope_tables(S)
    cost, sint = cos.T, sin.T
    x2 = x.reshape(T, D)

    oa = 0
    ob = oa + RW_COLS
    oc = ob + 3 * GW
    od = oc + 2 * GW

    for l in range(L):
        lambda_init = 0.8 - 0.6 * math.exp(-0.3 * l)
        wl = w_in[l].astype(BF16)
        h = _norm_mod(x2, mod[l], norm1_g[l], seq=S, row0=0, tm=tm)

        lora_w = jnp.zeros((LANES, 3 * GW), F32)
        lora_w = lora_w.at[0:RW_DECAY_RANK, 0:GW].set(rw_w_up[l])
        lora_w = lora_w.at[RW_DECAY_RANK:RW_DECAY_RANK + RW_A_RANK, GW:2 * GW].set(rw_a_up[l])
        lora_w = lora_w.at[RW_DECAY_RANK + RW_A_RANK:LANES, 2 * GW:3 * GW].set(rw_g_up[l])
        vec = jnp.stack([rw_w0[l], rw_a0[l], rw_k_k[l], rw_k_a[l], rw_r_k[l].reshape(GW),
                         rw_ln_g[l], rw_ln_b[l], jnp.zeros((GW,), F32)])
        ya = _rwkv_mixer(h, wl[:, oa:ob], rw_mu[l].reshape(1, RW_COLS), lora_w, vec,
                         batch=Bn, seq=S, tt=tm, mode="hi")

        gk = jnp.tile(df_k_g[l], GW // DQ).reshape(1, GW)
        gqt = jnp.broadcast_to((jnp.tile(df_q_g[l], GW // DQ) * DQ ** -0.5)[:, None], (GW, tm))
        qb, kb, vb = _diff_prep(h, wl[:, ob + GW:ob + 2 * GW], wl[:, ob:ob + GW].T, wl[:, ob + 2 * GW:oc].T,
                                gk, gqt, cos, sin, cost, sint, seq=S, tm=tm, tile=tile)
        lam_vecs = jnp.stack([df_lam_q1[l], df_lam_k1[l], df_lam_q2[l], df_lam_k2[l]])
        sub_gain = jnp.broadcast_to((jnp.tile(df_sub_g[l], HG) * (1.0 - lambda_init))[:, None], (GW, tile))
        yb = _attention(qb, kb, vb, lam_vecs, sub_gain, batch=Bn, seq=S, tile=tile,
                        n_maps=2, lambda_init=lambda_init)

        sg_bias = jnp.repeat(sg_b[l].T, HEAD_DIM, axis=1)
        yc = _gmlp_mixer(h, wl[:, oc:od], sg_w[l], sg_bias,
                         jnp.stack([sg_ln_g[l], sg_ln_b[l]]), tm=tm)

        wf = jnp.zeros((D, LANES), BF16).at[:, 0:HG].set(wl[:, od + 3 * GW:od + 3 * GW + HG])
        fb = jnp.zeros((1, LANES), F32).at[0, 0:HG].set(fx_f_b[l])
        gkd = jnp.tile(fx_k_g[l], HG).reshape(1, GW)
        gqd = jnp.broadcast_to((jnp.tile(fx_q_g[l], HG) * HEAD_DIM ** -0.5)[:, None], (GW, tm))
        qd, kd, vd = _fox_prep(h, wl[:, od + GW:od + 2 * GW], wf, wl[:, od:od + GW].T,
                               wl[:, od + 2 * GW:od + 3 * GW].T, gkd, gqd, fb,
                               batch=Bn, seq=S, tm=tm, tile=tile)
        yd = _attention(qd, kd, vd, jnp.zeros((8, LANES), F32), jnp.zeros((8, LANES), F32),
                        batch=Bn, seq=S, tile=tile, n_maps=1, lambda_init=0.0)

        x2, h2 = _out_proj((ya, yb, yc, yd), w_out[l].astype(BF16).reshape(N_MIXERS, GW, D),
                           x2, mod[l], norm2_g[l], seq=S, tm=tm)

        wup = ffn_up[l].astype(BF16).reshape(D, 2, dff).transpose(1, 0, 2)
        cw = ffn_conv[l].reshape(3, 2, dff).transpose(1, 0, 2)
        cb = ffn_conv_b[l].reshape(2, 1, dff)
        x2 = _ffn(h2, x2, wup, cw, cb, ffn_down[l].astype(BF16), mod[l], seq=S, tm=tm, fc=fc)

    return x2.reshape(Bn, S, D)
```

```python
import functools
import math

import jax
import jax.numpy as jnp
from jax import lax
from jax.experimental import pallas as pl
from jax.experimental.pallas import tpu as pltpu

F32 = jnp.float32
BF16 = jnp.bfloat16
HIGHEST = lax.Precision.HIGHEST

N_MIXERS = 4
HEAD_DIM = 64
HG = 4
GW = HG * HEAD_DIM
DQ = HEAD_DIM // 2
RW_DECAY_RANK = 32
RW_A_RANK = 32
RW_GATE_RANK = 64
RW_COLS = 3 * GW + RW_DECAY_RANK + RW_A_RANK + RW_GATE_RANK
RW_GN_EPS = 64e-5
SG_CHUNK = 128
ATT_CHUNK = 64
ROPE_THETA = 10000.0
EPS = 1e-6
NEG_INF = -1e30

LANES = 128
V7X_VMEM_BYTES = 64 * 1024 * 1024
VMEM_LIMIT = 56 * 1024 * 1024

ROW_TILE = 512
RW_CHUNK = 64
ATT_TILE = 256
SLAB = 128
RW_MODE = "lo"

NN = (((1,), (0,)), ((), ()))
NT = (((1,), (1,)), ((), ()))
TN = (((0,), (0,)), ((), ()))


def _params(sem):
    return pltpu.CompilerParams(dimension_semantics=sem, vmem_limit_bytes=VMEM_LIMIT)


def _mm(a, b, dims=NN, mode="hi"):
    if mode == "hi":
        return lax.dot_general(a.astype(F32), b.astype(F32), dims, precision=HIGHEST,
                               preferred_element_type=F32)
    if mode == "lo":
        return lax.dot_general(a.astype(BF16), b.astype(BF16), dims, preferred_element_type=F32)
    ah = a.astype(BF16)
    al = (a - ah.astype(F32)).astype(BF16)
    bh = b.astype(BF16)
    bl = (b - bh.astype(F32)).astype(BF16)
    dg = functools.partial(lax.dot_general, dimension_numbers=dims, preferred_element_type=F32)
    return dg(ah, bh) + (dg(ah, bl) + dg(al, bh))


def _mm_exact(a, b, dims=NN, left=True):
    x = a if left else b
    xh = x.astype(BF16)
    xl = (x - xh.astype(F32)).astype(BF16)
    dg = functools.partial(lax.dot_general, dimension_numbers=dims, preferred_element_type=F32)
    if left:
        bb = b.astype(BF16)
        return dg(xh, bb) + dg(xl, bb)
    ab = a.astype(BF16)
    return dg(ab, xh) + dg(ab, xl)


def _iota(shape, dim):
    return lax.broadcasted_iota(jnp.int32, shape, dim)


def _group_matrix(n, group, value=1.0):
    same = (_iota((n, n), 0) // group) == (_iota((n, n), 1) // group)
    return jnp.where(same, value, 0.0).astype(F32)


def _softplus(x):
    return jnp.maximum(x, 0.0) + jnp.log1p(jnp.exp(-jnp.abs(x)))


def _sigmoid(x):
    return 1.0 / (1.0 + jnp.exp(-x))


def _ada_kernel(c_ref, w_ref, b_ref, o_ref):
    c = c_ref[...]
    cond = c * _sigmoid(c)
    o_ref[...] = _mm(cond, w_ref[...], NN, "hi") + b_ref[...]


def _ada_mod(c, ada_w, ada_b):
    L, D, D6 = ada_w.shape
    Bn = c.shape[0]
    tn = D6 // 4
    return pl.pallas_call(
        _ada_kernel,
        grid=(L, D6 // tn),
        in_specs=[
            pl.BlockSpec((Bn, D), lambda l, j: (0, 0)),
            pl.BlockSpec((None, D, tn), lambda l, j: (l, 0, j)),
            pl.BlockSpec((None, 1, tn), lambda l, j: (l, 0, j)),
        ],
        out_specs=pl.BlockSpec((None, Bn, tn), lambda l, j: (l, 0, j)),
        out_shape=jax.ShapeDtypeStruct((L, Bn, D6), F32),
        compiler_params=_params(("arbitrary", "arbitrary")),
        name="ada_mod",
    )(c, ada_w, ada_b.reshape(L, 1, D6))


def _modulated_norm(x, g, shift, scale):
    y = x * lax.rsqrt(jnp.mean(x * x, axis=-1, keepdims=True) + EPS) * g
    return y * (1.0 + scale) + shift


def _norm_kernel(x_ref, mod_ref, g_ref, o_ref, *, row0):
    mod = mod_ref[...]
    h = _modulated_norm(x_ref[...], g_ref[...], mod[row0:row0 + 1], mod[row0 + 1:row0 + 2])
    o_ref[...] = h.astype(BF16)


def _norm_mod(x2, mod_l, g, *, seq, row0, tm):
    T, D = x2.shape
    per_seq = seq // tm
    return pl.pallas_call(
        functools.partial(_norm_kernel, row0=row0),
        grid=(T // tm,),
        in_specs=[
            pl.BlockSpec((tm, D), lambda i: (i, 0)),
            pl.BlockSpec((None, 6, D), lambda i: (i // per_seq, 0, 0)),
            pl.BlockSpec((1, D), lambda i: (0, 0)),
        ],
        out_specs=pl.BlockSpec((tm, D), lambda i: (i, 0)),
        out_shape=jax.ShapeDtypeStruct((T, D), BF16),
        compiler_params=_params(("arbitrary",)),
        name="norm_mod",
    )(x2, mod_l, g.reshape(1, D))


def _blockdiag(mp, head_lane_masks):
    return jnp.concatenate([jnp.where(m, mp, 0.0) for m in head_lane_masks], axis=0)


def _apply_packed(mp, rhs, head_lane_masks, C, mode):
    full = _mm(mp, rhs, TN, mode)
    out = jnp.where(head_lane_masks[0], full[0:C], 0.0)
    for h in range(1, HG):
        out = out + jnp.where(head_lane_masks[h], full[h * C:(h + 1) * C], 0.0)
    return out


def _rwkv_kernel(h_ref, wa_ref, mu_ref, lora_ref, vec_ref, o_ref,
                 carry_ref, st_ref, r_s, lw_s, k_s, v_s, a_s, b_s, y_s, *, C, mode):
    TT = h_ref.shape[0]

    @pl.when(pl.program_id(1) == 0)
    def _():
        carry_ref[...] = jnp.zeros_like(carry_ref)
        st_ref[...] = jnp.zeros_like(st_ref)

    vec = vec_ref[...]
    w0, a0, k_k, k_a, r_k, ln_g, ln_b = (vec[i:i + 1] for i in range(7))

    pa = jnp.dot(h_ref[...], wa_ref[...], preferred_element_type=F32)
    prev = pltpu.roll(pa, 1, axis=0)
    prev = jnp.where(_iota((TT, 1), 0) == 0, carry_ref[...], prev)
    carry_ref[...] = pa[TT - 1:TT]
    pa = pa + (prev - pa) * mu_ref[...]

    r = pa[:, 0:GW]
    k = pa[:, GW:2 * GW]
    v = pa[:, 2 * GW:3 * GW]
    lo = pa[:, 3 * GW:3 * GW + LANES]
    lane = _iota((1, LANES), 1)
    act = jnp.where(lane < RW_DECAY_RANK, jnp.tanh(lo),
                    jnp.where(lane < RW_DECAY_RANK + RW_A_RANK, lo, _sigmoid(lo)))
    lora = _mm(act, lora_ref[...], NN, "x3")
    w_log = -_softplus(-(w0 + lora[:, 0:GW])) - 0.5
    lw = -jnp.exp(w_log)
    a = _sigmoid(a0 + lora[:, GW:2 * GW])
    gate = lora[:, 2 * GW:3 * GW]

    gsum = _group_matrix(GW, HEAD_DIM)
    kk = k * k_k
    kk = kk / jnp.maximum(jnp.sqrt(_mm_exact(kk * kk, gsum)), 1e-12)
    k2 = k * (1.0 + (a - 1.0) * k_a)
    bonus = _mm_exact(r * k2 * r_k, gsum) * v

    r_s[...] = r
    lw_s[...] = lw
    k_s[...] = k2
    v_s[...] = v
    a_s[...] = -kk
    b_s[...] = kk * a

    lane_head = _iota((1, GW), 1) // HEAD_DIM
    hmask = [lane_head == h for h in range(HG)]
    s_idx = _iota((C, GW), 0)
    t_idx = _iota((C, GW), 1) % C
    strict = s_idx < t_idx
    incl = s_idx <= t_idx
    eye_p = jnp.where(s_idx == t_idx, 1.0, 0.0).astype(F32)
    l_incl = jnp.where(_iota((C, C), 0) >= _iota((C, C), 1), 1.0, 0.0).astype(F32)
    bd = (_iota((GW, GW), 0) // HEAD_DIM) == (_iota((GW, GW), 1) // HEAD_DIM)

    def chunk(c, carry):
        sl = pl.ds(pl.multiple_of(c * C, C), C)
        lwc = lw_s[sl, :]
        cl = _mm_exact(l_incl, lwc, left=False)
        g_in = jnp.exp(cl)
        g_inv = jnp.exp(-cl)
        at = a_s[sl, :] * jnp.exp(cl - lwc)
        rt = r_s[sl, :] * g_in
        bt = b_s[sl, :] * g_inv
        kt = k_s[sl, :] * g_inv
        vc = v_s[sl, :]

        ar = jnp.concatenate([_blockdiag(at, hmask), _blockdiag(rt, hmask)], axis=0)
        bk = jnp.concatenate([bt, kt], axis=0)
        x = _mm(bk, ar, NT, mode)
        n_p = jnp.where(strict, x[0:C, 0:GW], 0.0)
        ak_p = jnp.where(strict, x[C:2 * C, 0:GW], 0.0)
        rb_p = jnp.where(incl, x[0:C, GW:2 * GW], 0.0)
        rk_p = jnp.where(incl, x[C:2 * C, GW:2 * GW], 0.0)

        p_p = eye_p + n_p
        q_p = n_p
        for _ in range(int(math.log2(C)) - 1):
            q_p = _mm(q_p, _blockdiag(q_p, hmask), NN, mode)
            p_p = p_p + _mm(p_p, _blockdiag(q_p, hmask), NN, mode)

        ht = st_ref[...]
        sh = _mm(jnp.concatenate([at, rt], axis=0), ht, NT, mode)
        rhs = sh[0:C] + _apply_packed(ak_p, vc, hmask, C, mode)
        u = _apply_packed(p_p, rhs, hmask, C, mode)
        uv = jnp.concatenate([u, vc], axis=0)
        y = sh[C:2 * C] + _apply_packed(jnp.concatenate([rb_p, rk_p], axis=0), uv, hmask, C, mode)
        y_s[sl, :] = y
        upd = _mm(uv, bk, TN, mode)
        st_ref[...] = (ht + jnp.where(bd, upd, 0.0)) * g_in[C - 1:C, :]
        return carry

    lax.fori_loop(0, TT // C, chunk, 0)

    gmean = _group_matrix(GW, HEAD_DIM, 1.0 / HEAD_DIM)
    y = y_s[...]
    yc = y - _mm_exact(y, gmean)
    yn = yc * lax.rsqrt(_mm_exact(yc * yc, gmean) + RW_GN_EPS) * ln_g + ln_b
    o_ref[...] = ((yn + bonus) * gate).astype(BF16)


def _rwkv_mixer(h, wa, mu, lora_w, vec, *, batch, seq, tt, mode):
    T, D = h.shape
    per_seq = seq // tt
    C = RW_CHUNK
    big = pltpu.VMEM((tt, GW), F32)
    return pl.pallas_call(
        functools.partial(_rwkv_kernel, C=C, mode=mode),
        grid=(batch, per_seq),
        in_specs=[
            pl.BlockSpec((tt, D), lambda b, i: (b * per_seq + i, 0)),
            pl.BlockSpec((D, RW_COLS), lambda b, i: (0, 0)),
            pl.BlockSpec((1, RW_COLS), lambda b, i: (0, 0)),
            pl.BlockSpec((LANES, 3 * GW), lambda b, i: (0, 0)),
            pl.BlockSpec((8, GW), lambda b, i: (0, 0)),
        ],
        out_specs=pl.BlockSpec((tt, GW), lambda b, i: (b * per_seq + i, 0)),
        out_shape=jax.ShapeDtypeStruct((T, GW), BF16),
        scratch_shapes=[pltpu.VMEM((1, RW_COLS), F32), pltpu.VMEM((GW, GW), F32)] + [big] * 7,
        compiler_params=_params(("arbitrary", "arbitrary")),
        name="rwkv7_mixer",
    )(h, wa, mu, lora_w, vec)


def _place_heads_lanes(n_in, width):
    src = _iota((n_in, HG * SLAB), 0)
    dst = _iota((n_in, HG * SLAB), 1)
    ok = ((dst // SLAB) == (src // width)) & ((dst % SLAB) == (src % width)) & ((dst % SLAB) < width)
    return jnp.where(ok, 1.0, 0.0).astype(BF16)


def _slab_rows(xt, extra):
    parts = []
    for h in range(HG):
        parts.append(xt[h * HEAD_DIM:(h + 1) * HEAD_DIM])
        parts.append(extra)
    return jnp.concatenate(parts, axis=0)


def _diff_prep_kernel(h_ref, wk_ref, wqt_ref, wvt_ref, gk_ref, gqt_ref,
                      cos_ref, sin_ref, cost_ref, sint_ref, q_out, k_out, v_out, *, tile):
    tm = h_ref.shape[0]
    hb = h_ref[...]
    k = jnp.dot(hb, wk_ref[...], preferred_element_type=F32)
    qt = lax.dot_general(wqt_ref[...], hb, NT, preferred_element_type=F32)
    vt = lax.dot_general(wvt_ref[...], hb, NT, preferred_element_type=F32)
    gmean = _group_matrix(GW, DQ, 1.0 / DQ)

    kn = k * lax.rsqrt(_mm_exact(k * k, gmean) + EPS) * gk_ref[...]
    first_half = (_iota((1, GW), 1) % DQ) < (DQ // 2)
    partner = jnp.where(first_half, pltpu.roll(kn, GW - DQ // 2, axis=1), pltpu.roll(kn, DQ // 2, axis=1))
    kr = kn * cos_ref[...] + partner * sin_ref[...]
    k_out[...] = jnp.dot(kr.astype(BF16), _place_heads_lanes(GW, HEAD_DIM),
                         preferred_element_type=F32).astype(BF16)

    qn = qt * lax.rsqrt(_mm_exact(gmean, qt * qt, left=False) + EPS) * gqt_ref[...]
    first_half_t = (_iota((GW, 1), 0) % DQ) < (DQ // 2)
    partner_t = jnp.where(first_half_t, pltpu.roll(qn, GW - DQ // 2, axis=0), pltpu.roll(qn, DQ // 2, axis=0))
    qr = (qn * cost_ref[...] + partner_t * sint_ref[...]).astype(BF16)
    qs = _slab_rows(qr, jnp.zeros((SLAB - HEAD_DIM, tm), BF16))
    vb = vt.astype(BF16)
    for c in range(tm // tile):
        q_out[c] = qs[:, c * tile:(c + 1) * tile]
        v_out[c] = vb[:, c * tile:(c + 1) * tile]


def _diff_prep(h, wk, wqt, wvt, gk, gqt, cos, sin, cost, sint, *, seq, tm, tile):
    T, D = h.shape
    per_seq = seq // tm
    nt = tm // tile
    const = lambda i: (0, 0)
    return pl.pallas_call(
        functools.partial(_diff_prep_kernel, tile=tile),
        grid=(T // tm,),
        in_specs=[
            pl.BlockSpec((tm, D), lambda i: (i, 0)),
            pl.BlockSpec((D, GW), const),
            pl.BlockSpec((GW, D), const),
            pl.BlockSpec((GW, D), const),
            pl.BlockSpec((1, GW), const),
            pl.BlockSpec((GW, tm), const),
            pl.BlockSpec((tm, GW), lambda i: (i % per_seq, 0)),
            pl.BlockSpec((tm, GW), lambda i: (i % per_seq, 0)),
            pl.BlockSpec((GW, tm), lambda i: (0, i % per_seq)),
            pl.BlockSpec((GW, tm), lambda i: (0, i % per_seq)),
        ],
        out_specs=[
            pl.BlockSpec((nt, HG * SLAB, tile), lambda i: (i, 0, 0)),
            pl.BlockSpec((tm, HG * SLAB), lambda i: (i, 0)),
            pl.BlockSpec((nt, GW, tile), lambda i: (i, 0, 0)),
        ],
        out_shape=[
            jax.ShapeDtypeStruct((T // tile, HG * SLAB, tile), BF16),
            jax.ShapeDtypeStruct((T, HG * SLAB), BF16),
            jax.ShapeDtypeStruct((T // tile, GW, tile), BF16),
        ],
        compiler_params=_params(("arbitrary",)),
        name="diff_prep",
    )(h, wk, wqt, wvt, gk, gqt, cos, sin, cost, sint)


def _fox_prep_kernel(h_ref, wk_ref, wf_ref, wqt_ref, wvt_ref, gk_ref, gqt_ref, fb_ref,
                     q_out, k_out, v_out, fcarry_ref, *, tile):
    tm = h_ref.shape[0]

    @pl.when(pl.program_id(1) == 0)
    def _():
        fcarry_ref[...] = jnp.zeros_like(fcarry_ref)

    hb = h_ref[...]
    k = jnp.dot(hb, wk_ref[...], preferred_element_type=F32)
    fl = jnp.dot(hb, wf_ref[...], preferred_element_type=F32)
    qt = lax.dot_general(wqt_ref[...], hb, NT, preferred_element_type=F32)
    vt = lax.dot_general(wvt_ref[...], hb, NT, preferred_element_type=F32)
    gmean = _group_matrix(GW, HEAD_DIM, 1.0 / HEAD_DIM)

    kn = (k * lax.rsqrt(_mm_exact(k * k, gmean) + EPS) * gk_ref[...]).astype(BF16)

    log_f = -_softplus(-(fl + fb_ref[...]))
    l_incl = jnp.where(_iota((tm, tm), 0) >= _iota((tm, tm), 1), 1.0, 0.0).astype(F32)
    cum = _mm_exact(l_incl, log_f, left=False) + fcarry_ref[...]
    fcarry_ref[...] = cum[tm - 1:tm]
    nf = -cum
    pieces = []
    for _ in range(3):
        p = nf.astype(BF16)
        pieces.append(p)
        nf = nf - p.astype(F32)
    src = _iota((LANES, HG * SLAB), 0)
    dst = _iota((LANES, HG * SLAB), 1)
    ks = jnp.dot(kn, _place_heads_lanes(GW, HEAD_DIM), preferred_element_type=F32)
    for i, p in enumerate(pieces):
        place = jnp.where((src < HG) & (dst == src * SLAB + HEAD_DIM + i), 1.0, 0.0).astype(BF16)
        ks = ks + jnp.dot(p, place, preferred_element_type=F32)
    k_out[...] = ks.astype(BF16)

    qn = (qt * lax.rsqrt(_mm_exact(gmean, qt * qt, left=False) + EPS) * gqt_ref[...]).astype(BF16)
    ones_rows = jnp.where(_iota((SLAB - HEAD_DIM, tm), 0) < 3, 1.0, 0.0).astype(BF16)
    qs = _slab_rows(qn, ones_rows)
    vb = vt.astype(BF16)
    for c in range(tm // tile):
        q_out[c] = qs[:, c * tile:(c + 1) * tile]
        v_out[c] = vb[:, c * tile:(c + 1) * tile]


def _fox_prep(h, wk, wf, wqt, wvt, gk, gqt, fb, *, batch, seq, tm, tile):
    T, D = h.shape
    per_seq = seq // tm
    nt = tm // tile
    const = lambda b, i: (0, 0)
    row = lambda b, i: (b * per_seq + i, 0)
    return pl.pallas_call(
        functools.partial(_fox_prep_kernel, tile=tile),
        grid=(batch, per_seq),
        in_specs=[
            pl.BlockSpec((tm, D), row),
            pl.BlockSpec((D, GW), const),
            pl.BlockSpec((D, LANES), const),
            pl.BlockSpec((GW, D), const),
            pl.BlockSpec((GW, D), const),
            pl.BlockSpec((1, GW), const),
            pl.BlockSpec((GW, tm), const),
            pl.BlockSpec((1, LANES), const),
        ],
        out_specs=[
            pl.BlockSpec((nt, HG * SLAB, tile), lambda b, i: (b * per_seq + i, 0, 0)),
            pl.BlockSpec((tm, HG * SLAB), row),
            pl.BlockSpec((nt, GW, tile), lambda b, i: (b * per_seq + i, 0, 0)),
        ],
        out_shape=[
            jax.ShapeDtypeStruct((T // tile, HG * SLAB, tile), BF16),
            jax.ShapeDtypeStruct((T, HG * SLAB), BF16),
            jax.ShapeDtypeStruct((T // tile, GW, tile), BF16),
        ],
        scratch_shapes=[pltpu.VMEM((1, LANES), F32)],
        compiler_params=_params(("arbitrary", "arbitrary")),
        name="fox_prep",
    )(h, wk, wf, wqt, wvt, gk, gqt, fb)


def _attn_kernel(q_ref, k_ref, v_ref, aux_ref, gain_ref, o_ref, qz_s, m_s, l_s, acc_s,
                 *, n_maps, tile, lambda_init):
    qi = pl.program_id(1)
    row = _iota((tile, tile), 0)
    col = _iota((tile, tile), 1)
    if n_maps == 2:
        diag_ok = (row // ATT_CHUNK) <= (col // ATT_CHUNK)
    else:
        diag_ok = row <= col

    slab_row = _iota((SLAB, 1), 0)
    for h in range(HG):
        qh = q_ref[h * SLAB:(h + 1) * SLAB, :]
        if n_maps == 2:
            qz_s[h, :, 0:tile] = jnp.where(slab_row < DQ, qh, jnp.zeros_like(qh))
            qz_s[h, :, tile:2 * tile] = jnp.where((slab_row >= DQ) & (slab_row < 2 * DQ), qh,
                                                  jnp.zeros_like(qh))
        else:
            qz_s[h] = qh
    m_s[...] = jnp.full(m_s.shape, NEG_INF, F32)
    l_s[...] = jnp.zeros(l_s.shape, F32)
    acc_s[...] = jnp.zeros(acc_s.shape, F32)

    def step(j, masked):
        def scores(h):
            kj = k_ref[pl.ds(pl.multiple_of(j * tile, tile), tile), h * SLAB:(h + 1) * SLAB]
            return jnp.dot(kj, qz_s[h], preferred_element_type=F32)

        s_next = scores(0)
        for h in range(HG):
            s_all = s_next
            if h + 1 < HG:
                s_next = scores(h + 1)
            vj = v_ref[j, h * HEAD_DIM:(h + 1) * HEAD_DIM, :]
            for mp in range(n_maps):
                c = h * n_maps + mp
                s = s_all[:, mp * tile:(mp + 1) * tile]
                if masked:
                    s = jnp.where(diag_ok, s, NEG_INF)
                m_old = m_s[c:c + 1, :]
                m_new = jnp.maximum(m_old, jnp.max(s, axis=0, keepdims=True))
                alpha = jnp.exp(m_old - m_new)
                p = jnp.exp(s - m_new)
                m_s[c:c + 1, :] = m_new
                l_s[c:c + 1, :] = alpha * l_s[c:c + 1, :] + jnp.sum(p, axis=0, keepdims=True)
                rows = slice(c * HEAD_DIM, (c + 1) * HEAD_DIM)
                acc_s[rows, :] = alpha * acc_s[rows, :] + jnp.dot(vj, p.astype(BF16),
                                                                  preferred_element_type=F32)

    def body(j, carry):
        step(j, False)
        return carry

    lax.fori_loop(0, qi, body, 0)
    step(qi, True)

    if n_maps == 2:
        aux = aux_ref[...]
        lam = (jnp.exp(jnp.sum(aux[0:1] * aux[1:2], axis=1, keepdims=True))
               - jnp.exp(jnp.sum(aux[2:3] * aux[3:4], axis=1, keepdims=True)) + lambda_init)
    outs = []
    for h in range(HG):
        c = h * n_maps
        o = acc_s[c * HEAD_DIM:(c + 1) * HEAD_DIM, :] / l_s[c:c + 1, :]
        if n_maps == 2:
            o = o - lam * (acc_s[(c + 1) * HEAD_DIM:(c + 2) * HEAD_DIM, :] / l_s[c + 1:c + 2, :])
            o = o * lax.rsqrt(jnp.mean(o * o, axis=0, keepdims=True) + EPS)
        outs.append(o)
    ot = jnp.concatenate(outs, axis=0)
    if n_maps == 2:
        ot = ot * gain_ref[...]
    o_ref[...] = ot.T.astype(BF16)


def _attention(q3, k2, v3, aux, gain, *, batch, seq, tile, n_maps, lambda_init):
    nq = seq // tile
    T = batch * seq
    n_chains = HG * n_maps
    scratch = [pltpu.VMEM((HG, SLAB, n_maps * tile), BF16),
               pltpu.VMEM((n_chains, tile), F32),
               pltpu.VMEM((n_chains, tile), F32),
               pltpu.VMEM((n_chains * HEAD_DIM, tile), F32)]
    return pl.pallas_call(
        functools.partial(_attn_kernel, n_maps=n_maps, tile=tile, lambda_init=lambda_init),
        grid=(batch, nq),
        in_specs=[
            pl.BlockSpec((None, HG * SLAB, tile), lambda b, i: (b * nq + i, 0, 0)),
            pl.BlockSpec((seq, HG * SLAB), lambda b, i: (b, 0)),
            pl.BlockSpec((nq, GW, tile), lambda b, i: (b, 0, 0)),
            pl.BlockSpec(aux.shape, lambda b, i: (0, 0)),
            pl.BlockSpec(gain.shape, lambda b, i: (0, 0)),
        ],
        out_specs=pl.BlockSpec((tile, GW), lambda b, i: (b * nq + i, 0)),
        out_shape=jax.ShapeDtypeStruct((T, GW), BF16),
        scratch_shapes=scratch,
        compiler_params=_params(("arbitrary", "arbitrary")),
        name="diff_attention" if n_maps == 2 else "forgetting_attention",
    )(q3, k2, v3, aux, gain)


def _erf_gelu(x):
    return 0.5 * x * (1.0 + lax.erf(x * (1.0 / math.sqrt(2.0))))


def _gmlp_kernel(h_ref, wc_ref, sgw_ref, bias_ref, ln_ref, o_ref):
    tm = h_ref.shape[0]
    pc = _erf_gelu(jnp.dot(h_ref[...], wc_ref[...], preferred_element_type=F32))
    u = pc[:, 0:GW]
    v = pc[:, GW:2 * GW]
    mu = jnp.mean(v, axis=-1, keepdims=True)
    vc = v - mu
    var = jnp.mean(vc * vc, axis=-1, keepdims=True)
    ln = ln_ref[...]
    vn = (vc * lax.rsqrt(var + EPS) * ln[0:1] + ln[1:2]).astype(BF16)

    causal = _iota((SG_CHUNK, SG_CHUNK), 0) >= _iota((SG_CHUNK, SG_CHUNK), 1)
    w = jnp.concatenate([jnp.where(causal, sgw_ref[g], 0.0) for g in range(HG)], axis=0).astype(BF16)
    lane_head = _iota((1, GW), 1) // HEAD_DIM
    bias = bias_ref[...]
    for n in range(tm // SG_CHUNK):
        rows = slice(n * SG_CHUNK, (n + 1) * SG_CHUNK)
        full = jnp.dot(w, vn[rows], preferred_element_type=F32)
        sv = bias
        for g in range(HG):
            sv = sv + jnp.where(lane_head == g, full[g * SG_CHUNK:(g + 1) * SG_CHUNK], 0.0)
        o_ref[rows, :] = (u[rows] * sv).astype(BF16)


def _gmlp_mixer(h, wc, sg_w, bias, ln, *, tm):
    T, D = h.shape
    return pl.pallas_call(
        _gmlp_kernel,
        grid=(T // tm,),
        in_specs=[
            pl.BlockSpec((tm, D), lambda i: (i, 0)),
            pl.BlockSpec((D, 2 * GW), lambda i: (0, 0)),
            pl.BlockSpec((HG, SG_CHUNK, SG_CHUNK), lambda i: (0, 0, 0)),
            pl.BlockSpec((SG_CHUNK, GW), lambda i: (0, 0)),
            pl.BlockSpec((2, GW), lambda i: (0, 0)),
        ],
        out_specs=pl.BlockSpec((tm, GW), lambda i: (i, 0)),
        out_shape=jax.ShapeDtypeStruct((T, GW), BF16),
        compiler_params=_params(("arbitrary",)),
        name="gmlp_mixer",
    )(h, wc, sg_w, bias, ln)


def _out_proj_kernel(ya_ref, yb_ref, yc_ref, yd_ref, w_ref, x_ref, mod_ref, g_ref, xo_ref, h_ref):
    acc = jnp.dot(ya_ref[...], w_ref[0], preferred_element_type=F32)
    acc = acc + jnp.dot(yb_ref[...], w_ref[1], preferred_element_type=F32)
    acc = acc + jnp.dot(yc_ref[...], w_ref[2], preferred_element_type=F32)
    acc = acc + jnp.dot(yd_ref[...], w_ref[3], preferred_element_type=F32)
    mod = mod_ref[...]
    x = x_ref[...] + mod[2:3] * acc
    xo_ref[...] = x
    h_ref[...] = _modulated_norm(x, g_ref[...], mod[3:4], mod[4:5]).astype(BF16)


def _out_proj(ys, w_out4, x2, mod_l, g2, *, seq, tm):
    T, D = x2.shape
    per_seq = seq // tm
    ymap = pl.BlockSpec((tm, GW), lambda i: (i, 0))
    return pl.pallas_call(
        _out_proj_kernel,
        grid=(T // tm,),
        in_specs=[ymap, ymap, ymap, ymap,
                  pl.BlockSpec((N_MIXERS, GW, D), lambda i: (0, 0, 0)),
                  pl.BlockSpec((tm, D), lambda i: (i, 0)),
                  pl.BlockSpec((None, 6, D), lambda i: (i // per_seq, 0, 0)),
                  pl.BlockSpec((1, D), lambda i: (0, 0))],
        out_specs=[pl.BlockSpec((tm, D), lambda i: (i, 0)), pl.BlockSpec((tm, D), lambda i: (i, 0))],
        out_shape=[jax.ShapeDtypeStruct((T, D), F32), jax.ShapeDtypeStruct((T, D), BF16)],
        compiler_params=_params(("arbitrary",)),
        name="out_proj",
    )(*ys, w_out4, x2, mod_l, g2.reshape(1, D))


HALO = 16


def _ffn_kernel(hh_ref, h_ref, x_ref, wup_ref, cw_ref, cb_ref, wdn_ref, mod_ref, o_ref, acc_ref,
                *, per_seq, n_fc):
    i = pl.program_id(0)
    j = pl.program_id(1)
    tm = h_ref.shape[0]
    fc = wdn_ref.shape[0]

    hx = jnp.concatenate([hh_ref[...], h_ref[...]], axis=0)
    seq_start = (i % per_seq) == 0
    halo_row = _iota((HALO + tm, 1), 0) < HALO

    def conv_half(half):
        u = jnp.dot(hx, wup_ref[half], preferred_element_type=F32)
        u = jnp.where(halo_row & seq_start, 0.0, u)
        cw = cw_ref[half]
        y = (cw[0:1] * pltpu.roll(u, 2, axis=0) + cw[1:2] * pltpu.roll(u, 1, axis=0) + cw[2:3] * u)
        return y[HALO:] + cb_ref[half]

    a = conv_half(0)
    g = conv_half(1)
    act = (a * (g * _sigmoid(g))).astype(BF16)
    part = jnp.dot(act, wdn_ref[...], preferred_element_type=F32)

    @pl.when(j == 0)
    def _():
        acc_ref[...] = part

    @pl.when(j > 0)
    def _():
        acc_ref[...] += part

    @pl.when(j == n_fc - 1)
    def _():
        o_ref[...] = x_ref[...] + mod_ref[...][5:6] * acc_ref[...]


def _ffn(h2, x2, wup, cw, cb, wdn, mod_l, *, seq, tm, fc):
    T, D = x2.shape
    dff = wdn.shape[0]
    n_fc = dff // fc
    per_seq = seq // tm
    halo_blocks = tm // HALO
    return pl.pallas_call(
        functools.partial(_ffn_kernel, per_seq=per_seq, n_fc=n_fc),
        grid=(T // tm, n_fc),
        in_specs=[
            pl.BlockSpec((HALO, D), lambda i, j: (jnp.maximum(i * halo_blocks - 1, 0), 0)),
            pl.BlockSpec((tm, D), lambda i, j: (i, 0)),
            pl.BlockSpec((tm, D), lambda i, j: (i, 0)),
            pl.BlockSpec((2, D, fc), lambda i, j: (0, 0, j)),
            pl.BlockSpec((2, 3, fc), lambda i, j: (0, 0, j)),
            pl.BlockSpec((2, 1, fc), lambda i, j: (0, 0, j)),
            pl.BlockSpec((fc, D), lambda i, j: (j, 0)),
            pl.BlockSpec((None, 6, D), lambda i, j: (i // per_seq, 0, 0)),
        ],
        out_specs=pl.BlockSpec((tm, D), lambda i, j: (i, 0)),
        out_shape=jax.ShapeDtypeStruct((T, D), F32),
        scratch_shapes=[pltpu.VMEM((tm, D), F32)],
        compiler_params=_params(("arbitrary", "arbitrary")),
        name="conv_glu_ffn",
    )(h2, h2, x2, wup, cw, cb, wdn, mod_l)


def _rope_tables(seq):
    inv = 1.0 / (ROPE_THETA ** (jnp.arange(0, DQ, 2, dtype=F32) / DQ))
    ang = jnp.arange(seq, dtype=F32)[:, None] * inv[None, :]
    cos = jnp.cos(ang)
    sin = jnp.sin(ang)
    cos_map = jnp.concatenate([cos, cos], axis=-1)
    sin_map = jnp.concatenate([-sin, sin], axis=-1)
    reps = GW // DQ
    return jnp.tile(cos_map, (1, reps)), jnp.tile(sin_map, (1, reps))


def kernel(x, c, ada_w, ada_b, norm1_g, norm2_g, w_in, w_out, rw_mu, rw_w0, rw_w_up, rw_a0, rw_a_up, rw_g_up, rw_k_k, rw_k_a, rw_r_k, rw_ln_g, rw_ln_b, df_lam_q1, df_lam_k1, df_lam_q2, df_lam_k2, df_q_g, df_k_g, df_sub_g, sg_w, sg_b, sg_ln_g, sg_ln_b, fx_q_g, fx_k_g, fx_f_b, ffn_up, ffn_conv, ffn_conv_b, ffn_down):
    Bn, S, D = x.shape
    L = ada_w.shape[0]
    T = Bn * S
    dff = ffn_down.shape[1]
    tm = min(ROW_TILE, S)
    tile = min(ATT_TILE, S)
    fc = dff // 2 if (dff // 2) % LANES == 0 else dff

    mod = _ada_mod(c, ada_w, ada_b).reshape(L, Bn, 6, D)
    cos, sin = _rope_tables(S)
    cost, sint = cos.T, sin.T
    x2 = x.reshape(T, D)

    oa = 0
    ob = oa + RW_COLS
    oc = ob + 3 * GW
    od = oc + 2 * GW

    for l in range(L):
        lambda_init = 0.8 - 0.6 * math.exp(-0.3 * l)
        wl = w_in[l].astype(BF16)
        h = _norm_mod(x2, mod[l], norm1_g[l], seq=S, row0=0, tm=tm)

        lora_w = jnp.zeros((LANES, 3 * GW), F32)
        lora_w = lora_w.at[0:RW_DECAY_RANK, 0:GW].set(rw_w_up[l])
        lora_w = lora_w.at[RW_DECAY_RANK:RW_DECAY_RANK + RW_A_RANK, GW:2 * GW].set(rw_a_up[l])
        lora_w = lora_w.at[RW_DECAY_RANK + RW_A_RANK:LANES, 2 * GW:3 * GW].set(rw_g_up[l])
        vec = jnp.stack([rw_w0[l], rw_a0[l], rw_k_k[l], rw_k_a[l], rw_r_k[l].reshape(GW),
                         rw_ln_g[l], rw_ln_b[l], jnp.zeros((GW,), F32)])
        ya = _rwkv_mixer(h, wl[:, oa:ob], rw_mu[l].reshape(1, RW_COLS), lora_w, vec,
                         batch=Bn, seq=S, tt=tm, mode=RW_MODE)

        gk = jnp.tile(df_k_g[l], GW // DQ).reshape(1, GW)
        gqt = jnp.broadcast_to((jnp.tile(df_q_g[l], GW // DQ) * DQ ** -0.5)[:, None], (GW, tm))
        qb, kb, vb = _diff_prep(h, wl[:, ob + GW:ob + 2 * GW], wl[:, ob:ob + GW].T, wl[:, ob + 2 * GW:oc].T,
                                gk, gqt, cos, sin, cost, sint, seq=S, tm=tm, tile=tile)
        lam_vecs = jnp.stack([df_lam_q1[l], df_lam_k1[l], df_lam_q2[l], df_lam_k2[l]])
        sub_gain = jnp.broadcast_to((jnp.tile(df_sub_g[l], HG) * (1.0 - lambda_init))[:, None], (GW, tile))
        yb = _attention(qb, kb, vb, lam_vecs, sub_gain, batch=Bn, seq=S, tile=tile,
                        n_maps=2, lambda_init=lambda_init)

        sg_bias = jnp.repeat(sg_b[l].T, HEAD_DIM, axis=1)
        yc = _gmlp_mixer(h, wl[:, oc:od], sg_w[l], sg_bias,
                         jnp.stack([sg_ln_g[l], sg_ln_b[l]]), tm=tm)

        wf = jnp.zeros((D, LANES), BF16).at[:, 0:HG].set(wl[:, od + 3 * GW:od + 3 * GW + HG])
        fb = jnp.zeros((1, LANES), F32).at[0, 0:HG].set(fx_f_b[l])
        gkd = jnp.tile(fx_k_g[l], HG).reshape(1, GW)
        gqd = jnp.broadcast_to((jnp.tile(fx_q_g[l], HG) * HEAD_DIM ** -0.5)[:, None], (GW, tm))
        qd, kd, vd = _fox_prep(h, wl[:, od + GW:od + 2 * GW], wf, wl[:, od:od + GW].T,
                               wl[:, od + 2 * GW:od + 3 * GW].T, gkd, gqd, fb,
                               batch=Bn, seq=S, tm=tm, tile=tile)
        yd = _attention(qd, kd, vd, jnp.zeros((8, LANES), F32), jnp.zeros((8, LANES), F32),
                        batch=Bn, seq=S, tile=tile, n_maps=1, lambda_init=0.0)

        x2, h2 = _out_proj((ya, yb, yc, yd), w_out[l].astype(BF16).reshape(N_MIXERS, GW, D),
                           x2, mod[l], norm2_g[l], seq=S, tm=tm)

        wup = ffn_up[l].astype(BF16).reshape(D, 2, dff).transpose(1, 0, 2)
        cw = ffn_conv[l].reshape(3, 2, dff).transpose(1, 0, 2)
        cb = ffn_conv_b[l].reshape(2, 1, dff)
        x2 = _ffn(h2, x2, wup, cw, cb, ffn_down[l].astype(BF16), mod[l], seq=S, tm=tm, fc=fc)

    return x2.reshape(Bn, S, D)
```

```python
import functools
import math

import jax
import jax.numpy as jnp
from jax import lax
from jax.experimental import pallas as pl
from jax.experimental.pallas import tpu as pltpu

F32 = jnp.float32
BF16 = jnp.bfloat16
HIGHEST = lax.Precision.HIGHEST

N_MIXERS = 4
HEAD_DIM = 64
HG = 4
GW = HG * HEAD_DIM
DQ = HEAD_DIM // 2
RW_DECAY_RANK = 32
RW_A_RANK = 32
RW_GATE_RANK = 64
RW_COLS = 3 * GW + RW_DECAY_RANK + RW_A_RANK + RW_GATE_RANK
RW_GN_EPS = 64e-5
SG_CHUNK = 128
ATT_CHUNK = 64
ROPE_THETA = 10000.0
EPS = 1e-6
LOG2E = math.log2(math.e)
NEG_INF = -1e30

LANES = 128
V7X_VMEM_BYTES = 64 * 1024 * 1024
VMEM_LIMIT = 56 * 1024 * 1024

ROW_TILE = 512
RW_CHUNK = 64
ATT_TILE = 256
SLAB = 128
RW_MODE = "lo"

NN = (((1,), (0,)), ((), ()))
NT = (((1,), (1,)), ((), ()))
TN = (((0,), (0,)), ((), ()))


def _params(sem):
    return pltpu.CompilerParams(dimension_semantics=sem, vmem_limit_bytes=VMEM_LIMIT)


def _mm(a, b, dims=NN, mode="hi"):
    if mode == "hi":
        return lax.dot_general(a.astype(F32), b.astype(F32), dims, precision=HIGHEST,
                               preferred_element_type=F32)
    if mode == "lo":
        return lax.dot_general(a.astype(BF16), b.astype(BF16), dims, preferred_element_type=F32)
    ah = a.astype(BF16)
    al = (a - ah.astype(F32)).astype(BF16)
    bh = b.astype(BF16)
    bl = (b - bh.astype(F32)).astype(BF16)
    dg = functools.partial(lax.dot_general, dimension_numbers=dims, preferred_element_type=F32)
    return dg(ah, bh) + (dg(ah, bl) + dg(al, bh))


def _mm_exact(a, b, dims=NN, left=True):
    x = a if left else b
    xh = x.astype(BF16)
    xl = (x - xh.astype(F32)).astype(BF16)
    dg = functools.partial(lax.dot_general, dimension_numbers=dims, preferred_element_type=F32)
    if left:
        bb = b.astype(BF16)
        return dg(xh, bb) + dg(xl, bb)
    ab = a.astype(BF16)
    return dg(ab, xh) + dg(ab, xl)


def _iota(shape, dim):
    return lax.broadcasted_iota(jnp.int32, shape, dim)


def _group_matrix(n, group, value=1.0):
    same = (_iota((n, n), 0) // group) == (_iota((n, n), 1) // group)
    return jnp.where(same, value, 0.0).astype(F32)


def _softplus(x):
    return jnp.maximum(x, 0.0) + jnp.log1p(jnp.exp(-jnp.abs(x)))


def _sigmoid(x):
    return 1.0 / (1.0 + jnp.exp(-x))


def _ada_kernel(c_ref, w_ref, b_ref, o_ref):
    c = c_ref[...]
    cond = c * _sigmoid(c)
    o_ref[...] = _mm(cond, w_ref[...], NN, "hi") + b_ref[...]


def _ada_mod(c, ada_w, ada_b):
    L, D, D6 = ada_w.shape
    Bn = c.shape[0]
    tn = D6 // 4
    return pl.pallas_call(
        _ada_kernel,
        grid=(L, D6 // tn),
        in_specs=[
            pl.BlockSpec((Bn, D), lambda l, j: (0, 0)),
            pl.BlockSpec((None, D, tn), lambda l, j: (l, 0, j)),
            pl.BlockSpec((None, 1, tn), lambda l, j: (l, 0, j)),
        ],
        out_specs=pl.BlockSpec((None, Bn, tn), lambda l, j: (l, 0, j)),
        out_shape=jax.ShapeDtypeStruct((L, Bn, D6), F32),
        compiler_params=_params(("arbitrary", "arbitrary")),
        name="ada_mod",
    )(c, ada_w, ada_b.reshape(L, 1, D6))


def _modulated_norm(x, g, shift, scale):
    y = x * lax.rsqrt(jnp.mean(x * x, axis=-1, keepdims=True) + EPS) * g
    return y * (1.0 + scale) + shift


def _norm_kernel(x_ref, mod_ref, g_ref, o_ref, *, row0):
    mod = mod_ref[...]
    h = _modulated_norm(x_ref[...], g_ref[...], mod[row0:row0 + 1], mod[row0 + 1:row0 + 2])
    o_ref[...] = h.astype(BF16)


def _norm_mod(x2, mod_l, g, *, seq, row0, tm):
    T, D = x2.shape
    per_seq = seq // tm
    return pl.pallas_call(
        functools.partial(_norm_kernel, row0=row0),
        grid=(T // tm,),
        in_specs=[
            pl.BlockSpec((tm, D), lambda i: (i, 0)),
            pl.BlockSpec((None, 6, D), lambda i: (i // per_seq, 0, 0)),
            pl.BlockSpec((1, D), lambda i: (0, 0)),
        ],
        out_specs=pl.BlockSpec((tm, D), lambda i: (i, 0)),
        out_shape=jax.ShapeDtypeStruct((T, D), BF16),
        compiler_params=_params(("arbitrary",)),
        name="norm_mod",
    )(x2, mod_l, g.reshape(1, D))


def _blockdiag(mp, head_lane_masks):
    return jnp.concatenate([jnp.where(m, mp, 0.0) for m in head_lane_masks], axis=0)


def _apply_packed(mp, rhs, head_lane_masks, C, mode):
    full = _mm(mp, rhs, TN, mode)
    out = jnp.where(head_lane_masks[0], full[0:C], 0.0)
    for h in range(1, HG):
        out = out + jnp.where(head_lane_masks[h], full[h * C:(h + 1) * C], 0.0)
    return out


def _rwkv_kernel(h_ref, wa_ref, mu_ref, lora_ref, vec_ref, o_ref,
                 carry_ref, st_ref, hs_ref, *, C, mode):
    TT = h_ref.shape[0]

    @pl.when(pl.program_id(1) == 0)
    def _():
        carry_ref[...] = jnp.zeros_like(carry_ref)
        st_ref[...] = jnp.zeros_like(st_ref)

    vec = vec_ref[...]
    w0, a0, k_k, k_a, r_k, ln_g, ln_b = (vec[i:i + 1] for i in range(7))

    pa = jnp.dot(h_ref[...], wa_ref[...], preferred_element_type=F32)
    prev = pltpu.roll(pa, 1, axis=0)
    prev = jnp.where(_iota((TT, 1), 0) == 0, carry_ref[...], prev)
    carry_ref[...] = pa[TT - 1:TT]
    pa = pa + (prev - pa) * mu_ref[...]

    r = pa[:, 0:GW]
    k = pa[:, GW:2 * GW]
    v = pa[:, 2 * GW:3 * GW]
    lo = pa[:, 3 * GW:3 * GW + LANES]
    lane = _iota((1, LANES), 1)
    act = jnp.where(lane < RW_DECAY_RANK, jnp.tanh(lo),
                    jnp.where(lane < RW_DECAY_RANK + RW_A_RANK, lo, _sigmoid(lo)))
    lora = _mm(act, lora_ref[...], NN, "x3")
    w_log = -_softplus(-(w0 + lora[:, 0:GW])) - 0.5
    lw = -jnp.exp(w_log)
    a = _sigmoid(a0 + lora[:, GW:2 * GW])
    gate = lora[:, 2 * GW:3 * GW]

    gsum = _group_matrix(GW, HEAD_DIM)
    kk = k * k_k
    kk = kk / jnp.maximum(jnp.sqrt(_mm(kk * kk, gsum, NN, "lo")), 1e-12)
    k2 = k * (1.0 + (a - 1.0) * k_a)
    bonus = _mm(r * k2 * r_k, gsum, NN, "lo") * v

    aa = -kk
    b = kk * a

    lane_head = _iota((1, GW), 1) // HEAD_DIM
    hmask = [lane_head == h for h in range(HG)]
    s_idx = _iota((C, GW), 0)
    t_idx = _iota((C, GW), 1) % C
    strict = s_idx < t_idx
    incl = s_idx <= t_idx
    eye_p = jnp.where(s_idx == t_idx, 1.0, 0.0).astype(F32)
    l_incl = jnp.where(_iota((C, C), 0) >= _iota((C, C), 1), 1.0, 0.0).astype(F32)
    bd = (_iota((GW, GW), 0) // HEAD_DIM) == (_iota((GW, GW), 1) // HEAD_DIM)
    eye_k = jnp.where(_iota((GW, GW), 0) == _iota((GW, GW), 1), 1.0, 0.0).astype(F32)
    chunks = range(TT // C)
    sl = lambda x, c: x[c * C:(c + 1) * C]

    cl = jnp.concatenate([_mm_exact(l_incl, sl(lw, c), left=False) for c in chunks], axis=0)
    g_in = jnp.exp(cl)
    g_inv = jnp.exp(-cl)
    at = aa * jnp.exp(cl - lw)
    rt = r * g_in
    bt = b * g_inv
    kt = k2 * g_inv

    bks = [jnp.concatenate([sl(bt, c), sl(kt, c)], axis=0) for c in chunks]
    xs = [_mm(bks[c], jnp.concatenate([_blockdiag(sl(at, c), hmask), _blockdiag(sl(rt, c), hmask)], axis=0),
              NT, mode) for c in chunks]
    n_p = [jnp.where(strict, x[0:C, 0:GW], 0.0) for x in xs]
    ak_p = [jnp.where(strict, x[C:2 * C, 0:GW], 0.0) for x in xs]
    rbk_p = [jnp.concatenate([jnp.where(incl, x[0:C, GW:2 * GW], 0.0),
                              jnp.where(incl, x[C:2 * C, GW:2 * GW], 0.0)], axis=0) for x in xs]

    p_p = [eye_p + n for n in n_p]
    q_p = n_p
    for _ in range(int(math.log2(C)) - 1):
        q_p = [_mm(q, _blockdiag(q, hmask), NN, mode) for q in q_p]
        p_p = [p + _mm(p, _blockdiag(q, hmask), NN, mode) for p, q in zip(p_p, q_p)]

    ta = [_apply_packed(p_p[c], sl(at, c), hmask, C, mode) for c in chunks]
    wv = [_apply_packed(ak_p[c], sl(v, c), hmask, C, mode) for c in chunks]
    tw = [_apply_packed(p_p[c], wv[c], hmask, C, mode) for c in chunks]
    g_end = [g_in[(c + 1) * C - 1:(c + 1) * C] for c in chunks]
    m_t = [(eye_k + jnp.where(bd, _mm(ta[c], sl(bt, c), TN, mode), 0.0)) * g_end[c] for c in chunks]
    g_t = [jnp.where(bd, _mm(jnp.concatenate([tw[c], sl(v, c)], axis=0), bks[c], TN, mode), 0.0) * g_end[c]
           for c in chunks]

    ht = st_ref[...]
    for c in chunks:
        hs_ref[c] = ht
        ht = _mm(ht, m_t[c], NN, mode) + g_t[c]
    st_ref[...] = ht

    ys = []
    for c in chunks:
        sh = _mm(jnp.concatenate([ta[c], sl(rt, c)], axis=0), hs_ref[c], NT, mode)
        uv = jnp.concatenate([sh[0:C] + tw[c], sl(v, c)], axis=0)
        ys.append(sh[C:2 * C] + _apply_packed(rbk_p[c], uv, hmask, C, mode))
    y = jnp.concatenate(ys, axis=0)

    gmean = _group_matrix(GW, HEAD_DIM, 1.0 / HEAD_DIM)
    yc = y - _mm(y, gmean, NN, "lo")
    yn = yc * lax.rsqrt(_mm(yc * yc, gmean, NN, "lo") + RW_GN_EPS) * ln_g + ln_b
    o_ref[...] = ((yn + bonus) * gate).astype(BF16)


def _rwkv_mixer(h, wa, mu, lora_w, vec, *, batch, seq, tt, mode):
    T, D = h.shape
    per_seq = seq // tt
    C = RW_CHUNK
    return pl.pallas_call(
        functools.partial(_rwkv_kernel, C=C, mode=mode),
        grid=(batch, per_seq),
        in_specs=[
            pl.BlockSpec((tt, D), lambda b, i: (b * per_seq + i, 0)),
            pl.BlockSpec((D, RW_COLS), lambda b, i: (0, 0)),
            pl.BlockSpec((1, RW_COLS), lambda b, i: (0, 0)),
            pl.BlockSpec((LANES, 3 * GW), lambda b, i: (0, 0)),
            pl.BlockSpec((8, GW), lambda b, i: (0, 0)),
        ],
        out_specs=pl.BlockSpec((tt, GW), lambda b, i: (b * per_seq + i, 0)),
        out_shape=jax.ShapeDtypeStruct((T, GW), BF16),
        scratch_shapes=[pltpu.VMEM((1, RW_COLS), F32), pltpu.VMEM((GW, GW), F32),
                        pltpu.VMEM((tt // C, GW, GW), F32)],
        compiler_params=_params(("arbitrary", "arbitrary")),
        name="rwkv7_mixer",
    )(h, wa, mu, lora_w, vec)


def _place_heads_lanes(n_in, width):
    src = _iota((n_in, HG * SLAB), 0)
    dst = _iota((n_in, HG * SLAB), 1)
    ok = ((dst // SLAB) == (src // width)) & ((dst % SLAB) == (src % width)) & ((dst % SLAB) < width)
    return jnp.where(ok, 1.0, 0.0).astype(BF16)


def _slab_rows(xt, extra):
    parts = []
    for h in range(HG):
        parts.append(xt[h * HEAD_DIM:(h + 1) * HEAD_DIM])
        parts.append(extra)
    return jnp.concatenate(parts, axis=0)


def _diff_prep_kernel(h_ref, wk_ref, wqt_ref, wvt_ref, gk_ref, gqt_ref,
                      cos_ref, sin_ref, cost_ref, sint_ref, q_out, k_out, v_out, *, tile):
    tm = h_ref.shape[0]
    hb = h_ref[...]
    k = jnp.dot(hb, wk_ref[...], preferred_element_type=F32)
    qt = lax.dot_general(wqt_ref[...], hb, NT, preferred_element_type=F32)
    vt = lax.dot_general(wvt_ref[...], hb, NT, preferred_element_type=F32)
    gmean = _group_matrix(GW, DQ, 1.0 / DQ)

    kn = k * lax.rsqrt(_mm_exact(k * k, gmean) + EPS) * gk_ref[...]
    first_half = (_iota((1, GW), 1) % DQ) < (DQ // 2)
    partner = jnp.where(first_half, pltpu.roll(kn, GW - DQ // 2, axis=1), pltpu.roll(kn, DQ // 2, axis=1))
    kr = kn * cos_ref[...] + partner * sin_ref[...]
    k_out[...] = jnp.dot(kr.astype(BF16), _place_heads_lanes(GW, HEAD_DIM),
                         preferred_element_type=F32).astype(BF16)

    qn = qt * lax.rsqrt(_mm_exact(gmean, qt * qt, left=False) + EPS) * gqt_ref[...]
    first_half_t = (_iota((GW, 1), 0) % DQ) < (DQ // 2)
    partner_t = jnp.where(first_half_t, pltpu.roll(qn, GW - DQ // 2, axis=0), pltpu.roll(qn, DQ // 2, axis=0))
    qr = (qn * cost_ref[...] + partner_t * sint_ref[...]).astype(BF16)
    qs = _slab_rows(qr, jnp.zeros((SLAB - HEAD_DIM, tm), BF16))
    vb = vt.astype(BF16)
    for c in range(tm // tile):
        q_out[c] = qs[:, c * tile:(c + 1) * tile]
        v_out[c] = vb[:, c * tile:(c + 1) * tile]


def _diff_prep(h, wk, wqt, wvt, gk, gqt, cos, sin, cost, sint, *, seq, tm, tile):
    T, D = h.shape
    per_seq = seq // tm
    nt = tm // tile
    const = lambda i: (0, 0)
    return pl.pallas_call(
        functools.partial(_diff_prep_kernel, tile=tile),
        grid=(T // tm,),
        in_specs=[
            pl.BlockSpec((tm, D), lambda i: (i, 0)),
            pl.BlockSpec((D, GW), const),
            pl.BlockSpec((GW, D), const),
            pl.BlockSpec((GW, D), const),
            pl.BlockSpec((1, GW), const),
            pl.BlockSpec((GW, tm), const),
            pl.BlockSpec((tm, GW), lambda i: (i % per_seq, 0)),
            pl.BlockSpec((tm, GW), lambda i: (i % per_seq, 0)),
            pl.BlockSpec((GW, tm), lambda i: (0, i % per_seq)),
            pl.BlockSpec((GW, tm), lambda i: (0, i % per_seq)),
        ],
        out_specs=[
            pl.BlockSpec((nt, HG * SLAB, tile), lambda i: (i, 0, 0)),
            pl.BlockSpec((tm, HG * SLAB), lambda i: (i, 0)),
            pl.BlockSpec((nt, GW, tile), lambda i: (i, 0, 0)),
        ],
        out_shape=[
            jax.ShapeDtypeStruct((T // tile, HG * SLAB, tile), BF16),
            jax.ShapeDtypeStruct((T, HG * SLAB), BF16),
            jax.ShapeDtypeStruct((T // tile, GW, tile), BF16),
        ],
        compiler_params=_params(("arbitrary",)),
        name="diff_prep",
    )(h, wk, wqt, wvt, gk, gqt, cos, sin, cost, sint)


def _fox_prep_kernel(h_ref, wk_ref, wf_ref, wqt_ref, wvt_ref, gk_ref, gqt_ref, fb_ref,
                     q_out, k_out, v_out, fcarry_ref, *, tile):
    tm = h_ref.shape[0]

    @pl.when(pl.program_id(1) == 0)
    def _():
        fcarry_ref[...] = jnp.zeros_like(fcarry_ref)

    hb = h_ref[...]
    k = jnp.dot(hb, wk_ref[...], preferred_element_type=F32)
    fl = jnp.dot(hb, wf_ref[...], preferred_element_type=F32)
    qt = lax.dot_general(wqt_ref[...], hb, NT, preferred_element_type=F32)
    vt = lax.dot_general(wvt_ref[...], hb, NT, preferred_element_type=F32)
    gmean = _group_matrix(GW, HEAD_DIM, 1.0 / HEAD_DIM)

    kn = (k * lax.rsqrt(_mm_exact(k * k, gmean) + EPS) * gk_ref[...]).astype(BF16)

    log_f = -_softplus(-(fl + fb_ref[...]))
    l_incl = jnp.where(_iota((tm, tm), 0) >= _iota((tm, tm), 1), 1.0, 0.0).astype(F32)
    cum = _mm_exact(l_incl, log_f, left=False) + fcarry_ref[...]
    fcarry_ref[...] = cum[tm - 1:tm]
    nf = cum * (-LOG2E)
    pieces = []
    for _ in range(3):
        p = nf.astype(BF16)
        pieces.append(p)
        nf = nf - p.astype(F32)
    src = _iota((LANES, HG * SLAB), 0)
    dst = _iota((LANES, HG * SLAB), 1)
    ks = jnp.dot(kn, _place_heads_lanes(GW, HEAD_DIM), preferred_element_type=F32)
    for i, p in enumerate(pieces):
        place = jnp.where((src < HG) & (dst == src * SLAB + HEAD_DIM + i), 1.0, 0.0).astype(BF16)
        ks = ks + jnp.dot(p, place, preferred_element_type=F32)
    k_out[...] = ks.astype(BF16)

    qn = (qt * lax.rsqrt(_mm_exact(gmean, qt * qt, left=False) + EPS) * gqt_ref[...]).astype(BF16)
    ones_rows = jnp.where(_iota((SLAB - HEAD_DIM, tm), 0) < 3, 1.0, 0.0).astype(BF16)
    qs = _slab_rows(qn, ones_rows)
    vb = vt.astype(BF16)
    for c in range(tm // tile):
        q_out[c] = qs[:, c * tile:(c + 1) * tile]
        v_out[c] = vb[:, c * tile:(c + 1) * tile]


def _fox_prep(h, wk, wf, wqt, wvt, gk, gqt, fb, *, batch, seq, tm, tile):
    T, D = h.shape
    per_seq = seq // tm
    nt = tm // tile
    const = lambda b, i: (0, 0)
    row = lambda b, i: (b * per_seq + i, 0)
    return pl.pallas_call(
        functools.partial(_fox_prep_kernel, tile=tile),
        grid=(batch, per_seq),
        in_specs=[
            pl.BlockSpec((tm, D), row),
            pl.BlockSpec((D, GW), const),
            pl.BlockSpec((D, LANES), const),
            pl.BlockSpec((GW, D), const),
            pl.BlockSpec((GW, D), const),
            pl.BlockSpec((1, GW), const),
            pl.BlockSpec((GW, tm), const),
            pl.BlockSpec((1, LANES), const),
        ],
        out_specs=[
            pl.BlockSpec((nt, HG * SLAB, tile), lambda b, i: (b * per_seq + i, 0, 0)),
            pl.BlockSpec((tm, HG * SLAB), row),
            pl.BlockSpec((nt, GW, tile), lambda b, i: (b * per_seq + i, 0, 0)),
        ],
        out_shape=[
            jax.ShapeDtypeStruct((T // tile, HG * SLAB, tile), BF16),
            jax.ShapeDtypeStruct((T, HG * SLAB), BF16),
            jax.ShapeDtypeStruct((T // tile, GW, tile), BF16),
        ],
        scratch_shapes=[pltpu.VMEM((1, LANES), F32)],
        compiler_params=_params(("arbitrary", "arbitrary")),
        name="fox_prep",
    )(h, wk, wf, wqt, wvt, gk, gqt, fb)


def _attn_kernel(q_ref, k_ref, v_ref, aux_ref, gain_ref, o_ref, qz_s, m_s, l_s, acc_s,
                 s_a, s_b, p_a, p_b, al_a, al_b, *, n_maps, tile, lambda_init):
    qi = pl.program_id(1)
    row = _iota((tile, tile), 0)
    col = _iota((tile, tile), 1)
    if n_maps == 2:
        diag_ok = (row // ATT_CHUNK) <= (col // ATT_CHUNK)
    else:
        diag_ok = row <= col

    slab_row = _iota((SLAB, 1), 0)
    for h in range(HG):
        qh = q_ref[h * SLAB:(h + 1) * SLAB, :]
        if n_maps == 2:
            qz_s[h, :, 0:tile] = jnp.where(slab_row < DQ, qh, jnp.zeros_like(qh))
            qz_s[h, :, tile:2 * tile] = jnp.where((slab_row >= DQ) & (slab_row < 2 * DQ), qh,
                                                  jnp.zeros_like(qh))
        else:
            qz_s[h] = qh
    m_s[...] = jnp.full(m_s.shape, NEG_INF, F32)
    l_s[...] = jnp.zeros(l_s.shape, F32)
    acc_s[...] = jnp.zeros(acc_s.shape, F32)
    p_b[...] = jnp.zeros(p_b.shape, BF16)
    al_b[...] = jnp.zeros(al_b.shape, F32)

    buf_a = (s_a, p_a, al_a)
    buf_b = (s_b, p_b, al_b)

    def scores(j, s_buf):
        for h in range(HG):
            kj = k_ref[pl.ds(pl.multiple_of(j * tile, tile), tile), h * SLAB:(h + 1) * SLAB]
            s_buf[h] = jnp.dot(kj, qz_s[h], preferred_element_type=F32)

    def value_update(j, p_buf, al_buf):
        for h in range(HG):
            vj = v_ref[j, h * HEAD_DIM:(h + 1) * HEAD_DIM, :]
            for mp in range(n_maps):
                c = h * n_maps + mp
                rows = slice(c * HEAD_DIM, (c + 1) * HEAD_DIM)
                pv = jnp.dot(vj, p_buf[c], preferred_element_type=F32)
                acc_s[rows, :] = al_buf[c:c + 1, :] * acc_s[rows, :] + pv

    def softmax(s_buf, p_buf, al_buf, masked):
        for h in range(HG):
            for mp in range(n_maps):
                c = h * n_maps + mp
                s = s_buf[h, :, mp * tile:(mp + 1) * tile]
                if masked:
                    s = jnp.where(diag_ok, s, NEG_INF)
                m_old = m_s[c:c + 1, :]
                m_new = jnp.maximum(m_old, jnp.max(s, axis=0, keepdims=True))
                alpha = jnp.exp2(m_old - m_new)
                p = jnp.exp2(s - m_new)
                m_s[c:c + 1, :] = m_new
                l_s[c:c + 1, :] = alpha * l_s[c:c + 1, :] + jnp.sum(p, axis=0, keepdims=True)
                al_buf[c:c + 1, :] = alpha
                p_buf[c] = p.astype(BF16)

    def step(j, cur, nxt):
        value_update(jnp.maximum(j - 1, 0), nxt[1], nxt[2])
        scores(j + 1, nxt[0])
        softmax(cur[0], cur[1], cur[2], False)

    def last(j, cur, nxt):
        value_update(jnp.maximum(j - 1, 0), nxt[1], nxt[2])
        softmax(cur[0], cur[1], cur[2], True)
        value_update(j, cur[1], cur[2])

    scores(0, s_a)

    def pair(jj, carry):
        step(2 * jj, buf_a, buf_b)
        step(2 * jj + 1, buf_b, buf_a)
        return carry

    lax.fori_loop(0, lax.shift_right_logical(qi, 1), pair, 0)
    odd = (qi & 1) == 1

    @pl.when(odd)
    def _():
        step(qi - 1, buf_a, buf_b)
        last(qi, buf_b, buf_a)

    @pl.when(jnp.logical_not(odd))
    def _():
        last(qi, buf_a, buf_b)

    if n_maps == 2:
        aux = aux_ref[...]
        lam = (jnp.exp(jnp.sum(aux[0:1] * aux[1:2], axis=1, keepdims=True))
               - jnp.exp(jnp.sum(aux[2:3] * aux[3:4], axis=1, keepdims=True)) + lambda_init)
    outs = []
    for h in range(HG):
        c = h * n_maps
        o = acc_s[c * HEAD_DIM:(c + 1) * HEAD_DIM, :] / l_s[c:c + 1, :]
        if n_maps == 2:
            o = o - lam * (acc_s[(c + 1) * HEAD_DIM:(c + 2) * HEAD_DIM, :] / l_s[c + 1:c + 2, :])
            o = o * lax.rsqrt(jnp.mean(o * o, axis=0, keepdims=True) + EPS)
        outs.append(o)
    ot = jnp.concatenate(outs, axis=0)
    if n_maps == 2:
        ot = ot * gain_ref[...]
    o_ref[...] = ot.T.astype(BF16)


def _attention(q3, k2, v3, aux, gain, *, batch, seq, tile, n_maps, lambda_init):
    nq = seq // tile
    T = batch * seq
    n_chains = HG * n_maps
    scratch = [pltpu.VMEM((HG, SLAB, n_maps * tile), BF16),
               pltpu.VMEM((n_chains, tile), F32),
               pltpu.VMEM((n_chains, tile), F32),
               pltpu.VMEM((n_chains * HEAD_DIM, tile), F32),
               pltpu.VMEM((HG, tile, n_maps * tile), F32),
               pltpu.VMEM((HG, tile, n_maps * tile), F32),
               pltpu.VMEM((n_chains, tile, tile), BF16),
               pltpu.VMEM((n_chains, tile, tile), BF16),
               pltpu.VMEM((n_chains, tile), F32),
               pltpu.VMEM((n_chains, tile), F32)]
    return pl.pallas_call(
        functools.partial(_attn_kernel, n_maps=n_maps, tile=tile, lambda_init=lambda_init),
        grid=(batch, nq),
        in_specs=[
            pl.BlockSpec((None, HG * SLAB, tile), lambda b, i: (b * nq + i, 0, 0)),
            pl.BlockSpec((seq, HG * SLAB), lambda b, i: (b, 0)),
            pl.BlockSpec((nq, GW, tile), lambda b, i: (b, 0, 0)),
            pl.BlockSpec(aux.shape, lambda b, i: (0, 0)),
            pl.BlockSpec(gain.shape, lambda b, i: (0, 0)),
        ],
        out_specs=pl.BlockSpec((tile, GW), lambda b, i: (b * nq + i, 0)),
        out_shape=jax.ShapeDtypeStruct((T, GW), BF16),
        scratch_shapes=scratch,
        compiler_params=_params(("arbitrary", "arbitrary")),
        name="diff_attention" if n_maps == 2 else "forgetting_attention",
    )(q3, k2, v3, aux, gain)


def _erf_gelu(x):
    return 0.5 * x * (1.0 + lax.erf(x * (1.0 / math.sqrt(2.0))))


def _gmlp_kernel(h_ref, wc_ref, sgw_ref, bias_ref, ln_ref, o_ref):
    tm = h_ref.shape[0]
    pc = _erf_gelu(jnp.dot(h_ref[...], wc_ref[...], preferred_element_type=F32))
    u = pc[:, 0:GW]
    v = pc[:, GW:2 * GW]
    mu = jnp.mean(v, axis=-1, keepdims=True)
    vc = v - mu
    var = jnp.mean(vc * vc, axis=-1, keepdims=True)
    ln = ln_ref[...]
    vn = (vc * lax.rsqrt(var + EPS) * ln[0:1] + ln[1:2]).astype(BF16)

    causal = _iota((SG_CHUNK, SG_CHUNK), 0) >= _iota((SG_CHUNK, SG_CHUNK), 1)
    w = jnp.concatenate([jnp.where(causal, sgw_ref[g], 0.0) for g in range(HG)], axis=0).astype(BF16)
    lane_head = _iota((1, GW), 1) // HEAD_DIM
    bias = bias_ref[...]
    for n in range(tm // SG_CHUNK):
        rows = slice(n * SG_CHUNK, (n + 1) * SG_CHUNK)
        full = jnp.dot(w, vn[rows], preferred_element_type=F32)
        sv = bias
        for g in range(HG):
            sv = sv + jnp.where(lane_head == g, full[g * SG_CHUNK:(g + 1) * SG_CHUNK], 0.0)
        o_ref[rows, :] = (u[rows] * sv).astype(BF16)


def _gmlp_mixer(h, wc, sg_w, bias, ln, *, tm):
    T, D = h.shape
    return pl.pallas_call(
        _gmlp_kernel,
        grid=(T // tm,),
        in_specs=[
            pl.BlockSpec((tm, D), lambda i: (i, 0)),
            pl.BlockSpec((D, 2 * GW), lambda i: (0, 0)),
            pl.BlockSpec((HG, SG_CHUNK, SG_CHUNK), lambda i: (0, 0, 0)),
            pl.BlockSpec((SG_CHUNK, GW), lambda i: (0, 0)),
            pl.BlockSpec((2, GW), lambda i: (0, 0)),
        ],
        out_specs=pl.BlockSpec((tm, GW), lambda i: (i, 0)),
        out_shape=jax.ShapeDtypeStruct((T, GW), BF16),
        compiler_params=_params(("arbitrary",)),
        name="gmlp_mixer",
    )(h, wc, sg_w, bias, ln)


def _out_proj_kernel(ya_ref, yb_ref, yc_ref, yd_ref, w_ref, x_ref, mod_ref, g_ref, xo_ref, h_ref):
    acc = jnp.dot(ya_ref[...], w_ref[0], preferred_element_type=F32)
    acc = acc + jnp.dot(yb_ref[...], w_ref[1], preferred_element_type=F32)
    acc = acc + jnp.dot(yc_ref[...], w_ref[2], preferred_element_type=F32)
    acc = acc + jnp.dot(yd_ref[...], w_ref[3], preferred_element_type=F32)
    mod = mod_ref[...]
    x = x_ref[...] + mod[2:3] * acc
    xo_ref[...] = x
    h_ref[...] = _modulated_norm(x, g_ref[...], mod[3:4], mod[4:5]).astype(BF16)


def _out_proj(ys, w_out4, x2, mod_l, g2, *, seq, tm):
    T, D = x2.shape
    per_seq = seq // tm
    ymap = pl.BlockSpec((tm, GW), lambda i: (i, 0))
    return pl.pallas_call(
        _out_proj_kernel,
        grid=(T // tm,),
        in_specs=[ymap, ymap, ymap, ymap,
                  pl.BlockSpec((N_MIXERS, GW, D), lambda i: (0, 0, 0)),
                  pl.BlockSpec((tm, D), lambda i: (i, 0)),
                  pl.BlockSpec((None, 6, D), lambda i: (i // per_seq, 0, 0)),
                  pl.BlockSpec((1, D), lambda i: (0, 0))],
        out_specs=[pl.BlockSpec((tm, D), lambda i: (i, 0)), pl.BlockSpec((tm, D), lambda i: (i, 0))],
        out_shape=[jax.ShapeDtypeStruct((T, D), F32), jax.ShapeDtypeStruct((T, D), BF16)],
        compiler_params=_params(("arbitrary",)),
        name="out_proj",
    )(*ys, w_out4, x2, mod_l, g2.reshape(1, D))


HALO = 16


FFN_COLS = 256


def _ffn_kernel(hh_ref, h_ref, x_ref, wup_ref, cw_ref, cb_ref, wdn_ref, mod_ref, modn_ref, gn_ref,
                o_ref, hn_ref, act_ref, *, per_seq):
    i = pl.program_id(0)
    tm = h_ref.shape[0]
    dff = wdn_ref.shape[0]

    hx = jnp.concatenate([hh_ref[...], h_ref[...]], axis=0)
    zero_halo = (_iota((HALO + tm, 1), 0) < HALO) & ((i % per_seq) == 0)

    def conv_half(half, cols):
        u = jnp.dot(hx, wup_ref[half, :, cols], preferred_element_type=F32)
        u = jnp.where(zero_halo, 0.0, u)
        cw = cw_ref[half, :, cols]
        y = (cw[0:1] * pltpu.roll(u, 2, axis=0) + cw[1:2] * pltpu.roll(u, 1, axis=0) + cw[2:3] * u)
        return y[HALO:] + cb_ref[half, :, cols]

    for j in range(dff // FFN_COLS):
        cols = slice(j * FFN_COLS, (j + 1) * FFN_COLS)
        a = conv_half(0, cols)
        g = conv_half(1, cols)
        act_ref[:, cols] = (a * (g * _sigmoid(g))).astype(BF16)

    y = jnp.dot(act_ref[...], wdn_ref[...], preferred_element_type=F32)
    x = x_ref[...] + mod_ref[...][5:6] * y
    o_ref[...] = x
    modn = modn_ref[...]
    hn_ref[...] = _modulated_norm(x, gn_ref[...], modn[0:1], modn[1:2]).astype(BF16)


def _ffn(h2, x2, wup, cw, cb, wdn, mod_l, mod_next, g_next, *, seq, tm):
    T, D = x2.shape
    dff = wdn.shape[0]
    per_seq = seq // tm
    halo_blocks = tm // HALO
    once = pl.Buffered(1)
    row = lambda i: (i, 0)
    return pl.pallas_call(
        functools.partial(_ffn_kernel, per_seq=per_seq),
        grid=(T // tm,),
        in_specs=[
            pl.BlockSpec((HALO, D), lambda i: (jnp.maximum(i * halo_blocks - 1, 0), 0)),
            pl.BlockSpec((tm, D), row),
            pl.BlockSpec((tm, D), row),
            pl.BlockSpec((2, D, dff), lambda i: (0, 0, 0), pipeline_mode=once),
            pl.BlockSpec((2, 3, dff), lambda i: (0, 0, 0), pipeline_mode=once),
            pl.BlockSpec((2, 1, dff), lambda i: (0, 0, 0), pipeline_mode=once),
            pl.BlockSpec((dff, D), lambda i: (0, 0), pipeline_mode=once),
            pl.BlockSpec((None, 6, D), lambda i: (i // per_seq, 0, 0)),
            pl.BlockSpec((None, 6, D), lambda i: (i // per_seq, 0, 0)),
            pl.BlockSpec((1, D), lambda i: (0, 0)),
        ],
        out_specs=[pl.BlockSpec((tm, D), row), pl.BlockSpec((tm, D), row)],
        out_shape=[jax.ShapeDtypeStruct((T, D), F32), jax.ShapeDtypeStruct((T, D), BF16)],
        scratch_shapes=[pltpu.VMEM((tm, dff), BF16)],
        compiler_params=_params(("arbitrary",)),
        name="conv_glu_ffn",
    )(h2, h2, x2, wup, cw, cb, wdn, mod_l, mod_next, g_next.reshape(1, D))


def _rope_tables(seq):
    inv = 1.0 / (ROPE_THETA ** (jnp.arange(0, DQ, 2, dtype=F32) / DQ))
    ang = jnp.arange(seq, dtype=F32)[:, None] * inv[None, :]
    cos = jnp.cos(ang)
    sin = jnp.sin(ang)
    cos_map = jnp.concatenate([cos, cos], axis=-1)
    sin_map = jnp.concatenate([-sin, sin], axis=-1)
    reps = GW // DQ
    return jnp.tile(cos_map, (1, reps)), jnp.tile(sin_map, (1, reps))


def kernel(x, c, ada_w, ada_b, norm1_g, norm2_g, w_in, w_out, rw_mu, rw_w0, rw_w_up, rw_a0, rw_a_up, rw_g_up, rw_k_k, rw_k_a, rw_r_k, rw_ln_g, rw_ln_b, df_lam_q1, df_lam_k1, df_lam_q2, df_lam_k2, df_q_g, df_k_g, df_sub_g, sg_w, sg_b, sg_ln_g, sg_ln_b, fx_q_g, fx_k_g, fx_f_b, ffn_up, ffn_conv, ffn_conv_b, ffn_down):
    Bn, S, D = x.shape
    L = ada_w.shape[0]
    T = Bn * S
    dff = ffn_down.shape[1]
    tm = min(ROW_TILE, S)
    tile = min(ATT_TILE, S)

    mod = _ada_mod(c, ada_w, ada_b).reshape(L, Bn, 6, D)
    cos, sin = _rope_tables(S)
    cost, sint = cos.T, sin.T
    x2 = x.reshape(T, D)

    oa = 0
    ob = oa + RW_COLS
    oc = ob + 3 * GW
    od = oc + 2 * GW

    for l in range(L):
        lambda_init = 0.8 - 0.6 * math.exp(-0.3 * l)
        wl = w_in[l].astype(BF16)
        if l == 0:
            h = _norm_mod(x2, mod[l], norm1_g[l], seq=S, row0=0, tm=tm)

        lora_w = jnp.zeros((LANES, 3 * GW), F32)
        lora_w = lora_w.at[0:RW_DECAY_RANK, 0:GW].set(rw_w_up[l])
        lora_w = lora_w.at[RW_DECAY_RANK:RW_DECAY_RANK + RW_A_RANK, GW:2 * GW].set(rw_a_up[l])
        lora_w = lora_w.at[RW_DECAY_RANK + RW_A_RANK:LANES, 2 * GW:3 * GW].set(rw_g_up[l])
        vec = jnp.stack([rw_w0[l], rw_a0[l], rw_k_k[l], rw_k_a[l], rw_r_k[l].reshape(GW),
                         rw_ln_g[l], rw_ln_b[l], jnp.zeros((GW,), F32)])
        ya = _rwkv_mixer(h, wl[:, oa:ob], rw_mu[l].reshape(1, RW_COLS), lora_w, vec,
                         batch=Bn, seq=S, tt=tm, mode=RW_MODE)

        gk = jnp.tile(df_k_g[l], GW // DQ).reshape(1, GW)
        gqt = jnp.broadcast_to((jnp.tile(df_q_g[l], GW // DQ) * (DQ ** -0.5 * LOG2E))[:, None], (GW, tm))
        qb, kb, vb = _diff_prep(h, wl[:, ob + GW:ob + 2 * GW], wl[:, ob:ob + GW].T, wl[:, ob + 2 * GW:oc].T,
                                gk, gqt, cos, sin, cost, sint, seq=S, tm=tm, tile=tile)
        lam_vecs = jnp.stack([df_lam_q1[l], df_lam_k1[l], df_lam_q2[l], df_lam_k2[l]])
        sub_gain = jnp.broadcast_to((jnp.tile(df_sub_g[l], HG) * (1.0 - lambda_init))[:, None], (GW, tile))
        yb = _attention(qb, kb, vb, lam_vecs, sub_gain, batch=Bn, seq=S, tile=tile,
                        n_maps=2, lambda_init=lambda_init)

        sg_bias = jnp.repeat(sg_b[l].T, HEAD_DIM, axis=1)
        yc = _gmlp_mixer(h, wl[:, oc:od], sg_w[l], sg_bias,
                         jnp.stack([sg_ln_g[l], sg_ln_b[l]]), tm=tm)

        wf = jnp.zeros((D, LANES), BF16).at[:, 0:HG].set(wl[:, od + 3 * GW:od + 3 * GW + HG])
        fb = jnp.zeros((1, LANES), F32).at[0, 0:HG].set(fx_f_b[l])
        gkd = jnp.tile(fx_k_g[l], HG).reshape(1, GW)
        gqd = jnp.broadcast_to((jnp.tile(fx_q_g[l], HG) * (HEAD_DIM ** -0.5 * LOG2E))[:, None], (GW, tm))
        qd, kd, vd = _fox_prep(h, wl[:, od + GW:od + 2 * GW], wf, wl[:, od:od + GW].T,
                               wl[:, od + 2 * GW:od + 3 * GW].T, gkd, gqd, fb,
                               batch=Bn, seq=S, tm=tm, tile=tile)
        yd = _attention(qd, kd, vd, jnp.zeros((8, LANES), F32), jnp.zeros((8, LANES), F32),
                        batch=Bn, seq=S, tile=tile, n_maps=1, lambda_init=0.0)

        x2, h2 = _out_proj((ya, yb, yc, yd), w_out[l].astype(BF16).reshape(N_MIXERS, GW, D),
                           x2, mod[l], norm2_g[l], seq=S, tm=tm)

        wup = ffn_up[l].astype(BF16).reshape(D, 2, dff).transpose(1, 0, 2)
        cw = ffn_conv[l].reshape(3, 2, dff).transpose(1, 0, 2)
        cb = ffn_conv_b[l].reshape(2, 1, dff)
        nl = (l + 1) % L
        x2, h = _ffn(h2, x2, wup, cw, cb, ffn_down[l].astype(BF16), mod[l], mod[nl], norm1_g[nl],
                     seq=S, tm=tm)

    return x2.reshape(Bn, S, D)
```

```python
import functools
import math

import jax
import jax.numpy as jnp
from jax import lax
from jax.experimental import pallas as pl
from jax.experimental.pallas import tpu as pltpu

F32 = jnp.float32
BF16 = jnp.bfloat16
HIGHEST = lax.Precision.HIGHEST

N_MIXERS = 4
HEAD_DIM = 64
HG = 4
GW = HG * HEAD_DIM
DQ = HEAD_DIM // 2
RW_DECAY_RANK = 32
RW_A_RANK = 32
RW_GATE_RANK = 64
RW_COLS = 3 * GW + RW_DECAY_RANK + RW_A_RANK + RW_GATE_RANK
RW_GN_EPS = 64e-5
SG_CHUNK = 128
ATT_CHUNK = 64
ROPE_THETA = 10000.0
EPS = 1e-6
LOG2E = math.log2(math.e)
NEG_INF = -1e30

LANES = 128
V7X_VMEM_BYTES = 64 * 1024 * 1024
VMEM_LIMIT = 56 * 1024 * 1024

ROW_TILE = 512
RW_CHUNK = 64
ATT_TILE = 256
SLAB = 128
VROWS = HEAD_DIM + 16
RW_MODE = "lo"

NN = (((1,), (0,)), ((), ()))
NT = (((1,), (1,)), ((), ()))
TN = (((0,), (0,)), ((), ()))


def _params(sem):
    return pltpu.CompilerParams(dimension_semantics=sem, vmem_limit_bytes=VMEM_LIMIT)


def _mm(a, b, dims=NN, mode="hi"):
    if mode == "hi":
        return lax.dot_general(a.astype(F32), b.astype(F32), dims, precision=HIGHEST,
                               preferred_element_type=F32)
    if mode == "lo":
        return lax.dot_general(a.astype(BF16), b.astype(BF16), dims, preferred_element_type=F32)
    ah = a.astype(BF16)
    al = (a - ah.astype(F32)).astype(BF16)
    bh = b.astype(BF16)
    bl = (b - bh.astype(F32)).astype(BF16)
    dg = functools.partial(lax.dot_general, dimension_numbers=dims, preferred_element_type=F32)
    return dg(ah, bh) + (dg(ah, bl) + dg(al, bh))


def _mm_exact(a, b, dims=NN, left=True):
    x = a if left else b
    xh = x.astype(BF16)
    xl = (x - xh.astype(F32)).astype(BF16)
    dg = functools.partial(lax.dot_general, dimension_numbers=dims, preferred_element_type=F32)
    if left:
        bb = b.astype(BF16)
        return dg(xh, bb) + dg(xl, bb)
    ab = a.astype(BF16)
    return dg(ab, xh) + dg(ab, xl)


def _iota(shape, dim):
    return lax.broadcasted_iota(jnp.int32, shape, dim)


def _group_matrix(n, group, value=1.0):
    same = (_iota((n, n), 0) // group) == (_iota((n, n), 1) // group)
    return jnp.where(same, value, 0.0).astype(F32)


def _softplus(x):
    return jnp.maximum(x, 0.0) + jnp.log1p(jnp.exp(-jnp.abs(x)))


def _sigmoid(x):
    return 1.0 / (1.0 + jnp.exp(-x))


def _ada_kernel(c_ref, w_ref, b_ref, o_ref):
    c = c_ref[...]
    cond = c * _sigmoid(c)
    o_ref[...] = _mm(cond, w_ref[...], NN, "hi") + b_ref[...]


def _ada_mod(c, ada_w, ada_b):
    L, D, D6 = ada_w.shape
    Bn = c.shape[0]
    tn = D6 // 4
    return pl.pallas_call(
        _ada_kernel,
        grid=(L, D6 // tn),
        in_specs=[
            pl.BlockSpec((Bn, D), lambda l, j: (0, 0)),
            pl.BlockSpec((None, D, tn), lambda l, j: (l, 0, j)),
            pl.BlockSpec((None, 1, tn), lambda l, j: (l, 0, j)),
        ],
        out_specs=pl.BlockSpec((None, Bn, tn), lambda l, j: (l, 0, j)),
        out_shape=jax.ShapeDtypeStruct((L, Bn, D6), F32),
        compiler_params=_params(("arbitrary", "arbitrary")),
        name="ada_mod",
    )(c, ada_w, ada_b.reshape(L, 1, D6))


def _modulated_norm(x, g, shift, scale):
    y = x * lax.rsqrt(jnp.mean(x * x, axis=-1, keepdims=True) + EPS) * g
    return y * (1.0 + scale) + shift


def _norm_kernel(x_ref, mod_ref, g_ref, o_ref, *, row0):
    mod = mod_ref[...]
    h = _modulated_norm(x_ref[...], g_ref[...], mod[row0:row0 + 1], mod[row0 + 1:row0 + 2])
    o_ref[...] = h.astype(BF16)


def _norm_mod(x2, mod_l, g, *, seq, row0, tm):
    T, D = x2.shape
    per_seq = seq // tm
    return pl.pallas_call(
        functools.partial(_norm_kernel, row0=row0),
        grid=(T // tm,),
        in_specs=[
            pl.BlockSpec((tm, D), lambda i: (i, 0)),
            pl.BlockSpec((None, 6, D), lambda i: (i // per_seq, 0, 0)),
            pl.BlockSpec((1, D), lambda i: (0, 0)),
        ],
        out_specs=pl.BlockSpec((tm, D), lambda i: (i, 0)),
        out_shape=jax.ShapeDtypeStruct((T, D), BF16),
        compiler_params=_params(("arbitrary",)),
        name="norm_mod",
    )(x2, mod_l, g.reshape(1, D))


def _blockdiag(mp, head_lane_masks):
    return jnp.concatenate([jnp.where(m, mp, 0.0) for m in head_lane_masks], axis=0)


def _apply_packed(mp, rhs, head_lane_masks, C, mode):
    full = _mm(mp, rhs, TN, mode)
    out = jnp.where(head_lane_masks[0], full[0:C], 0.0)
    for h in range(1, HG):
        out = out + jnp.where(head_lane_masks[h], full[h * C:(h + 1) * C], 0.0)
    return out


def _rwkv_kernel(h_ref, wa_ref, mu_ref, lora_ref, vec_ref, o_ref,
                 carry_ref, st_ref, hs_ref, *, C, mode):
    TT = h_ref.shape[0]

    @pl.when(pl.program_id(1) == 0)
    def _():
        carry_ref[...] = jnp.zeros_like(carry_ref)
        st_ref[...] = jnp.zeros_like(st_ref)

    vec = vec_ref[...]
    w0, a0, k_k, k_a, r_k, ln_g, ln_b = (vec[i:i + 1] for i in range(7))

    pa = jnp.dot(h_ref[...], wa_ref[...], preferred_element_type=F32)
    prev = pltpu.roll(pa, 1, axis=0)
    prev = jnp.where(_iota((TT, 1), 0) == 0, carry_ref[...], prev)
    carry_ref[...] = pa[TT - 1:TT]
    pa = pa + (prev - pa) * mu_ref[...]

    r = pa[:, 0:GW]
    k = pa[:, GW:2 * GW]
    v = pa[:, 2 * GW:3 * GW]
    lo = pa[:, 3 * GW:3 * GW + LANES]
    lane = _iota((1, LANES), 1)
    act = jnp.where(lane < RW_DECAY_RANK, jnp.tanh(lo),
                    jnp.where(lane < RW_DECAY_RANK + RW_A_RANK, lo, _sigmoid(lo)))
    lora = _mm(act, lora_ref[...], NN, "x3")
    w_log = -_softplus(-(w0 + lora[:, 0:GW])) - 0.5
    lw = -jnp.exp(w_log)
    a = _sigmoid(a0 + lora[:, GW:2 * GW])
    gate = lora[:, 2 * GW:3 * GW]

    gsum = _group_matrix(GW, HEAD_DIM)
    kk = k * k_k
    kk = kk / jnp.maximum(jnp.sqrt(_mm(kk * kk, gsum, NN, "lo")), 1e-12)
    k2 = k * (1.0 + (a - 1.0) * k_a)
    bonus = _mm(r * k2 * r_k, gsum, NN, "lo") * v

    aa = -kk
    b = kk * a

    lane_head = _iota((1, GW), 1) // HEAD_DIM
    hmask = [lane_head == h for h in range(HG)]
    s_idx = _iota((C, GW), 0)
    t_idx = _iota((C, GW), 1) % C
    strict = s_idx < t_idx
    incl = s_idx <= t_idx
    eye_p = jnp.where(s_idx == t_idx, 1.0, 0.0).astype(F32)
    l_incl = jnp.where(_iota((C, C), 0) >= _iota((C, C), 1), 1.0, 0.0).astype(F32)
    bd = (_iota((GW, GW), 0) // HEAD_DIM) == (_iota((GW, GW), 1) // HEAD_DIM)
    eye_k = jnp.where(_iota((GW, GW), 0) == _iota((GW, GW), 1), 1.0, 0.0).astype(F32)
    chunks = range(TT // C)
    sl = lambda x, c: x[c * C:(c + 1) * C]

    cl = jnp.concatenate([_mm_exact(l_incl, sl(lw, c), left=False) for c in chunks], axis=0)
    g_in = jnp.exp(cl)
    g_inv = jnp.exp(-cl)
    at = aa * jnp.exp(cl - lw)
    rt = r * g_in
    bt = b * g_inv
    kt = k2 * g_inv

    bks = [jnp.concatenate([sl(bt, c), sl(kt, c)], axis=0) for c in chunks]
    xs = [_mm(bks[c], jnp.concatenate([_blockdiag(sl(at, c), hmask), _blockdiag(sl(rt, c), hmask)], axis=0),
              NT, mode) for c in chunks]
    n_p = [jnp.where(strict, x[0:C, 0:GW], 0.0) for x in xs]
    ak_p = [jnp.where(strict, x[C:2 * C, 0:GW], 0.0) for x in xs]
    rbk_p = [jnp.concatenate([jnp.where(incl, x[0:C, GW:2 * GW], 0.0),
                              jnp.where(incl, x[C:2 * C, GW:2 * GW], 0.0)], axis=0) for x in xs]

    p_p = [eye_p + n for n in n_p]
    q_p = n_p
    for _ in range(int(math.log2(C)) - 1):
        q_p = [_mm(q, _blockdiag(q, hmask), NN, mode) for q in q_p]
        p_p = [p + _mm(p, _blockdiag(q, hmask), NN, mode) for p, q in zip(p_p, q_p)]

    ta = [_apply_packed(p_p[c], sl(at, c), hmask, C, mode) for c in chunks]
    wv = [_apply_packed(ak_p[c], sl(v, c), hmask, C, mode) for c in chunks]
    tw = [_apply_packed(p_p[c], wv[c], hmask, C, mode) for c in chunks]
    g_end = [g_in[(c + 1) * C - 1:(c + 1) * C] for c in chunks]
    m_t = [(eye_k + jnp.where(bd, _mm(ta[c], sl(bt, c), TN, mode), 0.0)) * g_end[c] for c in chunks]
    g_t = [jnp.where(bd, _mm(jnp.concatenate([tw[c], sl(v, c)], axis=0), bks[c], TN, mode), 0.0) * g_end[c]
           for c in chunks]

    ht = st_ref[...]
    for c in chunks:
        hs_ref[c] = ht
        ht = _mm(ht, m_t[c], NN, mode) + g_t[c]
    st_ref[...] = ht

    ys = []
    for c in chunks:
        sh = _mm(jnp.concatenate([ta[c], sl(rt, c)], axis=0), hs_ref[c], NT, mode)
        uv = jnp.concatenate([sh[0:C] + tw[c], sl(v, c)], axis=0)
        ys.append(sh[C:2 * C] + _apply_packed(rbk_p[c], uv, hmask, C, mode))
    y = jnp.concatenate(ys, axis=0)

    gmean = _group_matrix(GW, HEAD_DIM, 1.0 / HEAD_DIM)
    yc = y - _mm(y, gmean, NN, "lo")
    yn = yc * lax.rsqrt(_mm(yc * yc, gmean, NN, "lo") + RW_GN_EPS) * ln_g + ln_b
    o_ref[...] = ((yn + bonus) * gate).astype(BF16)


def _rwkv_mixer(h, wa, mu, lora_w, vec, *, batch, seq, tt, mode):
    T, D = h.shape
    per_seq = seq // tt
    C = RW_CHUNK
    return pl.pallas_call(
        functools.partial(_rwkv_kernel, C=C, mode=mode),
        grid=(batch, per_seq),
        in_specs=[
            pl.BlockSpec((tt, D), lambda b, i: (b * per_seq + i, 0)),
            pl.BlockSpec((D, RW_COLS), lambda b, i: (0, 0)),
            pl.BlockSpec((1, RW_COLS), lambda b, i: (0, 0)),
            pl.BlockSpec((LANES, 3 * GW), lambda b, i: (0, 0)),
            pl.BlockSpec((8, GW), lambda b, i: (0, 0)),
        ],
        out_specs=pl.BlockSpec((tt, GW), lambda b, i: (b * per_seq + i, 0)),
        out_shape=jax.ShapeDtypeStruct((T, GW), BF16),
        scratch_shapes=[pltpu.VMEM((1, RW_COLS), F32), pltpu.VMEM((GW, GW), F32),
                        pltpu.VMEM((tt // C, GW, GW), F32)],
        compiler_params=_params(("arbitrary", "arbitrary")),
        name="rwkv7_mixer",
    )(h, wa, mu, lora_w, vec)


def _place_heads_lanes(n_in, width):
    src = _iota((n_in, HG * SLAB), 0)
    dst = _iota((n_in, HG * SLAB), 1)
    ok = ((dst // SLAB) == (src // width)) & ((dst % SLAB) == (src % width)) & ((dst % SLAB) < width)
    return jnp.where(ok, 1.0, 0.0).astype(BF16)


def _slab_rows(xt, extra):
    parts = []
    for h in range(HG):
        parts.append(xt[h * HEAD_DIM:(h + 1) * HEAD_DIM])
        parts.append(extra)
    return jnp.concatenate(parts, axis=0)


def _diff_prep_kernel(h_ref, wk_ref, wqt_ref, wvt_ref, gk_ref, gqt_ref,
                      cos_ref, sin_ref, cost_ref, sint_ref, q_out, k_out, v_out, *, tile):
    tm = h_ref.shape[0]
    hb = h_ref[...]
    k = jnp.dot(hb, wk_ref[...], preferred_element_type=F32)
    qt = lax.dot_general(wqt_ref[...], hb, NT, preferred_element_type=F32)
    vt = lax.dot_general(wvt_ref[...], hb, NT, preferred_element_type=F32)
    gmean = _group_matrix(GW, DQ, 1.0 / DQ)

    kn = k * lax.rsqrt(_mm_exact(k * k, gmean) + EPS) * gk_ref[...]
    first_half = (_iota((1, GW), 1) % DQ) < (DQ // 2)
    partner = jnp.where(first_half, pltpu.roll(kn, GW - DQ // 2, axis=1), pltpu.roll(kn, DQ // 2, axis=1))
    kr = kn * cos_ref[...] + partner * sin_ref[...]
    k_out[...] = jnp.dot(kr.astype(BF16), _place_heads_lanes(GW, HEAD_DIM),
                         preferred_element_type=F32).astype(BF16)

    qn = qt * lax.rsqrt(_mm_exact(gmean, qt * qt, left=False) + EPS) * gqt_ref[...]
    first_half_t = (_iota((GW, 1), 0) % DQ) < (DQ // 2)
    partner_t = jnp.where(first_half_t, pltpu.roll(qn, GW - DQ // 2, axis=0), pltpu.roll(qn, DQ // 2, axis=0))
    qr = (qn * cost_ref[...] + partner_t * sint_ref[...]).astype(BF16)
    qs = _slab_rows(qr, jnp.zeros((SLAB - HEAD_DIM, tm), BF16))
    vb = _slab_rows(vt.astype(BF16),
                    jnp.where(_iota((VROWS - HEAD_DIM, tm), 0) == 0, 1.0, 0.0).astype(BF16))
    for c in range(tm // tile):
        q_out[c] = qs[:, c * tile:(c + 1) * tile]
        v_out[c] = vb[:, c * tile:(c + 1) * tile]


def _diff_prep(h, wk, wqt, wvt, gk, gqt, cos, sin, cost, sint, *, seq, tm, tile):
    T, D = h.shape
    per_seq = seq // tm
    nt = tm // tile
    const = lambda i: (0, 0)
    return pl.pallas_call(
        functools.partial(_diff_prep_kernel, tile=tile),
        grid=(T // tm,),
        in_specs=[
            pl.BlockSpec((tm, D), lambda i: (i, 0)),
            pl.BlockSpec((D, GW), const),
            pl.BlockSpec((GW, D), const),
            pl.BlockSpec((GW, D), const),
            pl.BlockSpec((1, GW), const),
            pl.BlockSpec((GW, tm), const),
            pl.BlockSpec((tm, GW), lambda i: (i % per_seq, 0)),
            pl.BlockSpec((tm, GW), lambda i: (i % per_seq, 0)),
            pl.BlockSpec((GW, tm), lambda i: (0, i % per_seq)),
            pl.BlockSpec((GW, tm), lambda i: (0, i % per_seq)),
        ],
        out_specs=[
            pl.BlockSpec((nt, HG * SLAB, tile), lambda i: (i, 0, 0)),
            pl.BlockSpec((tm, HG * SLAB), lambda i: (i, 0)),
            pl.BlockSpec((nt, HG * VROWS, tile), lambda i: (i, 0, 0)),
        ],
        out_shape=[
            jax.ShapeDtypeStruct((T // tile, HG * SLAB, tile), BF16),
            jax.ShapeDtypeStruct((T, HG * SLAB), BF16),
            jax.ShapeDtypeStruct((T // tile, HG * VROWS, tile), BF16),
        ],
        compiler_params=_params(("arbitrary",)),
        name="diff_prep",
    )(h, wk, wqt, wvt, gk, gqt, cos, sin, cost, sint)


def _fox_prep_kernel(h_ref, wk_ref, wf_ref, wqt_ref, wvt_ref, gk_ref, gqt_ref, fb_ref,
                     q_out, k_out, v_out, fcarry_ref, *, tile):
    tm = h_ref.shape[0]

    @pl.when(pl.program_id(1) == 0)
    def _():
        fcarry_ref[...] = jnp.zeros_like(fcarry_ref)

    hb = h_ref[...]
    k = jnp.dot(hb, wk_ref[...], preferred_element_type=F32)
    fl = jnp.dot(hb, wf_ref[...], preferred_element_type=F32)
    qt = lax.dot_general(wqt_ref[...], hb, NT, preferred_element_type=F32)
    vt = lax.dot_general(wvt_ref[...], hb, NT, preferred_element_type=F32)
    gmean = _group_matrix(GW, HEAD_DIM, 1.0 / HEAD_DIM)

    kn = (k * lax.rsqrt(_mm_exact(k * k, gmean) + EPS) * gk_ref[...]).astype(BF16)

    log_f = -_softplus(-(fl + fb_ref[...]))
    l_incl = jnp.where(_iota((tm, tm), 0) >= _iota((tm, tm), 1), 1.0, 0.0).astype(F32)
    cum = _mm_exact(l_incl, log_f, left=False) + fcarry_ref[...]
    fcarry_ref[...] = cum[tm - 1:tm]
    nf = cum * (-LOG2E)
    pieces = []
    for _ in range(3):
        p = nf.astype(BF16)
        pieces.append(p)
        nf = nf - p.astype(F32)
    src = _iota((LANES, HG * SLAB), 0)
    dst = _iota((LANES, HG * SLAB), 1)
    ks = jnp.dot(kn, _place_heads_lanes(GW, HEAD_DIM), preferred_element_type=F32)
    for i, p in enumerate(pieces):
        place = jnp.where((src < HG) & (dst == src * SLAB + HEAD_DIM + i), 1.0, 0.0).astype(BF16)
        ks = ks + jnp.dot(p, place, preferred_element_type=F32)
    k_out[...] = ks.astype(BF16)

    qn = (qt * lax.rsqrt(_mm_exact(gmean, qt * qt, left=False) + EPS) * gqt_ref[...]).astype(BF16)
    ones_rows = jnp.where(_iota((SLAB - HEAD_DIM, tm), 0) < 3, 1.0, 0.0).astype(BF16)
    qs = _slab_rows(qn, ones_rows)
    vb = _slab_rows(vt.astype(BF16),
                    jnp.where(_iota((VROWS - HEAD_DIM, tm), 0) == 0, 1.0, 0.0).astype(BF16))
    for c in range(tm // tile):
        q_out[c] = qs[:, c * tile:(c + 1) * tile]
        v_out[c] = vb[:, c * tile:(c + 1) * tile]


def _fox_prep(h, wk, wf, wqt, wvt, gk, gqt, fb, *, batch, seq, tm, tile):
    T, D = h.shape
    per_seq = seq // tm
    nt = tm // tile
    const = lambda b, i: (0, 0)
    row = lambda b, i: (b * per_seq + i, 0)
    return pl.pallas_call(
        functools.partial(_fox_prep_kernel, tile=tile),
        grid=(batch, per_seq),
        in_specs=[
            pl.BlockSpec((tm, D), row),
            pl.BlockSpec((D, GW), const),
            pl.BlockSpec((D, LANES), const),
            pl.BlockSpec((GW, D), const),
            pl.BlockSpec((GW, D), const),
            pl.BlockSpec((1, GW), const),
            pl.BlockSpec((GW, tm), const),
            pl.BlockSpec((1, LANES), const),
        ],
        out_specs=[
            pl.BlockSpec((nt, HG * SLAB, tile), lambda b, i: (b * per_seq + i, 0, 0)),
            pl.BlockSpec((tm, HG * SLAB), row),
            pl.BlockSpec((nt, HG * VROWS, tile), lambda b, i: (b * per_seq + i, 0, 0)),
        ],
        out_shape=[
            jax.ShapeDtypeStruct((T // tile, HG * SLAB, tile), BF16),
            jax.ShapeDtypeStruct((T, HG * SLAB), BF16),
            jax.ShapeDtypeStruct((T // tile, HG * VROWS, tile), BF16),
        ],
        scratch_shapes=[pltpu.VMEM((1, LANES), F32)],
        compiler_params=_params(("arbitrary", "arbitrary")),
        name="fox_prep",
    )(h, wk, wf, wqt, wvt, gk, gqt, fb)


def _attn_kernel(q_ref, k_ref, v_ref, aux_ref, gain_ref, o_ref, qz_s, m_s, acc_s,
                 s_a, s_b, p_a, p_b, al_a, al_b, *, n_maps, tile, lambda_init):
    qi = pl.program_id(1)
    row = _iota((tile, tile), 0)
    col = _iota((tile, tile), 1)
    if n_maps == 2:
        diag_ok = (row // ATT_CHUNK) <= (col // ATT_CHUNK)
    else:
        diag_ok = row <= col

    slab_row = _iota((SLAB, 1), 0)
    for h in range(HG):
        qh = q_ref[h * SLAB:(h + 1) * SLAB, :]
        if n_maps == 2:
            qz_s[h, :, 0:tile] = jnp.where(slab_row < DQ, qh, jnp.zeros_like(qh))
            qz_s[h, :, tile:2 * tile] = jnp.where((slab_row >= DQ) & (slab_row < 2 * DQ), qh,
                                                  jnp.zeros_like(qh))
        else:
            qz_s[h] = qh
    m_s[...] = jnp.full(m_s.shape, NEG_INF, F32)
    acc_s[...] = jnp.zeros(acc_s.shape, F32)
    p_b[...] = jnp.zeros(p_b.shape, BF16)
    al_b[...] = jnp.zeros(al_b.shape, F32)

    buf_a = (s_a, p_a, al_a)
    buf_b = (s_b, p_b, al_b)

    all_heads = tuple(range(HG))

    def scores(j, s_buf, heads=all_heads):
        for h in heads:
            kj = k_ref[pl.ds(pl.multiple_of(j * tile, tile), tile), h * SLAB:(h + 1) * SLAB]
            s_buf[h] = jnp.dot(kj, qz_s[h], preferred_element_type=F32)

    def value_update(j, p_buf, al_buf, heads=all_heads):
        for h in heads:
            vj = v_ref[j, h * VROWS:(h + 1) * VROWS, :]
            for mp in range(n_maps):
                c = h * n_maps + mp
                rows = slice(c * VROWS, (c + 1) * VROWS)
                pv = jnp.dot(vj, p_buf[c], preferred_element_type=F32)
                acc_s[rows, :] = al_buf[c:c + 1, :] * acc_s[rows, :] + pv

    def softmax(s_buf, p_buf, al_buf, masked, heads=all_heads):
        for h in heads:
            for mp in range(n_maps):
                c = h * n_maps + mp
                s = s_buf[h, :, mp * tile:(mp + 1) * tile]
                if masked:
                    s = jnp.where(diag_ok, s, NEG_INF)
                m_old = m_s[c:c + 1, :]
                m_new = jnp.maximum(m_old, jnp.max(s, axis=0, keepdims=True))
                alpha = jnp.exp2(m_old - m_new)
                p = jnp.exp2(s - m_new)
                m_s[c:c + 1, :] = m_new
                al_buf[c:c + 1, :] = alpha
                p_buf[c] = p.astype(BF16)

    def step(j, cur, nxt):
        for h in all_heads:
            scores(j + 1, nxt[0], (h,))
            value_update(jnp.maximum(j - 1, 0), nxt[1], nxt[2], (h,))
            softmax(cur[0], cur[1], cur[2], False, (h,))

    def last(j, cur, nxt):
        value_update(jnp.maximum(j - 1, 0), nxt[1], nxt[2])
        softmax(cur[0], cur[1], cur[2], True)
        value_update(j, cur[1], cur[2])

    scores(0, s_a)

    def pair(jj, carry):
        step(2 * jj, buf_a, buf_b)
        step(2 * jj + 1, buf_b, buf_a)
        return carry

    lax.fori_loop(0, lax.shift_right_logical(qi, 1), pair, 0)
    odd = (qi & 1) == 1

    @pl.when(odd)
    def _():
        step(qi - 1, buf_a, buf_b)
        last(qi, buf_b, buf_a)

    @pl.when(jnp.logical_not(odd))
    def _():
        last(qi, buf_a, buf_b)

    if n_maps == 2:
        aux = aux_ref[...]
        lam = (jnp.exp(jnp.sum(aux[0:1] * aux[1:2], axis=1, keepdims=True))
               - jnp.exp(jnp.sum(aux[2:3] * aux[3:4], axis=1, keepdims=True)) + lambda_init)
    outs = []
    for h in range(HG):
        c = h * n_maps
        normalized = lambda cc: (acc_s[cc * VROWS:cc * VROWS + HEAD_DIM, :]
                                 / acc_s[cc * VROWS + HEAD_DIM:cc * VROWS + HEAD_DIM + 1, :])
        o = normalized(c)
        if n_maps == 2:
            o = o - lam * normalized(c + 1)
            o = o * lax.rsqrt(jnp.mean(o * o, axis=0, keepdims=True) + EPS)
        outs.append(o)
    ot = jnp.concatenate(outs, axis=0)
    if n_maps == 2:
        ot = ot * gain_ref[...]
    o_ref[...] = ot.T.astype(BF16)


def _attention(q3, k2, v3, aux, gain, *, batch, seq, tile, n_maps, lambda_init):
    nq = seq // tile
    T = batch * seq
    n_chains = HG * n_maps
    scratch = [pltpu.VMEM((HG, SLAB, n_maps * tile), BF16),
               pltpu.VMEM((n_chains, tile), F32),
               pltpu.VMEM((n_chains * VROWS, tile), F32),
               pltpu.VMEM((HG, tile, n_maps * tile), F32),
               pltpu.VMEM((HG, tile, n_maps * tile), F32),
               pltpu.VMEM((n_chains, tile, tile), BF16),
               pltpu.VMEM((n_chains, tile, tile), BF16),
               pltpu.VMEM((n_chains, tile), F32),
               pltpu.VMEM((n_chains, tile), F32)]
    return pl.pallas_call(
        functools.partial(_attn_kernel, n_maps=n_maps, tile=tile, lambda_init=lambda_init),
        grid=(batch, nq),
        in_specs=[
            pl.BlockSpec((None, HG * SLAB, tile), lambda b, i: (b * nq + i, 0, 0)),
            pl.BlockSpec((seq, HG * SLAB), lambda b, i: (b, 0)),
            pl.BlockSpec((nq, HG * VROWS, tile), lambda b, i: (b, 0, 0)),
            pl.BlockSpec(aux.shape, lambda b, i: (0, 0)),
            pl.BlockSpec(gain.shape, lambda b, i: (0, 0)),
        ],
        out_specs=pl.BlockSpec((tile, GW), lambda b, i: (b * nq + i, 0)),
        out_shape=jax.ShapeDtypeStruct((T, GW), BF16),
        scratch_shapes=scratch,
        compiler_params=_params(("arbitrary", "arbitrary")),
        name="diff_attention" if n_maps == 2 else "forgetting_attention",
    )(q3, k2, v3, aux, gain)


def _erf_gelu(x):
    return 0.5 * x * (1.0 + lax.erf(x * (1.0 / math.sqrt(2.0))))


def _gmlp_kernel(h_ref, wc_ref, sgw_ref, bias_ref, ln_ref, o_ref):
    tm = h_ref.shape[0]
    pc = _erf_gelu(jnp.dot(h_ref[...], wc_ref[...], preferred_element_type=F32))
    u = pc[:, 0:GW]
    v = pc[:, GW:2 * GW]
    mu = jnp.mean(v, axis=-1, keepdims=True)
    vc = v - mu
    var = jnp.mean(vc * vc, axis=-1, keepdims=True)
    ln = ln_ref[...]
    vn = (vc * lax.rsqrt(var + EPS) * ln[0:1] + ln[1:2]).astype(BF16)

    causal = _iota((SG_CHUNK, SG_CHUNK), 0) >= _iota((SG_CHUNK, SG_CHUNK), 1)
    w = jnp.concatenate([jnp.where(causal, sgw_ref[g], 0.0) for g in range(HG)], axis=0).astype(BF16)
    lane_head = _iota((1, GW), 1) // HEAD_DIM
    bias = bias_ref[...]
    for n in range(tm // SG_CHUNK):
        rows = slice(n * SG_CHUNK, (n + 1) * SG_CHUNK)
        full = jnp.dot(w, vn[rows], preferred_element_type=F32)
        sv = bias
        for g in range(HG):
            sv = sv + jnp.where(lane_head == g, full[g * SG_CHUNK:(g + 1) * SG_CHUNK], 0.0)
        o_ref[rows, :] = (u[rows] * sv).astype(BF16)


def _gmlp_mixer(h, wc, sg_w, bias, ln, *, tm):
    T, D = h.shape
    return pl.pallas_call(
        _gmlp_kernel,
        grid=(T // tm,),
        in_specs=[
            pl.BlockSpec((tm, D), lambda i: (i, 0)),
            pl.BlockSpec((D, 2 * GW), lambda i: (0, 0)),
            pl.BlockSpec((HG, SG_CHUNK, SG_CHUNK), lambda i: (0, 0, 0)),
            pl.BlockSpec((SG_CHUNK, GW), lambda i: (0, 0)),
            pl.BlockSpec((2, GW), lambda i: (0, 0)),
        ],
        out_specs=pl.BlockSpec((tm, GW), lambda i: (i, 0)),
        out_shape=jax.ShapeDtypeStruct((T, GW), BF16),
        compiler_params=_params(("arbitrary",)),
        name="gmlp_mixer",
    )(h, wc, sg_w, bias, ln)


HALO = 16


FFN_COLS = 256


def _ffn_kernel(yah, ybh, ych, ydh, ya, yb, yc, yd, xh_ref, x_ref, wo_ref, g2_ref,
                wup_ref, cw_ref, cb_ref, wdn_ref, mod_ref, modn_ref, gn_ref,
                o_ref, hn_ref, act_ref, *, per_seq):
    i = pl.program_id(0)
    tm = x_ref.shape[0]
    dff = wdn_ref.shape[0]
    mod = mod_ref[...]

    proj = None
    for m, (halo, main) in enumerate(((yah, ya), (ybh, yb), (ych, yc), (ydh, yd))):
        ym = jnp.concatenate([halo[...], main[...]], axis=0)
        part = jnp.dot(ym, wo_ref[m], preferred_element_type=F32)
        proj = part if proj is None else proj + part
    x_mid = jnp.concatenate([xh_ref[...], x_ref[...]], axis=0) + mod[2:3] * proj
    hx = _modulated_norm(x_mid, g2_ref[...], mod[3:4], mod[4:5]).astype(BF16)
    zero_halo = (_iota((HALO + tm, 1), 0) < HALO) & ((i % per_seq) == 0)

    def conv_half(cols):
        u = jnp.dot(hx, wup_ref[:, cols], preferred_element_type=F32)
        u = jnp.where(zero_halo, 0.0, u)
        cw = cw_ref[:, cols]
        y = (cw[0:1] * pltpu.roll(u, 2, axis=0) + cw[1:2] * pltpu.roll(u, 1, axis=0) + cw[2:3] * u)
        return y[HALO:] + cb_ref[:, cols]

    for j in range(dff // FFN_COLS):
        a = conv_half(slice(j * FFN_COLS, (j + 1) * FFN_COLS))
        g = conv_half(slice(dff + j * FFN_COLS, dff + (j + 1) * FFN_COLS))
        act_ref[:, j * FFN_COLS:(j + 1) * FFN_COLS] = (a * (g * _sigmoid(g))).astype(BF16)

    y = jnp.dot(act_ref[...], wdn_ref[...], preferred_element_type=F32)
    x = x_mid[HALO:] + mod[5:6] * y
    o_ref[...] = x
    modn = modn_ref[...]
    hn_ref[...] = _modulated_norm(x, gn_ref[...], modn[0:1], modn[1:2]).astype(BF16)


def _ffn(ys, x2, w_out4, g2, wup, cw, cb, wdn, mod_l, mod_next, g_next, *, seq, tm):
    T, D = x2.shape
    dff = wdn.shape[0]
    per_seq = seq // tm
    halo_blocks = tm // HALO
    once = pl.Buffered(1)
    row = lambda i: (i, 0)
    halo = lambda i: (jnp.maximum(i * halo_blocks - 1, 0), 0)
    return pl.pallas_call(
        functools.partial(_ffn_kernel, per_seq=per_seq),
        grid=(T // tm,),
        in_specs=[pl.BlockSpec((HALO, GW), halo)] * N_MIXERS + [pl.BlockSpec((tm, GW), row)] * N_MIXERS + [
            pl.BlockSpec((HALO, D), halo),
            pl.BlockSpec((tm, D), row),
            pl.BlockSpec((N_MIXERS, GW, D), lambda i: (0, 0, 0), pipeline_mode=once),
            pl.BlockSpec((1, D), lambda i: (0, 0)),
            pl.BlockSpec((D, 2 * dff), lambda i: (0, 0), pipeline_mode=once),
            pl.BlockSpec((3, 2 * dff), lambda i: (0, 0), pipeline_mode=once),
            pl.BlockSpec((1, 2 * dff), lambda i: (0, 0), pipeline_mode=once),
            pl.BlockSpec((dff, D), lambda i: (0, 0), pipeline_mode=once),
            pl.BlockSpec((None, 6, D), lambda i: (i // per_seq, 0, 0)),
            pl.BlockSpec((None, 6, D), lambda i: (i // per_seq, 0, 0)),
            pl.BlockSpec((1, D), lambda i: (0, 0)),
        ],
        out_specs=[pl.BlockSpec((tm, D), row), pl.BlockSpec((tm, D), row)],
        out_shape=[jax.ShapeDtypeStruct((T, D), F32), jax.ShapeDtypeStruct((T, D), BF16)],
        scratch_shapes=[pltpu.VMEM((tm, dff), BF16)],
        compiler_params=_params(("arbitrary",)),
        name="conv_glu_ffn",
    )(*ys, *ys, x2, x2, w_out4, g2.reshape(1, D), wup, cw, cb, wdn, mod_l, mod_next, g_next.reshape(1, D))


def _rope_tables(seq):
    inv = 1.0 / (ROPE_THETA ** (jnp.arange(0, DQ, 2, dtype=F32) / DQ))
    ang = jnp.arange(seq, dtype=F32)[:, None] * inv[None, :]
    cos = jnp.cos(ang)
    sin = jnp.sin(ang)
    cos_map = jnp.concatenate([cos, cos], axis=-1)
    sin_map = jnp.concatenate([-sin, sin], axis=-1)
    reps = GW // DQ
    return jnp.tile(cos_map, (1, reps)), jnp.tile(sin_map, (1, reps))


def kernel(x, c, ada_w, ada_b, norm1_g, norm2_g, w_in, w_out, rw_mu, rw_w0, rw_w_up, rw_a0, rw_a_up, rw_g_up, rw_k_k, rw_k_a, rw_r_k, rw_ln_g, rw_ln_b, df_lam_q1, df_lam_k1, df_lam_q2, df_lam_k2, df_q_g, df_k_g, df_sub_g, sg_w, sg_b, sg_ln_g, sg_ln_b, fx_q_g, fx_k_g, fx_f_b, ffn_up, ffn_conv, ffn_conv_b, ffn_down):
    Bn, S, D = x.shape
    L = ada_w.shape[0]
    T = Bn * S
    dff = ffn_down.shape[1]
    tm = min(ROW_TILE, S)
    tile = min(ATT_TILE, S)

    mod = _ada_mod(c, ada_w, ada_b).reshape(L, Bn, 6, D)
    cos, sin = _rope_tables(S)
    cost, sint = cos.T, sin.T
    x2 = x.reshape(T, D)

    oa = 0
    ob = oa + RW_COLS
    oc = ob + 3 * GW
    od = oc + 2 * GW

    for l in range(L):
        lambda_init = 0.8 - 0.6 * math.exp(-0.3 * l)
        wl = w_in[l].astype(BF16)
        if l == 0:
            h = _norm_mod(x2, mod[l], norm1_g[l], seq=S, row0=0, tm=tm)

        lora_w = jnp.zeros((LANES, 3 * GW), F32)
        lora_w = lora_w.at[0:RW_DECAY_RANK, 0:GW].set(rw_w_up[l])
        lora_w = lora_w.at[RW_DECAY_RANK:RW_DECAY_RANK + RW_A_RANK, GW:2 * GW].set(rw_a_up[l])
        lora_w = lora_w.at[RW_DECAY_RANK + RW_A_RANK:LANES, 2 * GW:3 * GW].set(rw_g_up[l])
        vec = jnp.stack([rw_w0[l], rw_a0[l], rw_k_k[l], rw_k_a[l], rw_r_k[l].reshape(GW),
                         rw_ln_g[l], rw_ln_b[l], jnp.zeros((GW,), F32)])
        ya = _rwkv_mixer(h, wl[:, oa:ob], rw_mu[l].reshape(1, RW_COLS), lora_w, vec,
                         batch=Bn, seq=S, tt=tm, mode=RW_MODE)

        gk = jnp.tile(df_k_g[l], GW // DQ).reshape(1, GW)
        gqt = jnp.broadcast_to((jnp.tile(df_q_g[l], GW // DQ) * (DQ ** -0.5 * LOG2E))[:, None], (GW, tm))
        qb, kb, vb = _diff_prep(h, wl[:, ob + GW:ob + 2 * GW], wl[:, ob:ob + GW].T, wl[:, ob + 2 * GW:oc].T,
                                gk, gqt, cos, sin, cost, sint, seq=S, tm=tm, tile=tile)
        lam_vecs = jnp.stack([df_lam_q1[l], df_lam_k1[l], df_lam_q2[l], df_lam_k2[l]])
        sub_gain = jnp.broadcast_to((jnp.tile(df_sub_g[l], HG) * (1.0 - lambda_init))[:, None], (GW, tile))
        yb = _attention(qb, kb, vb, lam_vecs, sub_gain, batch=Bn, seq=S, tile=tile,
                        n_maps=2, lambda_init=lambda_init)

        sg_bias = jnp.repeat(sg_b[l].T, HEAD_DIM, axis=1)
        yc = _gmlp_mixer(h, wl[:, oc:od], sg_w[l], sg_bias,
                         jnp.stack([sg_ln_g[l], sg_ln_b[l]]), tm=tm)

        wf = jnp.zeros((D, LANES), BF16).at[:, 0:HG].set(wl[:, od + 3 * GW:od + 3 * GW + HG])
        fb = jnp.zeros((1, LANES), F32).at[0, 0:HG].set(fx_f_b[l])
        gkd = jnp.tile(fx_k_g[l], HG).reshape(1, GW)
        gqd = jnp.broadcast_to((jnp.tile(fx_q_g[l], HG) * (HEAD_DIM ** -0.5 * LOG2E))[:, None], (GW, tm))
        qd, kd, vd = _fox_prep(h, wl[:, od + GW:od + 2 * GW], wf, wl[:, od:od + GW].T,
                               wl[:, od + 2 * GW:od + 3 * GW].T, gkd, gqd, fb,
                               batch=Bn, seq=S, tm=tm, tile=tile)
        yd = _attention(qd, kd, vd, jnp.zeros((8, LANES), F32), jnp.zeros((8, LANES), F32),
                        batch=Bn, seq=S, tile=tile, n_maps=1, lambda_init=0.0)

        nl = (l + 1) % L
        x2, h = _ffn((ya, yb, yc, yd), x2, w_out[l].astype(BF16).reshape(N_MIXERS, GW, D), norm2_g[l],
                     ffn_up[l].astype(BF16), ffn_conv[l], ffn_conv_b[l].reshape(1, 2 * dff),
                     ffn_down[l].astype(BF16), mod[l], mod[nl], norm1_g[nl], seq=S, tm=tm)

    return x2.reshape(Bn, S, D)
```

```python
import functools
import math

import jax
import jax.numpy as jnp
from jax import lax
from jax.experimental import pallas as pl
from jax.experimental.pallas import tpu as pltpu

F32 = jnp.float32
BF16 = jnp.bfloat16
HIGHEST = lax.Precision.HIGHEST

N_MIXERS = 4
HEAD_DIM = 64
HG = 4
GW = HG * HEAD_DIM
DQ = HEAD_DIM // 2
RW_DECAY_RANK = 32
RW_A_RANK = 32
RW_GATE_RANK = 64
RW_COLS = 3 * GW + RW_DECAY_RANK + RW_A_RANK + RW_GATE_RANK
RW_GN_EPS = 64e-5
SG_CHUNK = 128
ATT_CHUNK = 64
ROPE_THETA = 10000.0
EPS = 1e-6
LOG2E = math.log2(math.e)
NEG_INF = -1e30

LANES = 128
V7X_VMEM_BYTES = 64 * 1024 * 1024
VMEM_LIMIT = 56 * 1024 * 1024

ROW_TILE = 512
RW_CHUNK = 64
ATT_TILE = 256
SLAB = 128
VROWS = HEAD_DIM + 16
RW_MODE = "lo"

NN = (((1,), (0,)), ((), ()))
NT = (((1,), (1,)), ((), ()))
TN = (((0,), (0,)), ((), ()))


def _params(sem):
    return pltpu.CompilerParams(dimension_semantics=sem, vmem_limit_bytes=VMEM_LIMIT)


def _mm(a, b, dims=NN, mode="hi"):
    if mode == "hi":
        return lax.dot_general(a.astype(F32), b.astype(F32), dims, precision=HIGHEST,
                               preferred_element_type=F32)
    if mode == "lo":
        return lax.dot_general(a.astype(BF16), b.astype(BF16), dims, preferred_element_type=F32)
    ah = a.astype(BF16)
    al = (a - ah.astype(F32)).astype(BF16)
    bh = b.astype(BF16)
    bl = (b - bh.astype(F32)).astype(BF16)
    dg = functools.partial(lax.dot_general, dimension_numbers=dims, preferred_element_type=F32)
    return dg(ah, bh) + (dg(ah, bl) + dg(al, bh))


def _mm_exact(a, b, dims=NN, left=True):
    x = a if left else b
    xh = x.astype(BF16)
    xl = (x - xh.astype(F32)).astype(BF16)
    dg = functools.partial(lax.dot_general, dimension_numbers=dims, preferred_element_type=F32)
    if left:
        bb = b.astype(BF16)
        return dg(xh, bb) + dg(xl, bb)
    ab = a.astype(BF16)
    return dg(ab, xh) + dg(ab, xl)


def _iota(shape, dim):
    return lax.broadcasted_iota(jnp.int32, shape, dim)


def _group_matrix(n, group, value=1.0):
    same = (_iota((n, n), 0) // group) == (_iota((n, n), 1) // group)
    return jnp.where(same, value, 0.0).astype(F32)


def _softplus(x):
    return jnp.maximum(x, 0.0) + jnp.log1p(jnp.exp(-jnp.abs(x)))


def _sigmoid(x):
    return 1.0 / (1.0 + jnp.exp(-x))


def _ada_kernel(c_ref, w_ref, b_ref, o_ref):
    c = c_ref[...]
    cond = c * _sigmoid(c)
    o_ref[...] = _mm(cond, w_ref[...], NN, "hi") + b_ref[...]


def _ada_mod(c, ada_w, ada_b):
    L, D, D6 = ada_w.shape
    Bn = c.shape[0]
    tn = D6 // 4
    return pl.pallas_call(
        _ada_kernel,
        grid=(L, D6 // tn),
        in_specs=[
            pl.BlockSpec((Bn, D), lambda l, j: (0, 0)),
            pl.BlockSpec((None, D, tn), lambda l, j: (l, 0, j)),
            pl.BlockSpec((None, 1, tn), lambda l, j: (l, 0, j)),
        ],
        out_specs=pl.BlockSpec((None, Bn, tn), lambda l, j: (l, 0, j)),
        out_shape=jax.ShapeDtypeStruct((L, Bn, D6), F32),
        compiler_params=_params(("arbitrary", "arbitrary")),
        name="ada_mod",
    )(c, ada_w, ada_b.reshape(L, 1, D6))


def _modulated_norm(x, g, shift, scale):
    y = x * lax.rsqrt(jnp.mean(x * x, axis=-1, keepdims=True) + EPS) * g
    return y * (1.0 + scale) + shift


def _norm_kernel(x_ref, mod_ref, g_ref, o_ref, *, row0):
    mod = mod_ref[...]
    h = _modulated_norm(x_ref[...], g_ref[...], mod[row0:row0 + 1], mod[row0 + 1:row0 + 2])
    o_ref[...] = h.astype(BF16)


def _norm_mod(x2, mod_l, g, *, seq, row0, tm):
    T, D = x2.shape
    per_seq = seq // tm
    return pl.pallas_call(
        functools.partial(_norm_kernel, row0=row0),
        grid=(T // tm,),
        in_specs=[
            pl.BlockSpec((tm, D), lambda i: (i, 0)),
            pl.BlockSpec((None, 6, D), lambda i: (i // per_seq, 0, 0)),
            pl.BlockSpec((1, D), lambda i: (0, 0)),
        ],
        out_specs=pl.BlockSpec((tm, D), lambda i: (i, 0)),
        out_shape=jax.ShapeDtypeStruct((T, D), BF16),
        compiler_params=_params(("arbitrary",)),
        name="norm_mod",
    )(x2, mod_l, g.reshape(1, D))


def _blockdiag(mp, head_lane_masks):
    return jnp.concatenate([jnp.where(m, mp, 0.0) for m in head_lane_masks], axis=0)


def _apply_packed(mp, rhs, head_lane_masks, C, mode):
    full = _mm(mp, rhs, TN, mode)
    out = jnp.where(head_lane_masks[0], full[0:C], 0.0)
    for h in range(1, HG):
        out = out + jnp.where(head_lane_masks[h], full[h * C:(h + 1) * C], 0.0)
    return out


def _rwkv_kernel(h_ref, wa_ref, mu_ref, lora_ref, vec_ref, o_ref,
                 carry_ref, st_ref, hs_ref, *, C, mode):
    TT = h_ref.shape[0]

    @pl.when(pl.program_id(1) == 0)
    def _():
        carry_ref[...] = jnp.zeros_like(carry_ref)
        st_ref[...] = jnp.zeros_like(st_ref)

    vec = vec_ref[...]
    w0, a0, k_k, k_a, r_k, ln_g, ln_b = (vec[i:i + 1] for i in range(7))

    pa = jnp.dot(h_ref[...], wa_ref[...], preferred_element_type=F32)
    prev = pltpu.roll(pa, 1, axis=0)
    prev = jnp.where(_iota((TT, 1), 0) == 0, carry_ref[...], prev)
    carry_ref[...] = pa[TT - 1:TT]
    pa = pa + (prev - pa) * mu_ref[...]

    r = pa[:, 0:GW]
    k = pa[:, GW:2 * GW]
    v = pa[:, 2 * GW:3 * GW]
    lo = pa[:, 3 * GW:3 * GW + LANES]
    lane = _iota((1, LANES), 1)
    act = jnp.where(lane < RW_DECAY_RANK, jnp.tanh(lo),
                    jnp.where(lane < RW_DECAY_RANK + RW_A_RANK, lo, _sigmoid(lo)))
    lora = _mm_exact(act, lora_ref[...])
    w_log = -_softplus(-(w0 + lora[:, 0:GW])) - 0.5
    lw = -jnp.exp(w_log)
    a = _sigmoid(a0 + lora[:, GW:2 * GW])
    gate = lora[:, 2 * GW:3 * GW]

    gsum = _group_matrix(GW, HEAD_DIM)
    kk = k * k_k
    kk = kk / jnp.maximum(jnp.sqrt(_mm(kk * kk, gsum, NN, "lo")), 1e-12)
    k2 = k * (1.0 + (a - 1.0) * k_a)
    bonus = _mm(r * k2 * r_k, gsum, NN, "lo") * v

    aa = -kk
    b = kk * a

    lane_head = _iota((1, GW), 1) // HEAD_DIM
    hmask = [lane_head == h for h in range(HG)]
    s_idx = _iota((C, GW), 0)
    t_idx = _iota((C, GW), 1) % C
    strict = s_idx < t_idx
    incl = s_idx <= t_idx
    eye_p = jnp.where(s_idx == t_idx, 1.0, 0.0).astype(F32)
    l_incl = jnp.where(_iota((C, C), 0) >= _iota((C, C), 1), 1.0, 0.0).astype(F32)
    bd = (_iota((GW, GW), 0) // HEAD_DIM) == (_iota((GW, GW), 1) // HEAD_DIM)
    eye_k = jnp.where(_iota((GW, GW), 0) == _iota((GW, GW), 1), 1.0, 0.0).astype(F32)
    chunks = range(TT // C)
    sl = lambda x, c: x[c * C:(c + 1) * C]

    cl = jnp.concatenate([_mm_exact(l_incl, sl(lw, c), left=False) for c in chunks], axis=0)
    g_in = jnp.exp(cl)
    g_inv = jnp.exp(-cl)
    at = aa * jnp.exp(cl - lw)
    rt = r * g_in
    bt = b * g_inv
    kt = k2 * g_inv

    bks = [jnp.concatenate([sl(bt, c), sl(kt, c)], axis=0) for c in chunks]
    xs = [_mm(bks[c], jnp.concatenate([_blockdiag(sl(at, c), hmask), _blockdiag(sl(rt, c), hmask)], axis=0),
              NT, mode) for c in chunks]
    n_p = [jnp.where(strict, x[0:C, 0:GW], 0.0) for x in xs]
    ak_p = [jnp.where(strict, x[C:2 * C, 0:GW], 0.0) for x in xs]
    rbk_p = [jnp.concatenate([jnp.where(incl, x[0:C, GW:2 * GW], 0.0),
                              jnp.where(incl, x[C:2 * C, GW:2 * GW], 0.0)], axis=0) for x in xs]

    p_p = [eye_p + n for n in n_p]
    q_p = n_p
    for _ in range(int(math.log2(C)) - 1):
        q_p = [_mm(q, _blockdiag(q, hmask), NN, mode) for q in q_p]
        p_p = [p + _mm(p, _blockdiag(q, hmask), NN, mode) for p, q in zip(p_p, q_p)]

    ta = [_apply_packed(p_p[c], sl(at, c), hmask, C, mode) for c in chunks]
    wv = [_apply_packed(ak_p[c], sl(v, c), hmask, C, mode) for c in chunks]
    tw = [_apply_packed(p_p[c], wv[c], hmask, C, mode) for c in chunks]
    g_end = [g_in[(c + 1) * C - 1:(c + 1) * C] for c in chunks]
    m_t = [(eye_k + jnp.where(bd, _mm(ta[c], sl(bt, c), TN, mode), 0.0)) * g_end[c] for c in chunks]
    g_t = [jnp.where(bd, _mm(jnp.concatenate([tw[c], sl(v, c)], axis=0), bks[c], TN, mode), 0.0) * g_end[c]
           for c in chunks]

    ht = st_ref[...]
    for c in chunks:
        hs_ref[c] = ht
        ht = _mm(ht, m_t[c], NN, mode) + g_t[c]
    st_ref[...] = ht

    ys = []
    for c in chunks:
        sh = _mm(jnp.concatenate([ta[c], sl(rt, c)], axis=0), hs_ref[c], NT, mode)
        uv = jnp.concatenate([sh[0:C] + tw[c], sl(v, c)], axis=0)
        ys.append(sh[C:2 * C] + _apply_packed(rbk_p[c], uv, hmask, C, mode))
    y = jnp.concatenate(ys, axis=0)

    gmean = _group_matrix(GW, HEAD_DIM, 1.0 / HEAD_DIM)
    yc = y - _mm(y, gmean, NN, "lo")
    yn = yc * lax.rsqrt(_mm(yc * yc, gmean, NN, "lo") + RW_GN_EPS) * ln_g + ln_b
    o_ref[...] = ((yn + bonus) * gate).astype(BF16)


def _rwkv_mixer(h, wa, mu, lora_w, vec, *, batch, seq, tt, mode):
    T, D = h.shape
    per_seq = seq // tt
    C = RW_CHUNK
    return pl.pallas_call(
        functools.partial(_rwkv_kernel, C=C, mode=mode),
        grid=(batch, per_seq),
        in_specs=[
            pl.BlockSpec((tt, D), lambda b, i: (b * per_seq + i, 0)),
            pl.BlockSpec((D, RW_COLS), lambda b, i: (0, 0)),
            pl.BlockSpec((1, RW_COLS), lambda b, i: (0, 0)),
            pl.BlockSpec((LANES, 3 * GW), lambda b, i: (0, 0)),
            pl.BlockSpec((8, GW), lambda b, i: (0, 0)),
        ],
        out_specs=pl.BlockSpec((tt, GW), lambda b, i: (b * per_seq + i, 0)),
        out_shape=jax.ShapeDtypeStruct((T, GW), BF16),
        scratch_shapes=[pltpu.VMEM((1, RW_COLS), F32), pltpu.VMEM((GW, GW), F32),
                        pltpu.VMEM((tt // C, GW, GW), F32)],
        compiler_params=_params(("arbitrary", "arbitrary")),
        name="rwkv7_mixer",
    )(h, wa, mu, lora_w, vec)


def _place_heads_lanes(n_in, width):
    src = _iota((n_in, HG * SLAB), 0)
    dst = _iota((n_in, HG * SLAB), 1)
    ok = ((dst // SLAB) == (src // width)) & ((dst % SLAB) == (src % width)) & ((dst % SLAB) < width)
    return jnp.where(ok, 1.0, 0.0).astype(BF16)


def _slab_rows(xt, extra):
    parts = []
    for h in range(HG):
        parts.append(xt[h * HEAD_DIM:(h + 1) * HEAD_DIM])
        parts.append(extra)
    return jnp.concatenate(parts, axis=0)


def _diff_prep_kernel(h_ref, wk_ref, wqt_ref, wvt_ref, gk_ref, gqt_ref,
                      cos_ref, sin_ref, cost_ref, sint_ref, q_out, k_out, v_out, *, tile):
    tm = h_ref.shape[0]
    hb = h_ref[...]
    k = jnp.dot(hb, wk_ref[...], preferred_element_type=F32)
    qt = lax.dot_general(wqt_ref[...], hb, NT, preferred_element_type=F32)
    vt = lax.dot_general(wvt_ref[...], hb, NT, preferred_element_type=F32)
    gmean = _group_matrix(GW, DQ, 1.0 / DQ)

    kn = k * lax.rsqrt(_mm(k * k, gmean, NN, "lo") + EPS) * gk_ref[...]
    first_half = (_iota((1, GW), 1) % DQ) < (DQ // 2)
    partner = jnp.where(first_half, pltpu.roll(kn, GW - DQ // 2, axis=1), pltpu.roll(kn, DQ // 2, axis=1))
    kr = kn * cos_ref[...] + partner * sin_ref[...]
    k_out[...] = jnp.dot(kr.astype(BF16), _place_heads_lanes(GW, HEAD_DIM),
                         preferred_element_type=F32).astype(BF16)

    qn = qt * lax.rsqrt(_mm(gmean, qt * qt, NN, "lo") + EPS) * gqt_ref[...]
    first_half_t = (_iota((GW, 1), 0) % DQ) < (DQ // 2)
    partner_t = jnp.where(first_half_t, pltpu.roll(qn, GW - DQ // 2, axis=0), pltpu.roll(qn, DQ // 2, axis=0))
    qr = (qn * cost_ref[...] + partner_t * sint_ref[...]).astype(BF16)
    qs = _slab_rows(qr, jnp.zeros((SLAB - HEAD_DIM, tm), BF16))
    vb = _slab_rows(vt.astype(BF16),
                    jnp.where(_iota((VROWS - HEAD_DIM, tm), 0) == 0, 1.0, 0.0).astype(BF16))
    for c in range(tm // tile):
        q_out[c] = qs[:, c * tile:(c + 1) * tile]
        v_out[c] = vb[:, c * tile:(c + 1) * tile]


def _diff_prep(h, wk, wqt, wvt, gk, gqt, cos, sin, cost, sint, *, seq, tm, tile):
    T, D = h.shape
    per_seq = seq // tm
    nt = tm // tile
    const = lambda i: (0, 0)
    return pl.pallas_call(
        functools.partial(_diff_prep_kernel, tile=tile),
        grid=(T // tm,),
        in_specs=[
            pl.BlockSpec((tm, D), lambda i: (i, 0)),
            pl.BlockSpec((D, GW), const),
            pl.BlockSpec((GW, D), const),
            pl.BlockSpec((GW, D), const),
            pl.BlockSpec((1, GW), const),
            pl.BlockSpec((GW, tm), const),
            pl.BlockSpec((tm, GW), lambda i: (i % per_seq, 0)),
            pl.BlockSpec((tm, GW), lambda i: (i % per_seq, 0)),
            pl.BlockSpec((GW, tm), lambda i: (0, i % per_seq)),
            pl.BlockSpec((GW, tm), lambda i: (0, i % per_seq)),
        ],
        out_specs=[
            pl.BlockSpec((nt, HG * SLAB, tile), lambda i: (i, 0, 0)),
            pl.BlockSpec((tm, HG * SLAB), lambda i: (i, 0)),
            pl.BlockSpec((nt, HG * VROWS, tile), lambda i: (i, 0, 0)),
        ],
        out_shape=[
            jax.ShapeDtypeStruct((T // tile, HG * SLAB, tile), BF16),
            jax.ShapeDtypeStruct((T, HG * SLAB), BF16),
            jax.ShapeDtypeStruct((T // tile, HG * VROWS, tile), BF16),
        ],
        compiler_params=_params(("arbitrary",)),
        name="diff_prep",
    )(h, wk, wqt, wvt, gk, gqt, cos, sin, cost, sint)


def _fox_prep_kernel(h_ref, wk_ref, wf_ref, wqt_ref, wvt_ref, gk_ref, gqt_ref, fb_ref,
                     q_out, k_out, v_out, fcarry_ref, *, tile):
    tm = h_ref.shape[0]

    @pl.when(pl.program_id(1) == 0)
    def _():
        fcarry_ref[...] = jnp.zeros_like(fcarry_ref)

    hb = h_ref[...]
    k = jnp.dot(hb, wk_ref[...], preferred_element_type=F32)
    fl = jnp.dot(hb, wf_ref[...], preferred_element_type=F32)
    qt = lax.dot_general(wqt_ref[...], hb, NT, preferred_element_type=F32)
    vt = lax.dot_general(wvt_ref[...], hb, NT, preferred_element_type=F32)
    gmean = _group_matrix(GW, HEAD_DIM, 1.0 / HEAD_DIM)

    kn = (k * lax.rsqrt(_mm(k * k, gmean, NN, "lo") + EPS) * gk_ref[...]).astype(BF16)

    log_f = -_softplus(-(fl + fb_ref[...]))
    cb = min(tm, LANES)
    l_incl = jnp.where(_iota((cb, cb), 0) >= _iota((cb, cb), 1), 1.0, 0.0).astype(F32)
    carry = fcarry_ref[...]
    blocks = []
    for r in range(tm // cb):
        blocks.append(_mm_exact(l_incl, log_f[r * cb:(r + 1) * cb], left=False) + carry)
        carry = blocks[-1][cb - 1:cb]
    fcarry_ref[...] = carry
    nf = jnp.concatenate(blocks, axis=0) * (-LOG2E)
    head_lane = _iota((1, LANES), 1) < HG
    packed = jnp.zeros_like(nf)
    for i in range(3):
        p = jnp.where(head_lane, nf.astype(BF16).astype(F32), 0.0)
        packed = packed + (p if i == 0 else pltpu.roll(p, HG * i, axis=1))
        nf = nf - p
    src = _iota((LANES, HG * SLAB), 0)
    dst = _iota((LANES, HG * SLAB), 1)
    place = jnp.where((src < 3 * HG) & (dst == (src % HG) * SLAB + HEAD_DIM + src // HG), 1.0, 0.0)
    ks = (jnp.dot(kn, _place_heads_lanes(GW, HEAD_DIM), preferred_element_type=F32)
          + jnp.dot(packed.astype(BF16), place.astype(BF16), preferred_element_type=F32))
    k_out[...] = ks.astype(BF16)

    qn = (qt * lax.rsqrt(_mm(gmean, qt * qt, NN, "lo") + EPS) * gqt_ref[...]).astype(BF16)
    ones_rows = jnp.where(_iota((SLAB - HEAD_DIM, tm), 0) < 3, 1.0, 0.0).astype(BF16)
    qs = _slab_rows(qn, ones_rows)
    vb = _slab_rows(vt.astype(BF16),
                    jnp.where(_iota((VROWS - HEAD_DIM, tm), 0) == 0, 1.0, 0.0).astype(BF16))
    for c in range(tm // tile):
        q_out[c] = qs[:, c * tile:(c + 1) * tile]
        v_out[c] = vb[:, c * tile:(c + 1) * tile]


def _fox_prep(h, wk, wf, wqt, wvt, gk, gqt, fb, *, batch, seq, tm, tile):
    T, D = h.shape
    per_seq = seq // tm
    nt = tm // tile
    const = lambda b, i: (0, 0)
    row = lambda b, i: (b * per_seq + i, 0)
    return pl.pallas_call(
        functools.partial(_fox_prep_kernel, tile=tile),
        grid=(batch, per_seq),
        in_specs=[
            pl.BlockSpec((tm, D), row),
            pl.BlockSpec((D, GW), const),
            pl.BlockSpec((D, LANES), const),
            pl.BlockSpec((GW, D), const),
            pl.BlockSpec((GW, D), const),
            pl.BlockSpec((1, GW), const),
            pl.BlockSpec((GW, tm), const),
            pl.BlockSpec((1, LANES), const),
        ],
        out_specs=[
            pl.BlockSpec((nt, HG * SLAB, tile), lambda b, i: (b * per_seq + i, 0, 0)),
            pl.BlockSpec((tm, HG * SLAB), row),
            pl.BlockSpec((nt, HG * VROWS, tile), lambda b, i: (b * per_seq + i, 0, 0)),
        ],
        out_shape=[
            jax.ShapeDtypeStruct((T // tile, HG * SLAB, tile), BF16),
            jax.ShapeDtypeStruct((T, HG * SLAB), BF16),
            jax.ShapeDtypeStruct((T // tile, HG * VROWS, tile), BF16),
        ],
        scratch_shapes=[pltpu.VMEM((1, LANES), F32)],
        compiler_params=_params(("arbitrary", "arbitrary")),
        name="fox_prep",
    )(h, wk, wf, wqt, wvt, gk, gqt, fb)


def _attn_kernel(q_ref, k_ref, v_ref, aux_ref, gain_ref, o_ref, qz_s, m_s, acc_s,
                 s_a, s_b, p_a, p_b, al_a, al_b, *, n_maps, tile, lambda_init):
    qi = pl.program_id(1)
    row = _iota((tile, tile), 0)
    col = _iota((tile, tile), 1)
    if n_maps == 2:
        diag_ok = (row // ATT_CHUNK) <= (col // ATT_CHUNK)
    else:
        diag_ok = row <= col

    slab_row = _iota((SLAB, 1), 0)
    for h in range(HG):
        qh = q_ref[h * SLAB:(h + 1) * SLAB, :]
        if n_maps == 2:
            qz_s[h, :, 0:tile] = jnp.where(slab_row < DQ, qh, jnp.zeros_like(qh))
            qz_s[h, :, tile:2 * tile] = jnp.where((slab_row >= DQ) & (slab_row < 2 * DQ), qh,
                                                  jnp.zeros_like(qh))
        else:
            qz_s[h] = qh
    m_s[...] = jnp.full(m_s.shape, NEG_INF, F32)
    acc_s[...] = jnp.zeros(acc_s.shape, F32)
    p_b[...] = jnp.zeros(p_b.shape, BF16)
    al_b[...] = jnp.zeros(al_b.shape, F32)

    buf_a = (s_a, p_a, al_a)
    buf_b = (s_b, p_b, al_b)

    all_heads = tuple(range(HG))

    def scores(j, s_buf, heads=all_heads):
        for h in heads:
            kj = k_ref[pl.ds(pl.multiple_of(j * tile, tile), tile), h * SLAB:(h + 1) * SLAB]
            s_buf[h] = jnp.dot(kj, qz_s[h], preferred_element_type=F32)

    def value_update(j, p_buf, al_buf, heads=all_heads):
        for h in heads:
            vj = v_ref[j, h * VROWS:(h + 1) * VROWS, :]
            for mp in range(n_maps):
                c = h * n_maps + mp
                rows = slice(c * VROWS, (c + 1) * VROWS)
                pv = jnp.dot(vj, p_buf[c], preferred_element_type=F32)
                acc_s[rows, :] = al_buf[c:c + 1, :] * acc_s[rows, :] + pv

    def softmax(s_buf, p_buf, al_buf, masked, heads=all_heads):
        for h in heads:
            for mp in range(n_maps):
                c = h * n_maps + mp
                s = s_buf[h, :, mp * tile:(mp + 1) * tile]
                if masked:
                    s = jnp.where(diag_ok, s, NEG_INF)
                m_old = m_s[c:c + 1, :]
                m_new = jnp.maximum(m_old, jnp.max(s, axis=0, keepdims=True))
                m_s[c:c + 1, :] = m_new
                al_buf[c:c + 1, :] = jnp.exp2(m_old - m_new)
                p_buf[c] = jnp.exp2(s - m_new).astype(BF16)

    def step(j, cur, nxt):
        for h in all_heads:
            scores(j + 1, nxt[0], (h,))
            value_update(jnp.maximum(j - 1, 0), nxt[1], nxt[2], (h,))
            softmax(cur[0], cur[1], cur[2], False, (h,))

    def last(j, cur, nxt):
        value_update(jnp.maximum(j - 1, 0), nxt[1], nxt[2])
        softmax(cur[0], cur[1], cur[2], True)
        value_update(j, cur[1], cur[2])

    scores(0, s_a)

    def pair(j):
        step(j, buf_a, buf_b)
        step(j + 1, buf_b, buf_a)

    pairs_per_trip = 2 if n_maps == 1 else 1
    shift = pairs_per_trip.bit_length()

    def trip(jj, carry):
        for t in range(pairs_per_trip):
            pair(2 * pairs_per_trip * jj + 2 * t)
        return carry

    lax.fori_loop(0, lax.shift_right_logical(qi, shift), trip, 0)
    odd = (qi & 1) == 1

    if pairs_per_trip == 2:
        @pl.when((qi & 2) == 2)
        def _():
            pair(qi - (qi & 3))

    @pl.when(odd)
    def _():
        step(qi - 1, buf_a, buf_b)
        last(qi, buf_b, buf_a)

    @pl.when(jnp.logical_not(odd))
    def _():
        last(qi, buf_a, buf_b)

    if n_maps == 2:
        aux = aux_ref[...]
        lam = (jnp.exp(jnp.sum(aux[0:1] * aux[1:2], axis=1, keepdims=True))
               - jnp.exp(jnp.sum(aux[2:3] * aux[3:4], axis=1, keepdims=True)) + lambda_init)
    outs = []
    for h in range(HG):
        c = h * n_maps
        normalized = lambda cc: (acc_s[cc * VROWS:cc * VROWS + HEAD_DIM, :]
                                 / acc_s[cc * VROWS + HEAD_DIM:cc * VROWS + HEAD_DIM + 1, :])
        o = normalized(c)
        if n_maps == 2:
            o = o - lam * normalized(c + 1)
            o = o * lax.rsqrt(jnp.mean(o * o, axis=0, keepdims=True) + EPS)
        outs.append(o)
    ot = jnp.concatenate(outs, axis=0)
    if n_maps == 2:
        ot = ot * gain_ref[...]
    o_ref[...] = ot.T.astype(BF16)


def _attention(q3, k2, v3, aux, gain, *, batch, seq, tile, n_maps, lambda_init):
    nq = seq // tile
    T = batch * seq
    n_chains = HG * n_maps
    scratch = [pltpu.VMEM((HG, SLAB, n_maps * tile), BF16),
               pltpu.VMEM((n_chains, tile), F32),
               pltpu.VMEM((n_chains * VROWS, tile), F32),
               pltpu.VMEM((HG, tile, n_maps * tile), F32),
               pltpu.VMEM((HG, tile, n_maps * tile), F32),
               pltpu.VMEM((n_chains, tile, tile), BF16),
               pltpu.VMEM((n_chains, tile, tile), BF16),
               pltpu.VMEM((n_chains, tile), F32),
               pltpu.VMEM((n_chains, tile), F32)]
    return pl.pallas_call(
        functools.partial(_attn_kernel, n_maps=n_maps, tile=tile, lambda_init=lambda_init),
        grid=(batch, nq),
        in_specs=[
            pl.BlockSpec((None, HG * SLAB, tile), lambda b, i: (b * nq + i, 0, 0)),
            pl.BlockSpec((seq, HG * SLAB), lambda b, i: (b, 0)),
            pl.BlockSpec((nq, HG * VROWS, tile), lambda b, i: (b, 0, 0)),
            pl.BlockSpec(aux.shape, lambda b, i: (0, 0)),
            pl.BlockSpec(gain.shape, lambda b, i: (0, 0)),
        ],
        out_specs=pl.BlockSpec((tile, GW), lambda b, i: (b * nq + i, 0)),
        out_shape=jax.ShapeDtypeStruct((T, GW), BF16),
        scratch_shapes=scratch,
        compiler_params=_params(("arbitrary", "arbitrary")),
        name="diff_attention" if n_maps == 2 else "forgetting_attention",
    )(q3, k2, v3, aux, gain)


def _erf_gelu(x):
    return 0.5 * x * (1.0 + lax.erf(x * (1.0 / math.sqrt(2.0))))


def _gmlp_kernel(h_ref, wc_ref, sgw_ref, bias_ref, ln_ref, o_ref):
    tm = h_ref.shape[0]
    pc = _erf_gelu(jnp.dot(h_ref[...], wc_ref[...], preferred_element_type=F32))
    u = pc[:, 0:GW]
    v = pc[:, GW:2 * GW]
    mu = jnp.mean(v, axis=-1, keepdims=True)
    vc = v - mu
    var = jnp.mean(vc * vc, axis=-1, keepdims=True)
    ln = ln_ref[...]
    vn = (vc * lax.rsqrt(var + EPS) * ln[0:1] + ln[1:2]).astype(BF16)

    causal = _iota((SG_CHUNK, SG_CHUNK), 0) >= _iota((SG_CHUNK, SG_CHUNK), 1)
    w = jnp.concatenate([jnp.where(causal, sgw_ref[g], 0.0) for g in range(HG)], axis=0).astype(BF16)
    lane_head = _iota((1, GW), 1) // HEAD_DIM
    bias = bias_ref[...]
    for n in range(tm // SG_CHUNK):
        rows = slice(n * SG_CHUNK, (n + 1) * SG_CHUNK)
        full = jnp.dot(w, vn[rows], preferred_element_type=F32)
        sv = bias
        for g in range(HG):
            sv = sv + jnp.where(lane_head == g, full[g * SG_CHUNK:(g + 1) * SG_CHUNK], 0.0)
        o_ref[rows, :] = (u[rows] * sv).astype(BF16)


def _gmlp_mixer(h, wc, sg_w, bias, ln, *, tm):
    T, D = h.shape
    return pl.pallas_call(
        _gmlp_kernel,
        grid=(T // tm,),
        in_specs=[
            pl.BlockSpec((tm, D), lambda i: (i, 0)),
            pl.BlockSpec((D, 2 * GW), lambda i: (0, 0)),
            pl.BlockSpec((HG, SG_CHUNK, SG_CHUNK), lambda i: (0, 0, 0)),
            pl.BlockSpec((SG_CHUNK, GW), lambda i: (0, 0)),
            pl.BlockSpec((2, GW), lambda i: (0, 0)),
        ],
        out_specs=pl.BlockSpec((tm, GW), lambda i: (i, 0)),
        out_shape=jax.ShapeDtypeStruct((T, GW), BF16),
        compiler_params=_params(("arbitrary",)),
        name="gmlp_mixer",
    )(h, wc, sg_w, bias, ln)


HALO = 16


FFN_COLS = 256


def _ffn_kernel(yah, ybh, ych, ydh, ya, yb, yc, yd, xh_ref, x_ref, wo_ref, g2_ref,
                wup_ref, cw_ref, cb_ref, wdn_ref, mod_ref, modn_ref, gn_ref,
                o_ref, hn_ref, act_ref, *, per_seq):
    i = pl.program_id(0)
    tm = x_ref.shape[0]
    dff = wdn_ref.shape[0]
    mod = mod_ref[...]

    proj = None
    for m, (halo, main) in enumerate(((yah, ya), (ybh, yb), (ych, yc), (ydh, yd))):
        ym = jnp.concatenate([halo[...], main[...]], axis=0)
        part = jnp.dot(ym, wo_ref[m], preferred_element_type=F32)
        proj = part if proj is None else proj + part
    x_mid = jnp.concatenate([xh_ref[...], x_ref[...]], axis=0) + mod[2:3] * proj
    hx = _modulated_norm(x_mid, g2_ref[...], mod[3:4], mod[4:5]).astype(BF16)
    zero_halo = (_iota((HALO + tm, 1), 0) < HALO) & ((i % per_seq) == 0)

    def conv_half(cols):
        u = jnp.dot(hx, wup_ref[:, cols], preferred_element_type=F32)
        u = jnp.where(zero_halo, 0.0, u)
        cw = cw_ref[:, cols]
        y = (cw[0:1] * pltpu.roll(u, 2, axis=0) + cw[1:2] * pltpu.roll(u, 1, axis=0) + cw[2:3] * u)
        return y[HALO:] + cb_ref[:, cols]

    for j in range(dff // FFN_COLS):
        a = conv_half(slice(j * FFN_COLS, (j + 1) * FFN_COLS))
        g = conv_half(slice(dff + j * FFN_COLS, dff + (j + 1) * FFN_COLS))
        act_ref[:, j * FFN_COLS:(j + 1) * FFN_COLS] = (a * (g * _sigmoid(g))).astype(BF16)

    y = jnp.dot(act_ref[...], wdn_ref[...], preferred_element_type=F32)
    x = x_mid[HALO:] + mod[5:6] * y
    o_ref[...] = x
    modn = modn_ref[...]
    hn_ref[...] = _modulated_norm(x, gn_ref[...], modn[0:1], modn[1:2]).astype(BF16)


def _ffn(ys, x2, w_out4, g2, wup, cw, cb, wdn, mod_l, mod_next, g_next, *, seq, tm):
    T, D = x2.shape
    dff = wdn.shape[0]
    per_seq = seq // tm
    halo_blocks = tm // HALO
    once = pl.Buffered(1)
    row = lambda i: (i, 0)
    halo = lambda i: (jnp.maximum(i * halo_blocks - 1, 0), 0)
    return pl.pallas_call(
        functools.partial(_ffn_kernel, per_seq=per_seq),
        grid=(T // tm,),
        in_specs=[pl.BlockSpec((HALO, GW), halo)] * N_MIXERS + [pl.BlockSpec((tm, GW), row)] * N_MIXERS + [
            pl.BlockSpec((HALO, D), halo),
            pl.BlockSpec((tm, D), row),
            pl.BlockSpec((N_MIXERS, GW, D), lambda i: (0, 0, 0), pipeline_mode=once),
            pl.BlockSpec((1, D), lambda i: (0, 0)),
            pl.BlockSpec((D, 2 * dff), lambda i: (0, 0), pipeline_mode=once),
            pl.BlockSpec((3, 2 * dff), lambda i: (0, 0), pipeline_mode=once),
            pl.BlockSpec((1, 2 * dff), lambda i: (0, 0), pipeline_mode=once),
            pl.BlockSpec((dff, D), lambda i: (0, 0), pipeline_mode=once),
            pl.BlockSpec((None, 6, D), lambda i: (i // per_seq, 0, 0)),
            pl.BlockSpec((None, 6, D), lambda i: (i // per_seq, 0, 0)),
            pl.BlockSpec((1, D), lambda i: (0, 0)),
        ],
        out_specs=[pl.BlockSpec((tm, D), row), pl.BlockSpec((tm, D), row)],
        out_shape=[jax.ShapeDtypeStruct((T, D), F32), jax.ShapeDtypeStruct((T, D), BF16)],
        scratch_shapes=[pltpu.VMEM((tm, dff), BF16)],
        compiler_params=_params(("arbitrary",)),
        name="conv_glu_ffn",
    )(*ys, *ys, x2, x2, w_out4, g2.reshape(1, D), wup, cw, cb, wdn, mod_l, mod_next, g_next.reshape(1, D))


def _rope_tables(seq):
    inv = 1.0 / (ROPE_THETA ** (jnp.arange(0, DQ, 2, dtype=F32) / DQ))
    ang = jnp.arange(seq, dtype=F32)[:, None] * inv[None, :]
    cos = jnp.cos(ang)
    sin = jnp.sin(ang)
    cos_map = jnp.concatenate([cos, cos], axis=-1)
    sin_map = jnp.concatenate([-sin, sin], axis=-1)
    reps = GW // DQ
    return jnp.tile(cos_map, (1, reps)), jnp.tile(sin_map, (1, reps))


def kernel(x, c, ada_w, ada_b, norm1_g, norm2_g, w_in, w_out, rw_mu, rw_w0, rw_w_up, rw_a0, rw_a_up, rw_g_up, rw_k_k, rw_k_a, rw_r_k, rw_ln_g, rw_ln_b, df_lam_q1, df_lam_k1, df_lam_q2, df_lam_k2, df_q_g, df_k_g, df_sub_g, sg_w, sg_b, sg_ln_g, sg_ln_b, fx_q_g, fx_k_g, fx_f_b, ffn_up, ffn_conv, ffn_conv_b, ffn_down):
    Bn, S, D = x.shape
    L = ada_w.shape[0]
    T = Bn * S
    dff = ffn_down.shape[1]
    tm = min(ROW_TILE, S)
    tile = min(ATT_TILE, S)

    mod = _ada_mod(c, ada_w, ada_b).reshape(L, Bn, 6, D)
    cos, sin = _rope_tables(S)
    cost, sint = cos.T, sin.T
    x2 = x.reshape(T, D)

    oa = 0
    ob = oa + RW_COLS
    oc = ob + 3 * GW
    od = oc + 2 * GW

    for l in range(L):
        lambda_init = 0.8 - 0.6 * math.exp(-0.3 * l)
        wl = w_in[l].astype(BF16)
        if l == 0:
            h = _norm_mod(x2, mod[l], norm1_g[l], seq=S, row0=0, tm=tm)

        lora_w = jnp.zeros((LANES, 3 * GW), F32)
        lora_w = lora_w.at[0:RW_DECAY_RANK, 0:GW].set(rw_w_up[l])
        lora_w = lora_w.at[RW_DECAY_RANK:RW_DECAY_RANK + RW_A_RANK, GW:2 * GW].set(rw_a_up[l])
        lora_w = lora_w.at[RW_DECAY_RANK + RW_A_RANK:LANES, 2 * GW:3 * GW].set(rw_g_up[l])
        vec = jnp.stack([rw_w0[l], rw_a0[l], rw_k_k[l], rw_k_a[l], rw_r_k[l].reshape(GW),
                         rw_ln_g[l], rw_ln_b[l], jnp.zeros((GW,), F32)])
        ya = _rwkv_mixer(h, wl[:, oa:ob], rw_mu[l].reshape(1, RW_COLS), lora_w, vec,
                         batch=Bn, seq=S, tt=tm, mode=RW_MODE)

        gk = jnp.tile(df_k_g[l], GW // DQ).reshape(1, GW)
        gqt = jnp.broadcast_to((jnp.tile(df_q_g[l], GW // DQ) * (DQ ** -0.5 * LOG2E))[:, None], (GW, tm))
        qb, kb, vb = _diff_prep(h, wl[:, ob + GW:ob + 2 * GW], wl[:, ob:ob + GW].T, wl[:, ob + 2 * GW:oc].T,
                                gk, gqt, cos, sin, cost, sint, seq=S, tm=tm, tile=tile)
        lam_vecs = jnp.stack([df_lam_q1[l], df_lam_k1[l], df_lam_q2[l], df_lam_k2[l]])
        sub_gain = jnp.broadcast_to((jnp.tile(df_sub_g[l], HG) * (1.0 - lambda_init))[:, None], (GW, tile))
        yb = _attention(qb, kb, vb, lam_vecs, sub_gain, batch=Bn, seq=S, tile=tile,
                        n_maps=2, lambda_init=lambda_init)

        sg_bias = jnp.repeat(sg_b[l].T, HEAD_DIM, axis=1)
        yc = _gmlp_mixer(h, wl[:, oc:od], sg_w[l], sg_bias,
                         jnp.stack([sg_ln_g[l], sg_ln_b[l]]), tm=tm)

        wf = jnp.zeros((D, LANES), BF16).at[:, 0:HG].set(wl[:, od + 3 * GW:od + 3 * GW + HG])
        fb = jnp.zeros((1, LANES), F32).at[0, 0:HG].set(fx_f_b[l])
        gkd = jnp.tile(fx_k_g[l], HG).reshape(1, GW)
        gqd = jnp.broadcast_to((jnp.tile(fx_q_g[l], HG) * (HEAD_DIM ** -0.5 * LOG2E))[:, None], (GW, tm))
        qd, kd, vd = _fox_prep(h, wl[:, od + GW:od + 2 * GW], wf, wl[:, od:od + GW].T,
                               wl[:, od + 2 * GW:od + 3 * GW].T, gkd, gqd, fb,
                               batch=Bn, seq=S, tm=tm, tile=tile)
        yd = _attention(qd, kd, vd, jnp.zeros((8, LANES), F32), jnp.zeros((8, LANES), F32),
                        batch=Bn, seq=S, tile=tile, n_maps=1, lambda_init=0.0)

        nl = (l + 1) % L
        x2, h = _ffn((ya, yb, yc, yd), x2, w_out[l].astype(BF16).reshape(N_MIXERS, GW, D), norm2_g[l],
                     ffn_up[l].astype(BF16), ffn_conv[l], ffn_conv_b[l].reshape(1, 2 * dff),
                     ffn_down[l].astype(BF16), mod[l], mod[nl], norm1_g[nl], seq=S, tm=tm)

    return x2.reshape(Bn, S, D)
```

```python
import functools
import math

import jax
import jax.numpy as jnp
from jax import lax
from jax.experimental import pallas as pl
from jax.experimental.pallas import tpu as pltpu

F32 = jnp.float32
BF16 = jnp.bfloat16
HIGHEST = lax.Precision.HIGHEST

N_MIXERS = 4
HEAD_DIM = 64
HG = 4
GW = HG * HEAD_DIM
DQ = HEAD_DIM // 2
RW_DECAY_RANK = 32
RW_A_RANK = 32
RW_GATE_RANK = 64
RW_COLS = 3 * GW + RW_DECAY_RANK + RW_A_RANK + RW_GATE_RANK
RW_GN_EPS = 64e-5
SG_CHUNK = 128
ATT_CHUNK = 64
ROPE_THETA = 10000.0
EPS = 1e-6
LOG2E = math.log2(math.e)
NEG_INF = -1e30

LANES = 128
V7X_VMEM_BYTES = 64 * 1024 * 1024
VMEM_LIMIT = 56 * 1024 * 1024

ROW_TILE = 512
RW_CHUNK = 64
RW_TILE = 512
ATT_TILE = 256
SLAB = 128
VROWS = HEAD_DIM + 16
RW_MODE = "lo"

NN = (((1,), (0,)), ((), ()))
NT = (((1,), (1,)), ((), ()))
TN = (((0,), (0,)), ((), ()))


def _params(sem):
    return pltpu.CompilerParams(dimension_semantics=sem, vmem_limit_bytes=VMEM_LIMIT)


def _mm(a, b, dims=NN, mode="hi"):
    if mode == "hi":
        return lax.dot_general(a.astype(F32), b.astype(F32), dims, precision=HIGHEST,
                               preferred_element_type=F32)
    if mode == "lo":
        return lax.dot_general(a.astype(BF16), b.astype(BF16), dims, preferred_element_type=F32)
    ah = a.astype(BF16)
    al = (a - ah.astype(F32)).astype(BF16)
    bh = b.astype(BF16)
    bl = (b - bh.astype(F32)).astype(BF16)
    dg = functools.partial(lax.dot_general, dimension_numbers=dims, preferred_element_type=F32)
    return dg(ah, bh) + (dg(ah, bl) + dg(al, bh))


def _mm_exact(a, b, dims=NN, left=True):
    x = a if left else b
    xh = x.astype(BF16)
    xl = (x - xh.astype(F32)).astype(BF16)
    dg = functools.partial(lax.dot_general, dimension_numbers=dims, preferred_element_type=F32)
    if left:
        bb = b.astype(BF16)
        return dg(xh, bb) + dg(xl, bb)
    ab = a.astype(BF16)
    return dg(ab, xh) + dg(ab, xl)


def _iota(shape, dim):
    return lax.broadcasted_iota(jnp.int32, shape, dim)


def _group_matrix(n, group, value=1.0):
    same = (_iota((n, n), 0) // group) == (_iota((n, n), 1) // group)
    return jnp.where(same, value, 0.0).astype(F32)


def _softplus(x):
    return jnp.maximum(x, 0.0) + jnp.log1p(jnp.exp(-jnp.abs(x)))


def _sigmoid(x):
    return 1.0 / (1.0 + jnp.exp(-x))


def _ada_kernel(c_ref, w_ref, b_ref, o_ref):
    c = c_ref[...]
    cond = c * _sigmoid(c)
    o_ref[...] = _mm(cond, w_ref[...], NN, "hi") + b_ref[...]


def _ada_mod(c, ada_w, ada_b):
    L, D, D6 = ada_w.shape
    Bn = c.shape[0]
    tn = D6 // 4
    return pl.pallas_call(
        _ada_kernel,
        grid=(L, D6 // tn),
        in_specs=[
            pl.BlockSpec((Bn, D), lambda l, j: (0, 0)),
            pl.BlockSpec((None, D, tn), lambda l, j: (l, 0, j)),
            pl.BlockSpec((None, 1, tn), lambda l, j: (l, 0, j)),
        ],
        out_specs=pl.BlockSpec((None, Bn, tn), lambda l, j: (l, 0, j)),
        out_shape=jax.ShapeDtypeStruct((L, Bn, D6), F32),
        compiler_params=_params(("arbitrary", "arbitrary")),
        name="ada_mod",
    )(c, ada_w, ada_b.reshape(L, 1, D6))


def _modulated_norm(x, g, shift, scale):
    y = x * lax.rsqrt(jnp.mean(x * x, axis=-1, keepdims=True) + EPS) * g
    return y * (1.0 + scale) + shift


def _norm_kernel(x_ref, mod_ref, g_ref, o_ref, *, row0):
    mod = mod_ref[...]
    h = _modulated_norm(x_ref[...], g_ref[...], mod[row0:row0 + 1], mod[row0 + 1:row0 + 2])
    o_ref[...] = h.astype(BF16)


def _norm_mod(x2, mod_l, g, *, seq, row0, tm):
    T, D = x2.shape
    per_seq = seq // tm
    return pl.pallas_call(
        functools.partial(_norm_kernel, row0=row0),
        grid=(T // tm,),
        in_specs=[
            pl.BlockSpec((tm, D), lambda i: (i, 0)),
            pl.BlockSpec((None, 6, D), lambda i: (i // per_seq, 0, 0)),
            pl.BlockSpec((1, D), lambda i: (0, 0)),
        ],
        out_specs=pl.BlockSpec((tm, D), lambda i: (i, 0)),
        out_shape=jax.ShapeDtypeStruct((T, D), BF16),
        compiler_params=_params(("arbitrary",)),
        name="norm_mod",
    )(x2, mod_l, g.reshape(1, D))


def _blockdiag(mp, head_lane_masks):
    return jnp.concatenate([jnp.where(m, mp, 0.0) for m in head_lane_masks], axis=0)


def _apply_packed(mp, rhs, head_lane_masks, C, mode):
    full = _mm(mp, rhs, TN, mode)
    out = jnp.where(head_lane_masks[0], full[0:C], 0.0)
    for h in range(1, HG):
        out = out + jnp.where(head_lane_masks[h], full[h * C:(h + 1) * C], 0.0)
    return out


def _rwkv_kernel(h_ref, wa_ref, mu_ref, lora_ref, vec_ref, o_ref,
                 carry_ref, st_ref, hs_ref, *, C, mode):
    TT = h_ref.shape[0]

    @pl.when(pl.program_id(1) == 0)
    def _():
        carry_ref[...] = jnp.zeros_like(carry_ref)
        st_ref[...] = jnp.zeros_like(st_ref)

    vec = vec_ref[...]
    w0, a0, k_k, k_a, r_k, ln_g, ln_b = (vec[i:i + 1] for i in range(7))

    pa = jnp.dot(h_ref[...], wa_ref[...], preferred_element_type=F32)
    prev = pltpu.roll(pa, 1, axis=0)
    prev = jnp.where(_iota((TT, 1), 0) == 0, carry_ref[...], prev)
    carry_ref[...] = pa[TT - 1:TT]
    pa = pa + (prev - pa) * mu_ref[...]

    r = pa[:, 0:GW]
    k = pa[:, GW:2 * GW]
    v = pa[:, 2 * GW:3 * GW]
    lo = pa[:, 3 * GW:3 * GW + LANES]
    lane = _iota((1, LANES), 1)
    act = jnp.where(lane < RW_DECAY_RANK, jnp.tanh(lo),
                    jnp.where(lane < RW_DECAY_RANK + RW_A_RANK, lo, _sigmoid(lo)))
    lora = _mm_exact(act, lora_ref[...])
    w_log = -_softplus(-(w0 + lora[:, 0:GW])) - 0.5
    lw = -jnp.exp(w_log)
    a = _sigmoid(a0 + lora[:, GW:2 * GW])
    gate = lora[:, 2 * GW:3 * GW]

    gsum = _group_matrix(GW, HEAD_DIM)
    kk = k * k_k
    kk = kk / jnp.maximum(jnp.sqrt(_mm(kk * kk, gsum, NN, "lo")), 1e-12)
    k2 = k * (1.0 + (a - 1.0) * k_a)
    bonus = _mm(r * k2 * r_k, gsum, NN, "lo") * v

    aa = -kk
    b = kk * a

    lane_head = _iota((1, GW), 1) // HEAD_DIM
    hmask = [lane_head == h for h in range(HG)]
    s_idx = _iota((C, GW), 0)
    t_idx = _iota((C, GW), 1) % C
    strict = s_idx < t_idx
    incl = s_idx <= t_idx
    eye_p = jnp.where(s_idx == t_idx, 1.0, 0.0).astype(F32)
    l_incl = jnp.where(_iota((C, C), 0) >= _iota((C, C), 1), 1.0, 0.0).astype(F32)
    bd = (_iota((GW, GW), 0) // HEAD_DIM) == (_iota((GW, GW), 1) // HEAD_DIM)
    eye_k = jnp.where(_iota((GW, GW), 0) == _iota((GW, GW), 1), 1.0, 0.0).astype(F32)
    chunks = range(TT // C)
    sl = lambda x, c: x[c * C:(c + 1) * C]

    cl = jnp.concatenate([_mm_exact(l_incl, sl(lw, c), left=False) for c in chunks], axis=0)
    g_in = jnp.exp(cl)
    g_inv = jnp.exp(-cl)
    at = aa * jnp.exp(cl - lw)
    rt = r * g_in
    bt = b * g_inv
    kt = k2 * g_inv

    bks = [jnp.concatenate([sl(bt, c), sl(kt, c)], axis=0) for c in chunks]
    xs = [_mm(bks[c], jnp.concatenate([_blockdiag(sl(at, c), hmask), _blockdiag(sl(rt, c), hmask)], axis=0),
              NT, mode) for c in chunks]
    n_p = [jnp.where(strict, x[0:C, 0:GW], 0.0) for x in xs]
    ak_p = [jnp.where(strict, x[C:2 * C, 0:GW], 0.0) for x in xs]
    rbk_p = [jnp.concatenate([jnp.where(incl, x[0:C, GW:2 * GW], 0.0),
                              jnp.where(incl, x[C:2 * C, GW:2 * GW], 0.0)], axis=0) for x in xs]

    p_p = [eye_p + n for n in n_p]
    q_p = n_p
    bd_dtype = BF16 if mode == "lo" else F32
    q_bd = [_blockdiag(q, hmask).astype(bd_dtype) for q in q_p]
    for _ in range(int(math.log2(C)) - 1):
        q_p = [_mm(q, b, NN, mode) for q, b in zip(q_p, q_bd)]
        q_bd = [_blockdiag(q, hmask).astype(bd_dtype) for q in q_p]
        p_p = [p + _mm(p, b, NN, mode) for p, b in zip(p_p, q_bd)]

    ta = [_apply_packed(p_p[c], sl(at, c), hmask, C, mode) for c in chunks]
    wv = [_apply_packed(ak_p[c], sl(v, c), hmask, C, mode) for c in chunks]
    tw = [_apply_packed(p_p[c], wv[c], hmask, C, mode) for c in chunks]
    g_end = [g_in[(c + 1) * C - 1:(c + 1) * C] for c in chunks]
    m_t = [(eye_k + jnp.where(bd, _mm(ta[c], sl(bt, c), TN, mode), 0.0)) * g_end[c] for c in chunks]
    g_t = [jnp.where(bd, _mm(jnp.concatenate([tw[c], sl(v, c)], axis=0), bks[c], TN, mode), 0.0) * g_end[c]
           for c in chunks]

    ht = st_ref[...]
    for c in chunks:
        hs_ref[c] = ht
        ht = _mm(ht, m_t[c], NN, mode) + g_t[c]
    st_ref[...] = ht

    ys = []
    for c in chunks:
        sh = _mm(jnp.concatenate([ta[c], sl(rt, c)], axis=0), hs_ref[c], NT, mode)
        uv = jnp.concatenate([sh[0:C] + tw[c], sl(v, c)], axis=0)
        ys.append(sh[C:2 * C] + _apply_packed(rbk_p[c], uv, hmask, C, mode))
    y = jnp.concatenate(ys, axis=0)

    gmean = _group_matrix(GW, HEAD_DIM, 1.0 / HEAD_DIM)
    yc = y - _mm(y, gmean, NN, "lo")
    yn = yc * lax.rsqrt(_mm(yc * yc, gmean, NN, "lo") + RW_GN_EPS) * ln_g + ln_b
    o_ref[...] = ((yn + bonus) * gate).astype(BF16)


def _rwkv_mixer(h, wa, mu, lora_w, vec, *, batch, seq, tt, mode):
    T, D = h.shape
    per_seq = seq // tt
    C = RW_CHUNK
    return pl.pallas_call(
        functools.partial(_rwkv_kernel, C=C, mode=mode),
        grid=(batch, per_seq),
        in_specs=[
            pl.BlockSpec((tt, D), lambda b, i: (b * per_seq + i, 0)),
            pl.BlockSpec((D, RW_COLS), lambda b, i: (0, 0)),
            pl.BlockSpec((1, RW_COLS), lambda b, i: (0, 0)),
            pl.BlockSpec((LANES, 3 * GW), lambda b, i: (0, 0)),
            pl.BlockSpec((8, GW), lambda b, i: (0, 0)),
        ],
        out_specs=pl.BlockSpec((tt, GW), lambda b, i: (b * per_seq + i, 0)),
        out_shape=jax.ShapeDtypeStruct((T, GW), BF16),
        scratch_shapes=[pltpu.VMEM((1, RW_COLS), F32), pltpu.VMEM((GW, GW), F32),
                        pltpu.VMEM((tt // C, GW, GW), F32)],
        compiler_params=_params(("arbitrary", "arbitrary")),
        name="rwkv7_mixer",
    )(h, wa, mu, lora_w, vec)


def _place_heads_lanes(n_in, width):
    src = _iota((n_in, HG * SLAB), 0)
    dst = _iota((n_in, HG * SLAB), 1)
    ok = ((dst // SLAB) == (src // width)) & ((dst % SLAB) == (src % width)) & ((dst % SLAB) < width)
    return jnp.where(ok, 1.0, 0.0).astype(BF16)


def _slab_rows(xt, extra):
    parts = []
    for h in range(HG):
        parts.append(xt[h * HEAD_DIM:(h + 1) * HEAD_DIM])
        parts.append(extra)
    return jnp.concatenate(parts, axis=0)


def _diff_prep_kernel(h_ref, wk_ref, wqt_ref, wvt_ref, gk_ref, gqt_ref,
                      cos_ref, sin_ref, cost_ref, sint_ref, q_out, k_out, v_out, *, tile):
    tm = h_ref.shape[0]
    hb = h_ref[...]
    k = jnp.dot(hb, wk_ref[...], preferred_element_type=F32)
    qt = lax.dot_general(wqt_ref[...], hb, NT, preferred_element_type=F32)
    vt = lax.dot_general(wvt_ref[...], hb, NT, preferred_element_type=F32)
    gmean = _group_matrix(GW, DQ, 1.0 / DQ)

    kn = k * lax.rsqrt(_mm(k * k, gmean, NN, "lo") + EPS) * gk_ref[...]
    first_half = (_iota((1, GW), 1) % DQ) < (DQ // 2)
    partner = jnp.where(first_half, pltpu.roll(kn, GW - DQ // 2, axis=1), pltpu.roll(kn, DQ // 2, axis=1))
    kr = kn * cos_ref[...] + partner * sin_ref[...]
    k_out[...] = jnp.dot(kr.astype(BF16), _place_heads_lanes(GW, HEAD_DIM),
                         preferred_element_type=F32).astype(BF16)

    qn = qt * lax.rsqrt(_mm(gmean, qt * qt, NN, "lo") + EPS) * gqt_ref[...]
    first_half_t = (_iota((GW, 1), 0) % DQ) < (DQ // 2)
    partner_t = jnp.where(first_half_t, pltpu.roll(qn, GW - DQ // 2, axis=0), pltpu.roll(qn, DQ // 2, axis=0))
    qr = (qn * cost_ref[...] + partner_t * sint_ref[...]).astype(BF16)
    qs = _slab_rows(qr, jnp.zeros((SLAB - HEAD_DIM, tm), BF16))
    vb = _slab_rows(vt.astype(BF16),
                    jnp.where(_iota((VROWS - HEAD_DIM, tm), 0) == 0, 1.0, 0.0).astype(BF16))
    for c in range(tm // tile):
        q_out[c] = qs[:, c * tile:(c + 1) * tile]
        v_out[c] = vb[:, c * tile:(c + 1) * tile]


def _diff_prep(h, wk, wqt, wvt, gk, gqt, cos, sin, cost, sint, *, seq, tm, tile):
    T, D = h.shape
    per_seq = seq // tm
    nt = tm // tile
    const = lambda i: (0, 0)
    return pl.pallas_call(
        functools.partial(_diff_prep_kernel, tile=tile),
        grid=(T // tm,),
        in_specs=[
            pl.BlockSpec((tm, D), lambda i: (i, 0)),
            pl.BlockSpec((D, GW), const),
            pl.BlockSpec((GW, D), const),
            pl.BlockSpec((GW, D), const),
            pl.BlockSpec((1, GW), const),
            pl.BlockSpec((GW, tm), const),
            pl.BlockSpec((tm, GW), lambda i: (i % per_seq, 0)),
            pl.BlockSpec((tm, GW), lambda i: (i % per_seq, 0)),
            pl.BlockSpec((GW, tm), lambda i: (0, i % per_seq)),
            pl.BlockSpec((GW, tm), lambda i: (0, i % per_seq)),
        ],
        out_specs=[
            pl.BlockSpec((nt, HG * SLAB, tile), lambda i: (i, 0, 0)),
            pl.BlockSpec((tm, HG * SLAB), lambda i: (i, 0)),
            pl.BlockSpec((nt, HG * VROWS, tile), lambda i: (i, 0, 0)),
        ],
        out_shape=[
            jax.ShapeDtypeStruct((T // tile, HG * SLAB, tile), BF16),
            jax.ShapeDtypeStruct((T, HG * SLAB), BF16),
            jax.ShapeDtypeStruct((T // tile, HG * VROWS, tile), BF16),
        ],
        compiler_params=_params(("arbitrary",)),
        name="diff_prep",
    )(h, wk, wqt, wvt, gk, gqt, cos, sin, cost, sint)


def _fox_prep_kernel(h_ref, wk_ref, wf_ref, wqt_ref, wvt_ref, gk_ref, gqt_ref, fb_ref,
                     q_out, k_out, v_out, fcarry_ref, *, tile):
    tm = h_ref.shape[0]

    @pl.when(pl.program_id(1) == 0)
    def _():
        fcarry_ref[...] = jnp.zeros_like(fcarry_ref)

    hb = h_ref[...]
    k = jnp.dot(hb, wk_ref[...], preferred_element_type=F32)
    fl = jnp.dot(hb, wf_ref[...], preferred_element_type=F32)
    qt = lax.dot_general(wqt_ref[...], hb, NT, preferred_element_type=F32)
    vt = lax.dot_general(wvt_ref[...], hb, NT, preferred_element_type=F32)
    gmean = _group_matrix(GW, HEAD_DIM, 1.0 / HEAD_DIM)

    kn = (k * lax.rsqrt(_mm(k * k, gmean, NN, "lo") + EPS) * gk_ref[...]).astype(BF16)

    log_f = -_softplus(-(fl + fb_ref[...]))
    cb = min(tm, LANES)
    l_incl = jnp.where(_iota((cb, cb), 0) >= _iota((cb, cb), 1), 1.0, 0.0).astype(F32)
    carry = fcarry_ref[...]
    blocks = []
    for r in range(tm // cb):
        blocks.append(_mm_exact(l_incl, log_f[r * cb:(r + 1) * cb], left=False) + carry)
        carry = blocks[-1][cb - 1:cb]
    fcarry_ref[...] = carry
    nf = jnp.concatenate(blocks, axis=0) * (-LOG2E)
    head_lane = _iota((1, LANES), 1) < HG
    packed = jnp.zeros_like(nf)
    for i in range(3):
        p = jnp.where(head_lane, nf.astype(BF16).astype(F32), 0.0)
        packed = packed + (p if i == 0 else pltpu.roll(p, HG * i, axis=1))
        nf = nf - p
    src = _iota((LANES, HG * SLAB), 0)
    dst = _iota((LANES, HG * SLAB), 1)
    place = jnp.where((src < 3 * HG) & (dst == (src % HG) * SLAB + HEAD_DIM + src // HG), 1.0, 0.0)
    ks = (jnp.dot(kn, _place_heads_lanes(GW, HEAD_DIM), preferred_element_type=F32)
          + jnp.dot(packed.astype(BF16), place.astype(BF16), preferred_element_type=F32))
    k_out[...] = ks.astype(BF16)

    qn = (qt * lax.rsqrt(_mm(gmean, qt * qt, NN, "lo") + EPS) * gqt_ref[...]).astype(BF16)
    ones_rows = jnp.where(_iota((SLAB - HEAD_DIM, tm), 0) < 3, 1.0, 0.0).astype(BF16)
    qs = _slab_rows(qn, ones_rows)
    vb = _slab_rows(vt.astype(BF16),
                    jnp.where(_iota((VROWS - HEAD_DIM, tm), 0) == 0, 1.0, 0.0).astype(BF16))
    for c in range(tm // tile):
        q_out[c] = qs[:, c * tile:(c + 1) * tile]
        v_out[c] = vb[:, c * tile:(c + 1) * tile]


def _fox_prep(h, wk, wf, wqt, wvt, gk, gqt, fb, *, batch, seq, tm, tile):
    T, D = h.shape
    per_seq = seq // tm
    nt = tm // tile
    const = lambda b, i: (0, 0)
    row = lambda b, i: (b * per_seq + i, 0)
    return pl.pallas_call(
        functools.partial(_fox_prep_kernel, tile=tile),
        grid=(batch, per_seq),
        in_specs=[
            pl.BlockSpec((tm, D), row),
            pl.BlockSpec((D, GW), const),
            pl.BlockSpec((D, LANES), const),
            pl.BlockSpec((GW, D), const),
            pl.BlockSpec((GW, D), const),
            pl.BlockSpec((1, GW), const),
            pl.BlockSpec((GW, tm), const),
            pl.BlockSpec((1, LANES), const),
        ],
        out_specs=[
            pl.BlockSpec((nt, HG * SLAB, tile), lambda b, i: (b * per_seq + i, 0, 0)),
            pl.BlockSpec((tm, HG * SLAB), row),
            pl.BlockSpec((nt, HG * VROWS, tile), lambda b, i: (b * per_seq + i, 0, 0)),
        ],
        out_shape=[
            jax.ShapeDtypeStruct((T // tile, HG * SLAB, tile), BF16),
            jax.ShapeDtypeStruct((T, HG * SLAB), BF16),
            jax.ShapeDtypeStruct((T // tile, HG * VROWS, tile), BF16),
        ],
        scratch_shapes=[pltpu.VMEM((1, LANES), F32)],
        compiler_params=_params(("arbitrary", "arbitrary")),
        name="fox_prep",
    )(h, wk, wf, wqt, wvt, gk, gqt, fb)


def _attn_kernel(q_ref, k_ref, v_ref, aux_ref, gain_ref, o_ref, qz_s, m_s, acc_s,
                 s_a, s_b, p_a, p_b, al_a, al_b, mt_a, mt_b, *, n_maps, tile, lambda_init):
    qi = pl.program_id(1)
    row = _iota((tile, tile), 0)
    col = _iota((tile, tile), 1)
    if n_maps == 2:
        diag_ok = (row // ATT_CHUNK) <= (col // ATT_CHUNK)
    else:
        diag_ok = row <= col

    slab_row = _iota((SLAB, 1), 0)
    for h in range(HG):
        qh = q_ref[h * SLAB:(h + 1) * SLAB, :]
        if n_maps == 2:
            qz_s[h, :, 0:tile] = jnp.where(slab_row < DQ, qh, jnp.zeros_like(qh))
            qz_s[h, :, tile:2 * tile] = jnp.where((slab_row >= DQ) & (slab_row < 2 * DQ), qh,
                                                  jnp.zeros_like(qh))
        else:
            qz_s[h] = qh
    m_s[...] = jnp.full(m_s.shape, NEG_INF, F32)
    acc_s[...] = jnp.zeros(acc_s.shape, F32)
    p_b[...] = jnp.zeros(p_b.shape, BF16)
    al_b[...] = jnp.zeros(al_b.shape, F32)

    buf_a = (s_a, p_a, al_a, mt_a)
    buf_b = (s_b, p_b, al_b, mt_b)

    all_heads = tuple(range(HG))

    def scores(j, s_buf, mt_buf, heads=all_heads):
        for h in heads:
            kj = k_ref[pl.ds(pl.multiple_of(j * tile, tile), tile), h * SLAB:(h + 1) * SLAB]
            s = jnp.dot(kj, qz_s[h], preferred_element_type=F32)
            s_buf[h] = s
            for mp in range(n_maps):
                c = h * n_maps + mp
                mt_buf[c:c + 1, :] = jnp.max(s[:, mp * tile:(mp + 1) * tile], axis=0, keepdims=True)

    def value_update(j, p_buf, al_buf, heads=all_heads):
        for h in heads:
            vj = v_ref[j, h * VROWS:(h + 1) * VROWS, :]
            for mp in range(n_maps):
                c = h * n_maps + mp
                rows = slice(c * VROWS, (c + 1) * VROWS)
                pv = jnp.dot(vj, p_buf[c], preferred_element_type=F32)
                acc_s[rows, :] = al_buf[c:c + 1, :] * acc_s[rows, :] + pv

    def softmax(s_buf, p_buf, al_buf, mt_buf, masked, heads=all_heads):
        for h in heads:
            for mp in range(n_maps):
                c = h * n_maps + mp
                s = s_buf[h, :, mp * tile:(mp + 1) * tile]
                if masked:
                    s = jnp.where(diag_ok, s, NEG_INF)
                    tile_max = jnp.max(s, axis=0, keepdims=True)
                else:
                    tile_max = mt_buf[c:c + 1, :]
                m_old = m_s[c:c + 1, :]
                m_new = jnp.maximum(m_old, tile_max)
                m_s[c:c + 1, :] = m_new
                al_buf[c:c + 1, :] = jnp.exp2(m_old - m_new)
                p_buf[c] = jnp.exp2(s - m_new).astype(BF16)

    def step(j, cur, nxt):
        for h in all_heads:
            scores(j + 1, nxt[0], nxt[3], (h,))
            value_update(jnp.maximum(j - 1, 0), nxt[1], nxt[2], (h,))
            softmax(cur[0], cur[1], cur[2], cur[3], False, (h,))

    def last(j, cur, nxt):
        value_update(jnp.maximum(j - 1, 0), nxt[1], nxt[2])
        softmax(cur[0], cur[1], cur[2], cur[3], True)
        value_update(j, cur[1], cur[2])

    scores(0, s_a, mt_a)

    def pair(j):
        step(j, buf_a, buf_b)
        step(j + 1, buf_b, buf_a)

    pairs_per_trip = 2 if n_maps == 1 else 1
    shift = pairs_per_trip.bit_length()

    def trip(jj, carry):
        for t in range(pairs_per_trip):
            pair(2 * pairs_per_trip * jj + 2 * t)
        return carry

    lax.fori_loop(0, lax.shift_right_logical(qi, shift), trip, 0)
    odd = (qi & 1) == 1

    if pairs_per_trip == 2:
        @pl.when((qi & 2) == 2)
        def _():
            pair(qi - (qi & 3))

    @pl.when(odd)
    def _():
        step(qi - 1, buf_a, buf_b)
        last(qi, buf_b, buf_a)

    @pl.when(jnp.logical_not(odd))
    def _():
        last(qi, buf_a, buf_b)

    if n_maps == 2:
        aux = aux_ref[...]
        lam = (jnp.exp(jnp.sum(aux[0:1] * aux[1:2], axis=1, keepdims=True))
               - jnp.exp(jnp.sum(aux[2:3] * aux[3:4], axis=1, keepdims=True)) + lambda_init)
    outs = []
    for h in range(HG):
        c = h * n_maps
        normalized = lambda cc: (acc_s[cc * VROWS:cc * VROWS + HEAD_DIM, :]
                                 / acc_s[cc * VROWS + HEAD_DIM:cc * VROWS + HEAD_DIM + 1, :])
        o = normalized(c)
        if n_maps == 2:
            o = o - lam * normalized(c + 1)
            o = o * lax.rsqrt(jnp.mean(o * o, axis=0, keepdims=True) + EPS)
        outs.append(o)
    ot = jnp.concatenate(outs, axis=0)
    if n_maps == 2:
        ot = ot * gain_ref[...]
    o_ref[...] = ot.T.astype(BF16)


def _attention(q3, k2, v3, aux, gain, *, batch, seq, tile, n_maps, lambda_init):
    nq = seq // tile
    T = batch * seq
    n_chains = HG * n_maps
    scratch = [pltpu.VMEM((HG, SLAB, n_maps * tile), BF16),
               pltpu.VMEM((n_chains, tile), F32),
               pltpu.VMEM((n_chains * VROWS, tile), F32),
               pltpu.VMEM((HG, tile, n_maps * tile), F32),
               pltpu.VMEM((HG, tile, n_maps * tile), F32),
               pltpu.VMEM((n_chains, tile, tile), BF16),
               pltpu.VMEM((n_chains, tile, tile), BF16)] + [pltpu.VMEM((n_chains, tile), F32)] * 4
    return pl.pallas_call(
        functools.partial(_attn_kernel, n_maps=n_maps, tile=tile, lambda_init=lambda_init),
        grid=(batch, nq),
        in_specs=[
            pl.BlockSpec((None, HG * SLAB, tile), lambda b, i: (b * nq + i, 0, 0)),
            pl.BlockSpec((seq, HG * SLAB), lambda b, i: (b, 0)),
            pl.BlockSpec((nq, HG * VROWS, tile), lambda b, i: (b, 0, 0)),
            pl.BlockSpec(aux.shape, lambda b, i: (0, 0)),
            pl.BlockSpec(gain.shape, lambda b, i: (0, 0)),
        ],
        out_specs=pl.BlockSpec((tile, GW), lambda b, i: (b * nq + i, 0)),
        out_shape=jax.ShapeDtypeStruct((T, GW), BF16),
        scratch_shapes=scratch,
        compiler_params=_params(("arbitrary", "arbitrary")),
        name="diff_attention" if n_maps == 2 else "forgetting_attention",
    )(q3, k2, v3, aux, gain)


def _erf_gelu(x):
    return 0.5 * x * (1.0 + lax.erf(x * (1.0 / math.sqrt(2.0))))


def _gmlp_kernel(h_ref, wc_ref, sgw_ref, bias_ref, ln_ref, o_ref):
    tm = h_ref.shape[0]
    pc = _erf_gelu(jnp.dot(h_ref[...], wc_ref[...], preferred_element_type=F32))
    u = pc[:, 0:GW]
    v = pc[:, GW:2 * GW]
    mu = jnp.mean(v, axis=-1, keepdims=True)
    vc = v - mu
    var = jnp.mean(vc * vc, axis=-1, keepdims=True)
    ln = ln_ref[...]
    vn = (vc * lax.rsqrt(var + EPS) * ln[0:1] + ln[1:2]).astype(BF16)

    causal = _iota((SG_CHUNK, SG_CHUNK), 0) >= _iota((SG_CHUNK, SG_CHUNK), 1)
    w = jnp.concatenate([jnp.where(causal, sgw_ref[g], 0.0) for g in range(HG)], axis=0).astype(BF16)
    lane_head = _iota((1, GW), 1) // HEAD_DIM
    bias = bias_ref[...]
    for n in range(tm // SG_CHUNK):
        rows = slice(n * SG_CHUNK, (n + 1) * SG_CHUNK)
        full = jnp.dot(w, vn[rows], preferred_element_type=F32)
        sv = bias
        for g in range(HG):
            sv = sv + jnp.where(lane_head == g, full[g * SG_CHUNK:(g + 1) * SG_CHUNK], 0.0)
        o_ref[rows, :] = (u[rows] * sv).astype(BF16)


def _gmlp_mixer(h, wc, sg_w, bias, ln, *, tm):
    T, D = h.shape
    return pl.pallas_call(
        _gmlp_kernel,
        grid=(T // tm,),
        in_specs=[
            pl.BlockSpec((tm, D), lambda i: (i, 0)),
            pl.BlockSpec((D, 2 * GW), lambda i: (0, 0)),
            pl.BlockSpec((HG, SG_CHUNK, SG_CHUNK), lambda i: (0, 0, 0)),
            pl.BlockSpec((SG_CHUNK, GW), lambda i: (0, 0)),
            pl.BlockSpec((2, GW), lambda i: (0, 0)),
        ],
        out_specs=pl.BlockSpec((tm, GW), lambda i: (i, 0)),
        out_shape=jax.ShapeDtypeStruct((T, GW), BF16),
        compiler_params=_params(("arbitrary",)),
        name="gmlp_mixer",
    )(h, wc, sg_w, bias, ln)


HALO = 16


FFN_COLS = 256


def _ffn_kernel(yah, ybh, ych, ydh, ya, yb, yc, yd, xh_ref, x_ref, wo_ref, g2_ref,
                wup_ref, cw_ref, cb_ref, wdn_ref, mod_ref, modn_ref, gn_ref,
                o_ref, hn_ref, act_ref, *, per_seq):
    i = pl.program_id(0)
    tm = x_ref.shape[0]
    dff = wdn_ref.shape[0]
    mod = mod_ref[...]

    proj = None
    for m, (halo, main) in enumerate(((yah, ya), (ybh, yb), (ych, yc), (ydh, yd))):
        ym = jnp.concatenate([halo[...], main[...]], axis=0)
        part = jnp.dot(ym, wo_ref[m], preferred_element_type=F32)
        proj = part if proj is None else proj + part
    x_mid = jnp.concatenate([xh_ref[...], x_ref[...]], axis=0) + mod[2:3] * proj
    hx = _modulated_norm(x_mid, g2_ref[...], mod[3:4], mod[4:5]).astype(BF16)
    zero_halo = (_iota((HALO + tm, 1), 0) < HALO) & ((i % per_seq) == 0)

    def conv_half(cols):
        u = jnp.dot(hx, wup_ref[:, cols], preferred_element_type=F32)
        u = jnp.where(zero_halo, 0.0, u)
        cw = cw_ref[:, cols]
        y = (cw[0:1] * pltpu.roll(u, 2, axis=0) + cw[1:2] * pltpu.roll(u, 1, axis=0) + cw[2:3] * u)
        return y[HALO:] + cb_ref[:, cols]

    for j in range(dff // FFN_COLS):
        a = conv_half(slice(j * FFN_COLS, (j + 1) * FFN_COLS))
        g = conv_half(slice(dff + j * FFN_COLS, dff + (j + 1) * FFN_COLS))
        act_ref[:, j * FFN_COLS:(j + 1) * FFN_COLS] = (a * (g * _sigmoid(g))).astype(BF16)

    y = jnp.dot(act_ref[...], wdn_ref[...], preferred_element_type=F32)
    x = x_mid[HALO:] + mod[5:6] * y
    o_ref[...] = x
    modn = modn_ref[...]
    hn_ref[...] = _modulated_norm(x, gn_ref[...], modn[0:1], modn[1:2]).astype(BF16)


def _ffn(ys, x2, w_out4, g2, wup, cw, cb, wdn, mod_l, mod_next, g_next, *, seq, tm):
    T, D = x2.shape
    dff = wdn.shape[0]
    per_seq = seq // tm
    halo_blocks = tm // HALO
    once = pl.Buffered(1)
    row = lambda i: (i, 0)
    halo = lambda i: (jnp.maximum(i * halo_blocks - 1, 0), 0)
    return pl.pallas_call(
        functools.partial(_ffn_kernel, per_seq=per_seq),
        grid=(T // tm,),
        in_specs=[pl.BlockSpec((HALO, GW), halo)] * N_MIXERS + [pl.BlockSpec((tm, GW), row)] * N_MIXERS + [
            pl.BlockSpec((HALO, D), halo),
            pl.BlockSpec((tm, D), row),
            pl.BlockSpec((N_MIXERS, GW, D), lambda i: (0, 0, 0), pipeline_mode=once),
            pl.BlockSpec((1, D), lambda i: (0, 0)),
            pl.BlockSpec((D, 2 * dff), lambda i: (0, 0), pipeline_mode=once),
            pl.BlockSpec((3, 2 * dff), lambda i: (0, 0), pipeline_mode=once),
            pl.BlockSpec((1, 2 * dff), lambda i: (0, 0), pipeline_mode=once),
            pl.BlockSpec((dff, D), lambda i: (0, 0), pipeline_mode=once),
            pl.BlockSpec((None, 6, D), lambda i: (i // per_seq, 0, 0)),
            pl.BlockSpec((None, 6, D), lambda i: (i // per_seq, 0, 0)),
            pl.BlockSpec((1, D), lambda i: (0, 0)),
        ],
        out_specs=[pl.BlockSpec((tm, D), row), pl.BlockSpec((tm, D), row)],
        out_shape=[jax.ShapeDtypeStruct((T, D), F32), jax.ShapeDtypeStruct((T, D), BF16)],
        scratch_shapes=[pltpu.VMEM((tm, dff), BF16)],
        compiler_params=_params(("arbitrary",)),
        name="conv_glu_ffn",
    )(*ys, *ys, x2, x2, w_out4, g2.reshape(1, D), wup, cw, cb, wdn, mod_l, mod_next, g_next.reshape(1, D))


def _rope_tables(seq):
    inv = 1.0 / (ROPE_THETA ** (jnp.arange(0, DQ, 2, dtype=F32) / DQ))
    ang = jnp.arange(seq, dtype=F32)[:, None] * inv[None, :]
    cos = jnp.cos(ang)
    sin = jnp.sin(ang)
    cos_map = jnp.concatenate([cos, cos], axis=-1)
    sin_map = jnp.concatenate([-sin, sin], axis=-1)
    reps = GW // DQ
    return jnp.tile(cos_map, (1, reps)), jnp.tile(sin_map, (1, reps))


def kernel(x, c, ada_w, ada_b, norm1_g, norm2_g, w_in, w_out, rw_mu, rw_w0, rw_w_up, rw_a0, rw_a_up, rw_g_up, rw_k_k, rw_k_a, rw_r_k, rw_ln_g, rw_ln_b, df_lam_q1, df_lam_k1, df_lam_q2, df_lam_k2, df_q_g, df_k_g, df_sub_g, sg_w, sg_b, sg_ln_g, sg_ln_b, fx_q_g, fx_k_g, fx_f_b, ffn_up, ffn_conv, ffn_conv_b, ffn_down):
    Bn, S, D = x.shape
    L = ada_w.shape[0]
    T = Bn * S
    dff = ffn_down.shape[1]
    tm = min(ROW_TILE, S)
    tile = min(ATT_TILE, S)

    mod = _ada_mod(c, ada_w, ada_b).reshape(L, Bn, 6, D)
    cos, sin = _rope_tables(S)
    cost, sint = cos.T, sin.T
    x2 = x.reshape(T, D)

    oa = 0
    ob = oa + RW_COLS
    oc = ob + 3 * GW
    od = oc + 2 * GW

    for l in range(L):
        lambda_init = 0.8 - 0.6 * math.exp(-0.3 * l)
        wl = w_in[l].astype(BF16)
        if l == 0:
            h = _norm_mod(x2, mod[l], norm1_g[l], seq=S, row0=0, tm=tm)

        lora_w = jnp.zeros((LANES, 3 * GW), F32)
        lora_w = lora_w.at[0:RW_DECAY_RANK, 0:GW].set(rw_w_up[l])
        lora_w = lora_w.at[RW_DECAY_RANK:RW_DECAY_RANK + RW_A_RANK, GW:2 * GW].set(rw_a_up[l])
        lora_w = lora_w.at[RW_DECAY_RANK + RW_A_RANK:LANES, 2 * GW:3 * GW].set(rw_g_up[l])
        vec = jnp.stack([rw_w0[l], rw_a0[l], rw_k_k[l], rw_k_a[l], rw_r_k[l].reshape(GW),
                         rw_ln_g[l], rw_ln_b[l], jnp.zeros((GW,), F32)])
        ya = _rwkv_mixer(h, wl[:, oa:ob], rw_mu[l].reshape(1, RW_COLS), lora_w, vec,
                         batch=Bn, seq=S, tt=min(RW_TILE, S), mode=RW_MODE)

        gk = jnp.tile(df_k_g[l], GW // DQ).reshape(1, GW)
        gqt = jnp.broadcast_to((jnp.tile(df_q_g[l], GW // DQ) * (DQ ** -0.5 * LOG2E))[:, None], (GW, tm))
        qb, kb, vb = _diff_prep(h, wl[:, ob + GW:ob + 2 * GW], wl[:, ob:ob + GW].T, wl[:, ob + 2 * GW:oc].T,
                                gk, gqt, cos, sin, cost, sint, seq=S, tm=tm, tile=tile)
        lam_vecs = jnp.stack([df_lam_q1[l], df_lam_k1[l], df_lam_q2[l], df_lam_k2[l]])
        sub_gain = jnp.broadcast_to((jnp.tile(df_sub_g[l], HG) * (1.0 - lambda_init))[:, None], (GW, tile))
        yb = _attention(qb, kb, vb, lam_vecs, sub_gain, batch=Bn, seq=S, tile=tile,
                        n_maps=2, lambda_init=lambda_init)

        sg_bias = jnp.repeat(sg_b[l].T, HEAD_DIM, axis=1)
        yc = _gmlp_mixer(h, wl[:, oc:od], sg_w[l], sg_bias,
                         jnp.stack([sg_ln_g[l], sg_ln_b[l]]), tm=tm)

        wf = jnp.zeros((D, LANES), BF16).at[:, 0:HG].set(wl[:, od + 3 * GW:od + 3 * GW + HG])
        fb = jnp.zeros((1, LANES), F32).at[0, 0:HG].set(fx_f_b[l])
        gkd = jnp.tile(fx_k_g[l], HG).reshape(1, GW)
        gqd = jnp.broadcast_to((jnp.tile(fx_q_g[l], HG) * (HEAD_DIM ** -0.5 * LOG2E))[:, None], (GW, tm))
        qd, kd, vd = _fox_prep(h, wl[:, od + GW:od + 2 * GW], wf, wl[:, od:od + GW].T,
                               wl[:, od + 2 * GW:od + 3 * GW].T, gkd, gqd, fb,
                               batch=Bn, seq=S, tm=tm, tile=tile)
        yd = _attention(qd, kd, vd, jnp.zeros((8, LANES), F32), jnp.zeros((8, LANES), F32),
                        batch=Bn, seq=S, tile=tile, n_maps=1, lambda_init=0.0)

        nl = (l + 1) % L
        x2, h = _ffn((ya, yb, yc, yd), x2, w_out[l].astype(BF16).reshape(N_MIXERS, GW, D), norm2_g[l],
                     ffn_up[l].astype(BF16), ffn_conv[l], ffn_conv_b[l].reshape(1, 2 * dff),
                     ffn_down[l].astype(BF16), mod[l], mod[nl], norm1_g[nl], seq=S, tm=tm)

    return x2.reshape(Bn, S, D)
```

```python
import functools
import math

import jax
import jax.numpy as jnp
from jax import lax
from jax.experimental import pallas as pl
from jax.experimental.pallas import tpu as pltpu

F32 = jnp.float32
BF16 = jnp.bfloat16
HIGHEST = lax.Precision.HIGHEST

N_MIXERS = 4
HEAD_DIM = 64
HG = 4
GW = HG * HEAD_DIM
DQ = HEAD_DIM // 2
RW_DECAY_RANK = 32
RW_A_RANK = 32
RW_GATE_RANK = 64
RW_COLS = 3 * GW + RW_DECAY_RANK + RW_A_RANK + RW_GATE_RANK
RW_GN_EPS = 64e-5
SG_CHUNK = 128
ATT_CHUNK = 64
ROPE_THETA = 10000.0
EPS = 1e-6
LOG2E = math.log2(math.e)
NEG_INF = -1e30

LANES = 128
V7X_VMEM_BYTES = 64 * 1024 * 1024
VMEM_LIMIT = 56 * 1024 * 1024

ROW_TILE = 512
PREP_TILE = 1024
RW_CHUNK = 64
RW_TILE = 512
ATT_TILE = 256
SLAB = 128
VROWS = HEAD_DIM + 16
RW_MODE = "lo"

NN = (((1,), (0,)), ((), ()))
NT = (((1,), (1,)), ((), ()))
TN = (((0,), (0,)), ((), ()))


def _params(sem):
    return pltpu.CompilerParams(dimension_semantics=sem, vmem_limit_bytes=VMEM_LIMIT)


def _mm(a, b, dims=NN, mode="hi"):
    if mode == "hi":
        return lax.dot_general(a.astype(F32), b.astype(F32), dims, precision=HIGHEST,
                               preferred_element_type=F32)
    if mode == "lo":
        return lax.dot_general(a.astype(BF16), b.astype(BF16), dims, preferred_element_type=F32)
    ah = a.astype(BF16)
    al = (a - ah.astype(F32)).astype(BF16)
    bh = b.astype(BF16)
    bl = (b - bh.astype(F32)).astype(BF16)
    dg = functools.partial(lax.dot_general, dimension_numbers=dims, preferred_element_type=F32)
    return dg(ah, bh) + (dg(ah, bl) + dg(al, bh))


def _mm_exact(a, b, dims=NN, left=True):
    x = a if left else b
    xh = x.astype(BF16)
    xl = (x - xh.astype(F32)).astype(BF16)
    dg = functools.partial(lax.dot_general, dimension_numbers=dims, preferred_element_type=F32)
    if left:
        bb = b.astype(BF16)
        return dg(xh, bb) + dg(xl, bb)
    ab = a.astype(BF16)
    return dg(ab, xh) + dg(ab, xl)


def _iota(shape, dim):
    return lax.broadcasted_iota(jnp.int32, shape, dim)


def _group_matrix(n, group, value=1.0):
    same = (_iota((n, n), 0) // group) == (_iota((n, n), 1) // group)
    return jnp.where(same, value, 0.0).astype(F32)


def _softplus(x):
    return jnp.maximum(x, 0.0) + jnp.log1p(jnp.exp(-jnp.abs(x)))


def _sigmoid(x):
    return 1.0 / (1.0 + jnp.exp(-x))


def _ada_kernel(c_ref, w_ref, b_ref, o_ref):
    c = c_ref[...]
    cond = c * _sigmoid(c)
    o_ref[...] = _mm(cond, w_ref[...], NN, "hi") + b_ref[...]


def _ada_mod(c, ada_w, ada_b):
    L, D, D6 = ada_w.shape
    Bn = c.shape[0]
    tn = D6 // 4
    return pl.pallas_call(
        _ada_kernel,
        grid=(L, D6 // tn),
        in_specs=[
            pl.BlockSpec((Bn, D), lambda l, j: (0, 0)),
            pl.BlockSpec((None, D, tn), lambda l, j: (l, 0, j)),
            pl.BlockSpec((None, 1, tn), lambda l, j: (l, 0, j)),
        ],
        out_specs=pl.BlockSpec((None, Bn, tn), lambda l, j: (l, 0, j)),
        out_shape=jax.ShapeDtypeStruct((L, Bn, D6), F32),
        compiler_params=_params(("arbitrary", "arbitrary")),
        name="ada_mod",
    )(c, ada_w, ada_b.reshape(L, 1, D6))


def _modulated_norm(x, g, shift, scale):
    y = x * lax.rsqrt(jnp.mean(x * x, axis=-1, keepdims=True) + EPS) * g
    return y * (1.0 + scale) + shift


def _norm_kernel(x_ref, mod_ref, g_ref, o_ref, *, row0):
    mod = mod_ref[...]
    h = _modulated_norm(x_ref[...], g_ref[...], mod[row0:row0 + 1], mod[row0 + 1:row0 + 2])
    o_ref[...] = h.astype(BF16)


def _norm_mod(x2, mod_l, g, *, seq, row0, tm):
    T, D = x2.shape
    per_seq = seq // tm
    return pl.pallas_call(
        functools.partial(_norm_kernel, row0=row0),
        grid=(T // tm,),
        in_specs=[
            pl.BlockSpec((tm, D), lambda i: (i, 0)),
            pl.BlockSpec((None, 6, D), lambda i: (i // per_seq, 0, 0)),
            pl.BlockSpec((1, D), lambda i: (0, 0)),
        ],
        out_specs=pl.BlockSpec((tm, D), lambda i: (i, 0)),
        out_shape=jax.ShapeDtypeStruct((T, D), BF16),
        compiler_params=_params(("arbitrary",)),
        name="norm_mod",
    )(x2, mod_l, g.reshape(1, D))


def _blockdiag(mp, head_lane_masks):
    return jnp.concatenate([jnp.where(m, mp, 0.0) for m in head_lane_masks], axis=0)


def _apply_packed(mp, rhs, head_lane_masks, C, mode):
    full = _mm(mp, rhs, TN, mode)
    out = jnp.where(head_lane_masks[0], full[0:C], 0.0)
    for h in range(1, HG):
        out = out + jnp.where(head_lane_masks[h], full[h * C:(h + 1) * C], 0.0)
    return out


def _rwkv_kernel(h_ref, wa_ref, mu_ref, lora_ref, vec_ref, o_ref,
                 carry_ref, st_ref, hs_ref, *, C, mode):
    TT = h_ref.shape[0]

    @pl.when(pl.program_id(1) == 0)
    def _():
        carry_ref[...] = jnp.zeros_like(carry_ref)
        st_ref[...] = jnp.zeros_like(st_ref)

    vec = vec_ref[...]
    w0, a0, k_k, k_a, r_k, ln_g, ln_b = (vec[i:i + 1] for i in range(7))

    pa = jnp.dot(h_ref[...], wa_ref[...], preferred_element_type=F32)
    prev = pltpu.roll(pa, 1, axis=0)
    prev = jnp.where(_iota((TT, 1), 0) == 0, carry_ref[...], prev)
    carry_ref[...] = pa[TT - 1:TT]
    pa = pa + (prev - pa) * mu_ref[...]

    r = pa[:, 0:GW]
    k = pa[:, GW:2 * GW]
    v = pa[:, 2 * GW:3 * GW]
    lo = pa[:, 3 * GW:3 * GW + LANES]
    lane = _iota((1, LANES), 1)
    act = jnp.where(lane < RW_DECAY_RANK, jnp.tanh(lo),
                    jnp.where(lane < RW_DECAY_RANK + RW_A_RANK, lo, _sigmoid(lo)))
    lora = _mm_exact(act, lora_ref[...])
    w_log = -_softplus(-(w0 + lora[:, 0:GW])) - 0.5
    lw = -jnp.exp(w_log)
    a = _sigmoid(a0 + lora[:, GW:2 * GW])
    gate = lora[:, 2 * GW:3 * GW]

    gsum = _group_matrix(GW, HEAD_DIM)
    kk = k * k_k
    kk = kk / jnp.maximum(jnp.sqrt(_mm(kk * kk, gsum, NN, "lo")), 1e-12)
    k2 = k * (1.0 + (a - 1.0) * k_a)
    bonus = _mm(r * k2 * r_k, gsum, NN, "lo") * v

    aa = -kk
    b = kk * a

    lane_head = _iota((1, GW), 1) // HEAD_DIM
    hmask = [lane_head == h for h in range(HG)]
    s_idx = _iota((C, GW), 0)
    t_idx = _iota((C, GW), 1) % C
    strict = s_idx < t_idx
    incl = s_idx <= t_idx
    eye_p = jnp.where(s_idx == t_idx, 1.0, 0.0).astype(F32)
    l_incl = jnp.where(_iota((C, C), 0) >= _iota((C, C), 1), 1.0, 0.0).astype(F32)
    bd = (_iota((GW, GW), 0) // HEAD_DIM) == (_iota((GW, GW), 1) // HEAD_DIM)
    eye_k = jnp.where(_iota((GW, GW), 0) == _iota((GW, GW), 1), 1.0, 0.0).astype(F32)
    chunks = range(TT // C)
    sl = lambda x, c: x[c * C:(c + 1) * C]

    cl = jnp.concatenate([_mm_exact(l_incl, sl(lw, c), left=False) for c in chunks], axis=0)
    g_in = jnp.exp(cl)
    g_inv = jnp.exp(-cl)
    at = aa * jnp.exp(cl - lw)
    rt = r * g_in
    bt = b * g_inv
    kt = k2 * g_inv

    bks = [jnp.concatenate([sl(bt, c), sl(kt, c)], axis=0) for c in chunks]
    xs = [_mm(bks[c], jnp.concatenate([_blockdiag(sl(at, c), hmask), _blockdiag(sl(rt, c), hmask)], axis=0),
              NT, mode) for c in chunks]
    n_p = [jnp.where(strict, x[0:C, 0:GW], 0.0) for x in xs]
    ak_p = [jnp.where(strict, x[C:2 * C, 0:GW], 0.0) for x in xs]
    rbk_p = [jnp.concatenate([jnp.where(incl, x[0:C, GW:2 * GW], 0.0),
                              jnp.where(incl, x[C:2 * C, GW:2 * GW], 0.0)], axis=0) for x in xs]

    p_p = [eye_p + n for n in n_p]
    q_p = n_p
    bd_dtype = BF16 if mode == "lo" else F32
    q_bd = [_blockdiag(q, hmask).astype(bd_dtype) for q in q_p]
    for _ in range(int(math.log2(C)) - 1):
        q_p = [_mm(q, b, NN, mode) for q, b in zip(q_p, q_bd)]
        q_bd = [_blockdiag(q, hmask).astype(bd_dtype) for q in q_p]
        p_p = [p + _mm(p, b, NN, mode) for p, b in zip(p_p, q_bd)]

    ta = [_apply_packed(p_p[c], sl(at, c), hmask, C, mode) for c in chunks]
    wv = [_apply_packed(ak_p[c], sl(v, c), hmask, C, mode) for c in chunks]
    tw = [_apply_packed(p_p[c], wv[c], hmask, C, mode) for c in chunks]
    g_end = [g_in[(c + 1) * C - 1:(c + 1) * C] for c in chunks]
    m_t = [(eye_k + jnp.where(bd, _mm(ta[c], sl(bt, c), TN, mode), 0.0)) * g_end[c] for c in chunks]
    g_t = [jnp.where(bd, _mm(jnp.concatenate([tw[c], sl(v, c)], axis=0), bks[c], TN, mode), 0.0) * g_end[c]
           for c in chunks]

    ht = st_ref[...]
    for c in chunks:
        hs_ref[c] = ht
        ht = _mm(ht, m_t[c], NN, mode) + g_t[c]
    st_ref[...] = ht

    ys = []
    for c in chunks:
        sh = _mm(jnp.concatenate([ta[c], sl(rt, c)], axis=0), hs_ref[c], NT, mode)
        uv = jnp.concatenate([sh[0:C] + tw[c], sl(v, c)], axis=0)
        ys.append(sh[C:2 * C] + _apply_packed(rbk_p[c], uv, hmask, C, mode))
    y = jnp.concatenate(ys, axis=0)

    gmean = _group_matrix(GW, HEAD_DIM, 1.0 / HEAD_DIM)
    yc = y - _mm(y, gmean, NN, "lo")
    yn = yc * lax.rsqrt(_mm(yc * yc, gmean, NN, "lo") + RW_GN_EPS) * ln_g + ln_b
    o_ref[...] = ((yn + bonus) * gate).astype(BF16)


def _rwkv_mixer(h, wa, mu, lora_w, vec, *, batch, seq, tt, mode):
    T, D = h.shape
    per_seq = seq // tt
    C = RW_CHUNK
    return pl.pallas_call(
        functools.partial(_rwkv_kernel, C=C, mode=mode),
        grid=(batch, per_seq),
        in_specs=[
            pl.BlockSpec((tt, D), lambda b, i: (b * per_seq + i, 0)),
            pl.BlockSpec((D, RW_COLS), lambda b, i: (0, 0)),
            pl.BlockSpec((1, RW_COLS), lambda b, i: (0, 0)),
            pl.BlockSpec((LANES, 3 * GW), lambda b, i: (0, 0)),
            pl.BlockSpec((8, GW), lambda b, i: (0, 0)),
        ],
        out_specs=pl.BlockSpec((tt, GW), lambda b, i: (b * per_seq + i, 0)),
        out_shape=jax.ShapeDtypeStruct((T, GW), BF16),
        scratch_shapes=[pltpu.VMEM((1, RW_COLS), F32), pltpu.VMEM((GW, GW), F32),
                        pltpu.VMEM((tt // C, GW, GW), F32)],
        compiler_params=_params(("arbitrary", "arbitrary")),
        name="rwkv7_mixer",
    )(h, wa, mu, lora_w, vec)


def _place_heads_lanes(n_in, width):
    src = _iota((n_in, HG * SLAB), 0)
    dst = _iota((n_in, HG * SLAB), 1)
    ok = ((dst // SLAB) == (src // width)) & ((dst % SLAB) == (src % width)) & ((dst % SLAB) < width)
    return jnp.where(ok, 1.0, 0.0).astype(BF16)


def _slab_rows(xt, extra):
    parts = []
    for h in range(HG):
        parts.append(xt[h * HEAD_DIM:(h + 1) * HEAD_DIM])
        parts.append(extra)
    return jnp.concatenate(parts, axis=0)


def _diff_prep_kernel(h_ref, wk_ref, wqt_ref, wvt_ref, gk_ref, gqt_ref,
                      cos_ref, sin_ref, cost_ref, sint_ref, q_out, k_out, v_out, *, tile):
    tm = h_ref.shape[0]
    hb = h_ref[...]
    k = jnp.dot(hb, wk_ref[...], preferred_element_type=F32)
    qt = lax.dot_general(wqt_ref[...], hb, NT, preferred_element_type=F32)
    vt = lax.dot_general(wvt_ref[...], hb, NT, preferred_element_type=F32)
    gmean = _group_matrix(GW, DQ, 1.0 / DQ)

    kn = k * lax.rsqrt(_mm(k * k, gmean, NN, "lo") + EPS) * gk_ref[...]
    first_half = (_iota((1, GW), 1) % DQ) < (DQ // 2)
    partner = jnp.where(first_half, pltpu.roll(kn, GW - DQ // 2, axis=1), pltpu.roll(kn, DQ // 2, axis=1))
    kr = kn * cos_ref[...] + partner * sin_ref[...]
    k_out[...] = jnp.dot(kr.astype(BF16), _place_heads_lanes(GW, HEAD_DIM),
                         preferred_element_type=F32).astype(BF16)

    qn = qt * lax.rsqrt(_mm(gmean, qt * qt, NN, "lo") + EPS) * gqt_ref[...]
    first_half_t = (_iota((GW, 1), 0) % DQ) < (DQ // 2)
    partner_t = jnp.where(first_half_t, pltpu.roll(qn, GW - DQ // 2, axis=0), pltpu.roll(qn, DQ // 2, axis=0))
    qr = (qn * cost_ref[...] + partner_t * sint_ref[...]).astype(BF16)
    qs = _slab_rows(qr, jnp.zeros((SLAB - HEAD_DIM, tm), BF16))
    vb = _slab_rows(vt.astype(BF16),
                    jnp.where(_iota((VROWS - HEAD_DIM, tm), 0) == 0, 1.0, 0.0).astype(BF16))
    for c in range(tm // tile):
        q_out[c] = qs[:, c * tile:(c + 1) * tile]
        v_out[c] = vb[:, c * tile:(c + 1) * tile]


def _diff_prep(h, wk, wqt, wvt, gk, gqt, cos, sin, cost, sint, *, seq, tm, tile):
    T, D = h.shape
    per_seq = seq // tm
    nt = tm // tile
    const = lambda i: (0, 0)
    return pl.pallas_call(
        functools.partial(_diff_prep_kernel, tile=tile),
        grid=(T // tm,),
        in_specs=[
            pl.BlockSpec((tm, D), lambda i: (i, 0)),
            pl.BlockSpec((D, GW), const),
            pl.BlockSpec((GW, D), const),
            pl.BlockSpec((GW, D), const),
            pl.BlockSpec((1, GW), const),
            pl.BlockSpec((GW, tm), const),
            pl.BlockSpec((tm, GW), lambda i: (i % per_seq, 0)),
            pl.BlockSpec((tm, GW), lambda i: (i % per_seq, 0)),
            pl.BlockSpec((GW, tm), lambda i: (0, i % per_seq)),
            pl.BlockSpec((GW, tm), lambda i: (0, i % per_seq)),
        ],
        out_specs=[
            pl.BlockSpec((nt, HG * SLAB, tile), lambda i: (i, 0, 0)),
            pl.BlockSpec((tm, HG * SLAB), lambda i: (i, 0)),
            pl.BlockSpec((nt, HG * VROWS, tile), lambda i: (i, 0, 0)),
        ],
        out_shape=[
            jax.ShapeDtypeStruct((T // tile, HG * SLAB, tile), BF16),
            jax.ShapeDtypeStruct((T, HG * SLAB), BF16),
            jax.ShapeDtypeStruct((T // tile, HG * VROWS, tile), BF16),
        ],
        compiler_params=_params(("arbitrary",)),
        name="diff_prep",
    )(h, wk, wqt, wvt, gk, gqt, cos, sin, cost, sint)


def _fox_prep_kernel(h_ref, wk_ref, wf_ref, wqt_ref, wvt_ref, gk_ref, gqt_ref, fb_ref,
                     q_out, k_out, v_out, fcarry_ref, *, tile):
    tm = h_ref.shape[0]

    @pl.when(pl.program_id(1) == 0)
    def _():
        fcarry_ref[...] = jnp.zeros_like(fcarry_ref)

    hb = h_ref[...]
    k = jnp.dot(hb, wk_ref[...], preferred_element_type=F32)
    fl = jnp.dot(hb, wf_ref[...], preferred_element_type=F32)
    qt = lax.dot_general(wqt_ref[...], hb, NT, preferred_element_type=F32)
    vt = lax.dot_general(wvt_ref[...], hb, NT, preferred_element_type=F32)
    gmean = _group_matrix(GW, HEAD_DIM, 1.0 / HEAD_DIM)

    kn = (k * lax.rsqrt(_mm(k * k, gmean, NN, "lo") + EPS) * gk_ref[...]).astype(BF16)

    log_f = -_softplus(-(fl + fb_ref[...]))
    cb = min(tm, LANES)
    l_incl = jnp.where(_iota((cb, cb), 0) >= _iota((cb, cb), 1), 1.0, 0.0).astype(F32)
    carry = fcarry_ref[...]
    blocks = []
    for r in range(tm // cb):
        blocks.append(_mm_exact(l_incl, log_f[r * cb:(r + 1) * cb], left=False) + carry)
        carry = blocks[-1][cb - 1:cb]
    fcarry_ref[...] = carry
    nf = jnp.concatenate(blocks, axis=0) * (-LOG2E)
    head_lane = _iota((1, LANES), 1) < HG
    packed = jnp.zeros_like(nf)
    for i in range(3):
        p = jnp.where(head_lane, nf.astype(BF16).astype(F32), 0.0)
        packed = packed + (p if i == 0 else pltpu.roll(p, HG * i, axis=1))
        nf = nf - p
    src = _iota((LANES, HG * SLAB), 0)
    dst = _iota((LANES, HG * SLAB), 1)
    place = jnp.where((src < 3 * HG) & (dst == (src % HG) * SLAB + HEAD_DIM + src // HG), 1.0, 0.0)
    ks = (jnp.dot(kn, _place_heads_lanes(GW, HEAD_DIM), preferred_element_type=F32)
          + jnp.dot(packed.astype(BF16), place.astype(BF16), preferred_element_type=F32))
    k_out[...] = ks.astype(BF16)

    qn = (qt * lax.rsqrt(_mm(gmean, qt * qt, NN, "lo") + EPS) * gqt_ref[...]).astype(BF16)
    ones_rows = jnp.where(_iota((SLAB - HEAD_DIM, tm), 0) < 3, 1.0, 0.0).astype(BF16)
    qs = _slab_rows(qn, ones_rows)
    vb = _slab_rows(vt.astype(BF16),
                    jnp.where(_iota((VROWS - HEAD_DIM, tm), 0) == 0, 1.0, 0.0).astype(BF16))
    for c in range(tm // tile):
        q_out[c] = qs[:, c * tile:(c + 1) * tile]
        v_out[c] = vb[:, c * tile:(c + 1) * tile]


def _fox_prep(h, wk, wf, wqt, wvt, gk, gqt, fb, *, batch, seq, tm, tile):
    T, D = h.shape
    per_seq = seq // tm
    nt = tm // tile
    const = lambda b, i: (0, 0)
    row = lambda b, i: (b * per_seq + i, 0)
    return pl.pallas_call(
        functools.partial(_fox_prep_kernel, tile=tile),
        grid=(batch, per_seq),
        in_specs=[
            pl.BlockSpec((tm, D), row),
            pl.BlockSpec((D, GW), const),
            pl.BlockSpec((D, LANES), const),
            pl.BlockSpec((GW, D), const),
            pl.BlockSpec((GW, D), const),
            pl.BlockSpec((1, GW), const),
            pl.BlockSpec((GW, tm), const),
            pl.BlockSpec((1, LANES), const),
        ],
        out_specs=[
            pl.BlockSpec((nt, HG * SLAB, tile), lambda b, i: (b * per_seq + i, 0, 0)),
            pl.BlockSpec((tm, HG * SLAB), row),
            pl.BlockSpec((nt, HG * VROWS, tile), lambda b, i: (b * per_seq + i, 0, 0)),
        ],
        out_shape=[
            jax.ShapeDtypeStruct((T // tile, HG * SLAB, tile), BF16),
            jax.ShapeDtypeStruct((T, HG * SLAB), BF16),
            jax.ShapeDtypeStruct((T // tile, HG * VROWS, tile), BF16),
        ],
        scratch_shapes=[pltpu.VMEM((1, LANES), F32)],
        compiler_params=_params(("arbitrary", "arbitrary")),
        name="fox_prep",
    )(h, wk, wf, wqt, wvt, gk, gqt, fb)


def _attn_kernel(q_ref, k_ref, v_ref, aux_ref, gain_ref, o_ref, qz_s, m_s, acc_s,
                 s_a, s_b, p_a, p_b, al_a, al_b, mt_a, mt_b, *, n_maps, tile, lambda_init):
    qi = pl.program_id(1)
    row = _iota((tile, tile), 0)
    col = _iota((tile, tile), 1)
    if n_maps == 2:
        diag_ok = (row // ATT_CHUNK) <= (col // ATT_CHUNK)
    else:
        diag_ok = row <= col

    slab_row = _iota((SLAB, 1), 0)
    for h in range(HG):
        qh = q_ref[h * SLAB:(h + 1) * SLAB, :]
        if n_maps == 2:
            qz_s[h, :, 0:tile] = jnp.where(slab_row < DQ, qh, jnp.zeros_like(qh))
            qz_s[h, :, tile:2 * tile] = jnp.where((slab_row >= DQ) & (slab_row < 2 * DQ), qh,
                                                  jnp.zeros_like(qh))
        else:
            qz_s[h] = qh
    m_s[...] = jnp.full(m_s.shape, NEG_INF, F32)
    acc_s[...] = jnp.zeros(acc_s.shape, F32)
    p_b[...] = jnp.zeros(p_b.shape, BF16)
    al_b[...] = jnp.zeros(al_b.shape, F32)

    buf_a = (s_a, p_a, al_a, mt_a)
    buf_b = (s_b, p_b, al_b, mt_b)

    all_heads = tuple(range(HG))

    def scores(j, s_buf, mt_buf, heads=all_heads):
        for h in heads:
            kj = k_ref[pl.ds(pl.multiple_of(j * tile, tile), tile), h * SLAB:(h + 1) * SLAB]
            s = jnp.dot(kj, qz_s[h], preferred_element_type=F32)
            s_buf[h] = s
            for mp in range(n_maps):
                c = h * n_maps + mp
                mt_buf[c:c + 1, :] = jnp.max(s[:, mp * tile:(mp + 1) * tile], axis=0, keepdims=True)

    def value_update(j, p_buf, al_buf, heads=all_heads):
        for h in heads:
            vj = v_ref[j, h * VROWS:(h + 1) * VROWS, :]
            for mp in range(n_maps):
                c = h * n_maps + mp
                rows = slice(c * VROWS, (c + 1) * VROWS)
                pv = jnp.dot(vj, p_buf[c], preferred_element_type=F32)
                acc_s[rows, :] = al_buf[c:c + 1, :] * acc_s[rows, :] + pv

    def softmax(s_buf, p_buf, al_buf, mt_buf, masked, heads=all_heads):
        for h in heads:
            for mp in range(n_maps):
                c = h * n_maps + mp
                s = s_buf[h, :, mp * tile:(mp + 1) * tile]
                if masked:
                    s = jnp.where(diag_ok, s, NEG_INF)
                    tile_max = jnp.max(s, axis=0, keepdims=True)
                else:
                    tile_max = mt_buf[c:c + 1, :]
                m_old = m_s[c:c + 1, :]
                m_new = jnp.maximum(m_old, tile_max)
                m_s[c:c + 1, :] = m_new
                al_buf[c:c + 1, :] = jnp.exp2(m_old - m_new)
                p_buf[c] = jnp.exp2(s - m_new).astype(BF16)

    def step(j, cur, nxt):
        for h in all_heads:
            scores(j + 1, nxt[0], nxt[3], (h,))
            value_update(jnp.maximum(j - 1, 0), nxt[1], nxt[2], (h,))
            softmax(cur[0], cur[1], cur[2], cur[3], False, (h,))

    if n_maps == 2:
        aux = aux_ref[...]
        lam = (jnp.exp(jnp.sum(aux[0:1] * aux[1:2], axis=1, keepdims=True))
               - jnp.exp(jnp.sum(aux[2:3] * aux[3:4], axis=1, keepdims=True)) + lambda_init)

    def normalized(c):
        return (acc_s[c * VROWS:c * VROWS + HEAD_DIM, :]
                / acc_s[c * VROWS + HEAD_DIM:c * VROWS + HEAD_DIM + 1, :])

    def last(j, cur, nxt):
        value_update(jnp.maximum(j - 1, 0), nxt[1], nxt[2])
        softmax(cur[0], cur[1], cur[2], cur[3], True)
        outs = []
        for h in all_heads:
            value_update(j, cur[1], cur[2], (h,))
            o = normalized(h * n_maps)
            if n_maps == 2:
                o = o - lam * normalized(h * n_maps + 1)
                o = o * lax.rsqrt(jnp.mean(o * o, axis=0, keepdims=True) + EPS)
            outs.append(o)
        ot = jnp.concatenate(outs, axis=0)
        if n_maps == 2:
            ot = ot * gain_ref[...]
        o_ref[...] = ot.T.astype(BF16)

    scores(0, s_a, mt_a)

    def pair(j):
        step(j, buf_a, buf_b)
        step(j + 1, buf_b, buf_a)

    pairs_per_trip = 2 if n_maps == 1 else 1
    shift = pairs_per_trip.bit_length()

    def trip(jj, carry):
        for t in range(pairs_per_trip):
            pair(2 * pairs_per_trip * jj + 2 * t)
        return carry

    lax.fori_loop(0, lax.shift_right_logical(qi, shift), trip, 0)
    odd = (qi & 1) == 1

    if pairs_per_trip == 2:
        @pl.when((qi & 2) == 2)
        def _():
            pair(qi - (qi & 3))

    @pl.when(odd)
    def _():
        step(qi - 1, buf_a, buf_b)
        last(qi, buf_b, buf_a)

    @pl.when(jnp.logical_not(odd))
    def _():
        last(qi, buf_a, buf_b)


def _attention(q3, k2, v3, aux, gain, *, batch, seq, tile, n_maps, lambda_init):
    nq = seq // tile
    T = batch * seq
    n_chains = HG * n_maps
    scratch = [pltpu.VMEM((HG, SLAB, n_maps * tile), BF16),
               pltpu.VMEM((n_chains, tile), F32),
               pltpu.VMEM((n_chains * VROWS, tile), F32),
               pltpu.VMEM((HG, tile, n_maps * tile), F32),
               pltpu.VMEM((HG, tile, n_maps * tile), F32),
               pltpu.VMEM((n_chains, tile, tile), BF16),
               pltpu.VMEM((n_chains, tile, tile), BF16)] + [pltpu.VMEM((n_chains, tile), F32)] * 4
    return pl.pallas_call(
        functools.partial(_attn_kernel, n_maps=n_maps, tile=tile, lambda_init=lambda_init),
        grid=(batch, nq),
        in_specs=[
            pl.BlockSpec((None, HG * SLAB, tile), lambda b, i: (b * nq + i, 0, 0)),
            pl.BlockSpec((seq, HG * SLAB), lambda b, i: (b, 0)),
            pl.BlockSpec((nq, HG * VROWS, tile), lambda b, i: (b, 0, 0)),
            pl.BlockSpec(aux.shape, lambda b, i: (0, 0)),
            pl.BlockSpec(gain.shape, lambda b, i: (0, 0)),
        ],
        out_specs=pl.BlockSpec((tile, GW), lambda b, i: (b * nq + i, 0)),
        out_shape=jax.ShapeDtypeStruct((T, GW), BF16),
        scratch_shapes=scratch,
        compiler_params=_params(("arbitrary", "arbitrary")),
        name="diff_attention" if n_maps == 2 else "forgetting_attention",
    )(q3, k2, v3, aux, gain)


def _erf_gelu(x):
    return 0.5 * x * (1.0 + lax.erf(x * (1.0 / math.sqrt(2.0))))


def _gmlp_kernel(h_ref, wc_ref, sgw_ref, bias_ref, ln_ref, o_ref):
    tm = h_ref.shape[0]
    pc = _erf_gelu(jnp.dot(h_ref[...], wc_ref[...], preferred_element_type=F32))
    u = pc[:, 0:GW]
    v = pc[:, GW:2 * GW]
    mu = jnp.mean(v, axis=-1, keepdims=True)
    vc = v - mu
    var = jnp.mean(vc * vc, axis=-1, keepdims=True)
    ln = ln_ref[...]
    vn = (vc * lax.rsqrt(var + EPS) * ln[0:1] + ln[1:2]).astype(BF16)

    causal = _iota((SG_CHUNK, SG_CHUNK), 0) >= _iota((SG_CHUNK, SG_CHUNK), 1)
    w = jnp.concatenate([jnp.where(causal, sgw_ref[g], 0.0) for g in range(HG)], axis=0).astype(BF16)
    lane_head = _iota((1, GW), 1) // HEAD_DIM
    bias = bias_ref[...]
    for n in range(tm // SG_CHUNK):
        rows = slice(n * SG_CHUNK, (n + 1) * SG_CHUNK)
        full = jnp.dot(w, vn[rows], preferred_element_type=F32)
        sv = bias
        for g in range(HG):
            sv = sv + jnp.where(lane_head == g, full[g * SG_CHUNK:(g + 1) * SG_CHUNK], 0.0)
        o_ref[rows, :] = (u[rows] * sv).astype(BF16)


def _gmlp_mixer(h, wc, sg_w, bias, ln, *, tm):
    T, D = h.shape
    return pl.pallas_call(
        _gmlp_kernel,
        grid=(T // tm,),
        in_specs=[
            pl.BlockSpec((tm, D), lambda i: (i, 0)),
            pl.BlockSpec((D, 2 * GW), lambda i: (0, 0)),
            pl.BlockSpec((HG, SG_CHUNK, SG_CHUNK), lambda i: (0, 0, 0)),
            pl.BlockSpec((SG_CHUNK, GW), lambda i: (0, 0)),
            pl.BlockSpec((2, GW), lambda i: (0, 0)),
        ],
        out_specs=pl.BlockSpec((tm, GW), lambda i: (i, 0)),
        out_shape=jax.ShapeDtypeStruct((T, GW), BF16),
        compiler_params=_params(("arbitrary",)),
        name="gmlp_mixer",
    )(h, wc, sg_w, bias, ln)


HALO = 16


FFN_COLS = 256


def _ffn_kernel(yah, ybh, ych, ydh, ya, yb, yc, yd, xh_ref, x_ref, wo_ref, g2_ref,
                wup_ref, cw_ref, cb_ref, wdn_ref, mod_ref, modn_ref, gn_ref,
                o_ref, hn_ref, act_ref, *, per_seq):
    i = pl.program_id(0)
    tm = x_ref.shape[0]
    dff = wdn_ref.shape[0]
    mod = mod_ref[...]

    proj = None
    for m, (halo, main) in enumerate(((yah, ya), (ybh, yb), (ych, yc), (ydh, yd))):
        ym = jnp.concatenate([halo[...], main[...]], axis=0)
        part = jnp.dot(ym, wo_ref[m], preferred_element_type=F32)
        proj = part if proj is None else proj + part
    x_mid = jnp.concatenate([xh_ref[...], x_ref[...]], axis=0) + mod[2:3] * proj
    hx = _modulated_norm(x_mid, g2_ref[...], mod[3:4], mod[4:5]).astype(BF16)
    zero_halo = (_iota((HALO + tm, 1), 0) < HALO) & ((i % per_seq) == 0)

    def conv_half(cols):
        u = jnp.dot(hx, wup_ref[:, cols], preferred_element_type=F32)
        u = jnp.where(zero_halo, 0.0, u)
        cw = cw_ref[:, cols]
        y = (cw[0:1] * pltpu.roll(u, 2, axis=0) + cw[1:2] * pltpu.roll(u, 1, axis=0) + cw[2:3] * u)
        return y[HALO:] + cb_ref[:, cols]

    for j in range(dff // FFN_COLS):
        a = conv_half(slice(j * FFN_COLS, (j + 1) * FFN_COLS))
        g = conv_half(slice(dff + j * FFN_COLS, dff + (j + 1) * FFN_COLS))
        act_ref[:, j * FFN_COLS:(j + 1) * FFN_COLS] = (a * (g * _sigmoid(g))).astype(BF16)

    y = jnp.dot(act_ref[...], wdn_ref[...], preferred_element_type=F32)
    x = x_mid[HALO:] + mod[5:6] * y
    o_ref[...] = x
    modn = modn_ref[...]
    hn_ref[...] = _modulated_norm(x, gn_ref[...], modn[0:1], modn[1:2]).astype(BF16)


def _ffn(ys, x2, w_out4, g2, wup, cw, cb, wdn, mod_l, mod_next, g_next, *, seq, tm):
    T, D = x2.shape
    dff = wdn.shape[0]
    per_seq = seq // tm
    halo_blocks = tm // HALO
    once = pl.Buffered(1)
    row = lambda i: (i, 0)
    halo = lambda i: (jnp.maximum(i * halo_blocks - 1, 0), 0)
    return pl.pallas_call(
        functools.partial(_ffn_kernel, per_seq=per_seq),
        grid=(T // tm,),
        in_specs=[pl.BlockSpec((HALO, GW), halo)] * N_MIXERS + [pl.BlockSpec((tm, GW), row)] * N_MIXERS + [
            pl.BlockSpec((HALO, D), halo),
            pl.BlockSpec((tm, D), row),
            pl.BlockSpec((N_MIXERS, GW, D), lambda i: (0, 0, 0), pipeline_mode=once),
            pl.BlockSpec((1, D), lambda i: (0, 0)),
            pl.BlockSpec((D, 2 * dff), lambda i: (0, 0), pipeline_mode=once),
            pl.BlockSpec((3, 2 * dff), lambda i: (0, 0), pipeline_mode=once),
            pl.BlockSpec((1, 2 * dff), lambda i: (0, 0), pipeline_mode=once),
            pl.BlockSpec((dff, D), lambda i: (0, 0), pipeline_mode=once),
            pl.BlockSpec((None, 6, D), lambda i: (i // per_seq, 0, 0)),
            pl.BlockSpec((None, 6, D), lambda i: (i // per_seq, 0, 0)),
            pl.BlockSpec((1, D), lambda i: (0, 0)),
        ],
        out_specs=[pl.BlockSpec((tm, D), row), pl.BlockSpec((tm, D), row)],
        out_shape=[jax.ShapeDtypeStruct((T, D), F32), jax.ShapeDtypeStruct((T, D), BF16)],
        scratch_shapes=[pltpu.VMEM((tm, dff), BF16)],
        compiler_params=_params(("arbitrary",)),
        name="conv_glu_ffn",
    )(*ys, *ys, x2, x2, w_out4, g2.reshape(1, D), wup, cw, cb, wdn, mod_l, mod_next, g_next.reshape(1, D))


def _rope_tables(seq):
    inv = 1.0 / (ROPE_THETA ** (jnp.arange(0, DQ, 2, dtype=F32) / DQ))
    ang = jnp.arange(seq, dtype=F32)[:, None] * inv[None, :]
    cos = jnp.cos(ang)
    sin = jnp.sin(ang)
    cos_map = jnp.concatenate([cos, cos], axis=-1)
    sin_map = jnp.concatenate([-sin, sin], axis=-1)
    reps = GW // DQ
    return jnp.tile(cos_map, (1, reps)), jnp.tile(sin_map, (1, reps))


def kernel(x, c, ada_w, ada_b, norm1_g, norm2_g, w_in, w_out, rw_mu, rw_w0, rw_w_up, rw_a0, rw_a_up, rw_g_up, rw_k_k, rw_k_a, rw_r_k, rw_ln_g, rw_ln_b, df_lam_q1, df_lam_k1, df_lam_q2, df_lam_k2, df_q_g, df_k_g, df_sub_g, sg_w, sg_b, sg_ln_g, sg_ln_b, fx_q_g, fx_k_g, fx_f_b, ffn_up, ffn_conv, ffn_conv_b, ffn_down):
    Bn, S, D = x.shape
    L = ada_w.shape[0]
    T = Bn * S
    dff = ffn_down.shape[1]
    tm = min(ROW_TILE, S)
    tp = min(PREP_TILE, S)
    tile = min(ATT_TILE, S)

    mod = _ada_mod(c, ada_w, ada_b).reshape(L, Bn, 6, D)
    cos, sin = _rope_tables(S)
    cost, sint = cos.T, sin.T
    x2 = x.reshape(T, D)

    oa = 0
    ob = oa + RW_COLS
    oc = ob + 3 * GW
    od = oc + 2 * GW

    for l in range(L):
        lambda_init = 0.8 - 0.6 * math.exp(-0.3 * l)
        wl = w_in[l].astype(BF16)
        if l == 0:
            h = _norm_mod(x2, mod[l], norm1_g[l], seq=S, row0=0, tm=tm)

        lora_w = jnp.zeros((LANES, 3 * GW), F32)
        lora_w = lora_w.at[0:RW_DECAY_RANK, 0:GW].set(rw_w_up[l])
        lora_w = lora_w.at[RW_DECAY_RANK:RW_DECAY_RANK + RW_A_RANK, GW:2 * GW].set(rw_a_up[l])
        lora_w = lora_w.at[RW_DECAY_RANK + RW_A_RANK:LANES, 2 * GW:3 * GW].set(rw_g_up[l])
        vec = jnp.stack([rw_w0[l], rw_a0[l], rw_k_k[l], rw_k_a[l], rw_r_k[l].reshape(GW),
                         rw_ln_g[l], rw_ln_b[l], jnp.zeros((GW,), F32)])
        ya = _rwkv_mixer(h, wl[:, oa:ob], rw_mu[l].reshape(1, RW_COLS), lora_w, vec,
                         batch=Bn, seq=S, tt=min(RW_TILE, S), mode=RW_MODE)

        gk = jnp.tile(df_k_g[l], GW // DQ).reshape(1, GW)
        gqt = jnp.broadcast_to((jnp.tile(df_q_g[l], GW // DQ) * (DQ ** -0.5 * LOG2E))[:, None], (GW, tp))
        qb, kb, vb = _diff_prep(h, wl[:, ob + GW:ob + 2 * GW], wl[:, ob:ob + GW].T, wl[:, ob + 2 * GW:oc].T,
                                gk, gqt, cos, sin, cost, sint, seq=S, tm=tp, tile=tile)
        lam_vecs = jnp.stack([df_lam_q1[l], df_lam_k1[l], df_lam_q2[l], df_lam_k2[l]])
        sub_gain = jnp.broadcast_to((jnp.tile(df_sub_g[l], HG) * (1.0 - lambda_init))[:, None], (GW, tile))
        yb = _attention(qb, kb, vb, lam_vecs, sub_gain, batch=Bn, seq=S, tile=tile,
                        n_maps=2, lambda_init=lambda_init)

        sg_bias = jnp.repeat(sg_b[l].T, HEAD_DIM, axis=1)
        yc = _gmlp_mixer(h, wl[:, oc:od], sg_w[l], sg_bias,
                         jnp.stack([sg_ln_g[l], sg_ln_b[l]]), tm=tp)

        wf = jnp.zeros((D, LANES), BF16).at[:, 0:HG].set(wl[:, od + 3 * GW:od + 3 * GW + HG])
        fb = jnp.zeros((1, LANES), F32).at[0, 0:HG].set(fx_f_b[l])
        gkd = jnp.tile(fx_k_g[l], HG).reshape(1, GW)
        gqd = jnp.broadcast_to((jnp.tile(fx_q_g[l], HG) * (HEAD_DIM ** -0.5 * LOG2E))[:, None], (GW, tp))
        qd, kd, vd = _fox_prep(h, wl[:, od + GW:od + 2 * GW], wf, wl[:, od:od + GW].T,
                               wl[:, od + 2 * GW:od + 3 * GW].T, gkd, gqd, fb,
                               batch=Bn, seq=S, tm=tp, tile=tile)
        yd = _attention(qd, kd, vd, jnp.zeros((8, LANES), F32), jnp.zeros((8, LANES), F32),
                        batch=Bn, seq=S, tile=tile, n_maps=1, lambda_init=0.0)

        nl = (l + 1) % L
        x2, h = _ffn((ya, yb, yc, yd), x2, w_out[l].astype(BF16).reshape(N_MIXERS, GW, D), norm2_g[l],
                     ffn_up[l].astype(BF16), ffn_conv[l], ffn_conv_b[l].reshape(1, 2 * dff),
                     ffn_down[l].astype(BF16), mod[l], mod[nl], norm1_g[nl], seq=S, tm=tm)

    return x2.reshape(Bn, S, D)
```

```python
import functools
import math

import jax
import jax.numpy as jnp
from jax import lax
from jax.experimental import pallas as pl
from jax.experimental.pallas import tpu as pltpu

F32 = jnp.float32
BF16 = jnp.bfloat16
HIGHEST = lax.Precision.HIGHEST

N_MIXERS = 4
HEAD_DIM = 64
HG = 4
GW = HG * HEAD_DIM
DQ = HEAD_DIM // 2
RW_DECAY_RANK = 32
RW_A_RANK = 32
RW_GATE_RANK = 64
RW_COLS = 3 * GW + RW_DECAY_RANK + RW_A_RANK + RW_GATE_RANK
RW_GN_EPS = 64e-5
SG_CHUNK = 128
ATT_CHUNK = 64
ROPE_THETA = 10000.0
EPS = 1e-6
LOG2E = math.log2(math.e)
NEG_INF = -1e30

LANES = 128
V7X_VMEM_BYTES = 64 * 1024 * 1024
VMEM_LIMIT = 56 * 1024 * 1024

ROW_TILE = 512
PREP_TILE = 1024
RW_CHUNK = 64
RW_TILE = 512
ATT_TILE = 256
SLAB = 128
VROWS = HEAD_DIM + 16
RW_MODE = "lo"

NN = (((1,), (0,)), ((), ()))
NT = (((1,), (1,)), ((), ()))
TN = (((0,), (0,)), ((), ()))


def _params(sem):
    return pltpu.CompilerParams(dimension_semantics=sem, vmem_limit_bytes=VMEM_LIMIT)


def _mm(a, b, dims=NN, mode="hi"):
    if mode == "hi":
        return lax.dot_general(a.astype(F32), b.astype(F32), dims, precision=HIGHEST,
                               preferred_element_type=F32)
    if mode == "lo":
        return lax.dot_general(a.astype(BF16), b.astype(BF16), dims, preferred_element_type=F32)
    ah = a.astype(BF16)
    al = (a - ah.astype(F32)).astype(BF16)
    bh = b.astype(BF16)
    bl = (b - bh.astype(F32)).astype(BF16)
    dg = functools.partial(lax.dot_general, dimension_numbers=dims, preferred_element_type=F32)
    return dg(ah, bh) + (dg(ah, bl) + dg(al, bh))


def _mm_exact(a, b, dims=NN, left=True):
    x = a if left else b
    xh = x.astype(BF16)
    xl = (x - xh.astype(F32)).astype(BF16)
    dg = functools.partial(lax.dot_general, dimension_numbers=dims, preferred_element_type=F32)
    if left:
        bb = b.astype(BF16)
        return dg(xh, bb) + dg(xl, bb)
    ab = a.astype(BF16)
    return dg(ab, xh) + dg(ab, xl)


def _iota(shape, dim):
    return lax.broadcasted_iota(jnp.int32, shape, dim)


def _group_matrix(n, group, value=1.0):
    same = (_iota((n, n), 0) // group) == (_iota((n, n), 1) // group)
    return jnp.where(same, value, 0.0).astype(F32)


def _softplus(x):
    return jnp.maximum(x, 0.0) + jnp.log1p(jnp.exp(-jnp.abs(x)))


def _sigmoid(x):
    return 1.0 / (1.0 + jnp.exp(-x))


def _ada_kernel(c_ref, w_ref, b_ref, o_ref):
    c = c_ref[...]
    cond = c * _sigmoid(c)
    o_ref[...] = _mm(cond, w_ref[...], NN, "hi") + b_ref[...]


def _ada_mod(c, ada_w, ada_b):
    L, D, D6 = ada_w.shape
    Bn = c.shape[0]
    tn = D6 // 4
    return pl.pallas_call(
        _ada_kernel,
        grid=(L, D6 // tn),
        in_specs=[
            pl.BlockSpec((Bn, D), lambda l, j: (0, 0)),
            pl.BlockSpec((None, D, tn), lambda l, j: (l, 0, j)),
            pl.BlockSpec((None, 1, tn), lambda l, j: (l, 0, j)),
        ],
        out_specs=pl.BlockSpec((None, Bn, tn), lambda l, j: (l, 0, j)),
        out_shape=jax.ShapeDtypeStruct((L, Bn, D6), F32),
        compiler_params=_params(("arbitrary", "arbitrary")),
        name="ada_mod",
    )(c, ada_w, ada_b.reshape(L, 1, D6))


def _modulated_norm(x, g, shift, scale):
    y = x * lax.rsqrt(jnp.mean(x * x, axis=-1, keepdims=True) + EPS) * g
    return y * (1.0 + scale) + shift


def _norm_kernel(x_ref, mod_ref, g_ref, o_ref, *, row0):
    mod = mod_ref[...]
    h = _modulated_norm(x_ref[...], g_ref[...], mod[row0:row0 + 1], mod[row0 + 1:row0 + 2])
    o_ref[...] = h.astype(BF16)


def _norm_mod(x2, mod_l, g, *, seq, row0, tm):
    T, D = x2.shape
    per_seq = seq // tm
    return pl.pallas_call(
        functools.partial(_norm_kernel, row0=row0),
        grid=(T // tm,),
        in_specs=[
            pl.BlockSpec((tm, D), lambda i: (i, 0)),
            pl.BlockSpec((None, 6, D), lambda i: (i // per_seq, 0, 0)),
            pl.BlockSpec((1, D), lambda i: (0, 0)),
        ],
        out_specs=pl.BlockSpec((tm, D), lambda i: (i, 0)),
        out_shape=jax.ShapeDtypeStruct((T, D), BF16),
        compiler_params=_params(("arbitrary",)),
        name="norm_mod",
    )(x2, mod_l, g.reshape(1, D))


def _blockdiag(mp, head_lane_masks):
    return jnp.concatenate([jnp.where(m, mp, 0.0) for m in head_lane_masks], axis=0)


def _head_blocks(full, head_lane_masks, C):
    out = jnp.where(head_lane_masks[0], full[0:C], 0.0)
    for h in range(1, HG):
        out = out + jnp.where(head_lane_masks[h], full[h * C:(h + 1) * C], 0.0)
    return out


def _apply_packed(mp, rhs, head_lane_masks, C, mode):
    return _head_blocks(_mm(mp, rhs, TN, mode), head_lane_masks, C)


def _rwkv_kernel(h_ref, wa_ref, mu_ref, lora_ref, vec_ref, o_ref,
                 carry_ref, st_ref, hs_ref, *, C, mode):
    TT = h_ref.shape[0]

    @pl.when(pl.program_id(1) == 0)
    def _():
        carry_ref[...] = jnp.zeros_like(carry_ref)
        st_ref[...] = jnp.zeros_like(st_ref)

    vec = vec_ref[...]
    w0, a0, k_k, k_a, r_k, ln_g, ln_b = (vec[i:i + 1] for i in range(7))

    pa = jnp.dot(h_ref[...], wa_ref[...], preferred_element_type=F32)
    prev = pltpu.roll(pa, 1, axis=0)
    prev = jnp.where(_iota((TT, 1), 0) == 0, carry_ref[...], prev)
    carry_ref[...] = pa[TT - 1:TT]
    pa = pa + (prev - pa) * mu_ref[...]

    r = pa[:, 0:GW]
    k = pa[:, GW:2 * GW]
    v = pa[:, 2 * GW:3 * GW]
    lo = pa[:, 3 * GW:3 * GW + LANES]
    lane = _iota((1, LANES), 1)
    act = jnp.where(lane < RW_DECAY_RANK, jnp.tanh(lo),
                    jnp.where(lane < RW_DECAY_RANK + RW_A_RANK, lo, _sigmoid(lo)))
    lora = _mm_exact(act, lora_ref[...])
    w_log = -_softplus(-(w0 + lora[:, 0:GW])) - 0.5
    lw = -jnp.exp(w_log)
    a = _sigmoid(a0 + lora[:, GW:2 * GW])
    gate = lora[:, 2 * GW:3 * GW]

    gsum = _group_matrix(GW, HEAD_DIM)
    kk = k * k_k
    kk = kk / jnp.maximum(jnp.sqrt(_mm(kk * kk, gsum, NN, "lo")), 1e-12)
    k2 = k * (1.0 + (a - 1.0) * k_a)
    bonus = _mm(r * k2 * r_k, gsum, NN, "lo") * v

    aa = -kk
    b = kk * a

    lane_head = _iota((1, GW), 1) // HEAD_DIM
    hmask = [lane_head == h for h in range(HG)]
    s_idx = _iota((C, GW), 0)
    t_idx = _iota((C, GW), 1) % C
    strict = s_idx < t_idx
    incl = s_idx <= t_idx
    eye_p = jnp.where(s_idx == t_idx, 1.0, 0.0).astype(F32)
    l_incl = jnp.where(_iota((C, C), 0) >= _iota((C, C), 1), 1.0, 0.0).astype(F32)
    bd = (_iota((GW, GW), 0) // HEAD_DIM) == (_iota((GW, GW), 1) // HEAD_DIM)
    eye_k = jnp.where(_iota((GW, GW), 0) == _iota((GW, GW), 1), 1.0, 0.0).astype(F32)
    chunks = range(TT // C)
    sl = lambda x, c: x[c * C:(c + 1) * C]

    cl = jnp.concatenate([_mm_exact(l_incl, sl(lw, c), left=False) for c in chunks], axis=0)
    g_in = jnp.exp(cl)
    g_inv = jnp.exp(-cl)
    at = aa * jnp.exp(cl - lw)
    rt = r * g_in
    bt = b * g_inv
    kt = k2 * g_inv

    bks = [jnp.concatenate([sl(bt, c), sl(kt, c)], axis=0) for c in chunks]
    xs = [_mm(bks[c], jnp.concatenate([_blockdiag(sl(at, c), hmask), _blockdiag(sl(rt, c), hmask)], axis=0),
              NT, mode) for c in chunks]
    n_p = [jnp.where(strict, x[0:C, 0:GW], 0.0) for x in xs]
    ak_p = [jnp.where(strict, x[C:2 * C, 0:GW], 0.0) for x in xs]
    rbk_p = [jnp.concatenate([jnp.where(incl, x[0:C, GW:2 * GW], 0.0),
                              jnp.where(incl, x[C:2 * C, GW:2 * GW], 0.0)], axis=0) for x in xs]

    p_p = [eye_p + n for n in n_p]
    q_p = n_p
    bd_dtype = BF16 if mode == "lo" else F32
    q_bd = [_blockdiag(q, hmask).astype(bd_dtype) for q in q_p]
    for _ in range(int(math.log2(C)) - 1):
        q_p = [_mm(q, b, NN, mode) for q, b in zip(q_p, q_bd)]
        q_bd = [_blockdiag(q, hmask).astype(bd_dtype) for q in q_p]
        p_p = [p + _mm(p, b, NN, mode) for p, b in zip(p_p, q_bd)]

    wv = [_apply_packed(ak_p[c], sl(v, c), hmask, C, mode) for c in chunks]
    ta_tw = [_mm(p_p[c], jnp.concatenate([sl(at, c), wv[c]], axis=1), TN, mode) for c in chunks]
    ta = [_head_blocks(f[:, 0:GW], hmask, C) for f in ta_tw]
    tw = [_head_blocks(f[:, GW:2 * GW], hmask, C) for f in ta_tw]
    g_end = [g_in[(c + 1) * C - 1:(c + 1) * C] for c in chunks]
    m_t = [(eye_k + jnp.where(bd, _mm(ta[c], sl(bt, c), TN, mode), 0.0)) * g_end[c] for c in chunks]
    g_t = [jnp.where(bd, _mm(jnp.concatenate([tw[c], sl(v, c)], axis=0), bks[c], TN, mode), 0.0) * g_end[c]
           for c in chunks]

    ht = st_ref[...]
    for c in chunks:
        hs_ref[c] = ht
        ht = _mm(ht, m_t[c], NN, mode) + g_t[c]
    st_ref[...] = ht

    ys = []
    for c in chunks:
        sh = _mm(jnp.concatenate([ta[c], sl(rt, c)], axis=0), hs_ref[c], NT, mode)
        uv = jnp.concatenate([sh[0:C] + tw[c], sl(v, c)], axis=0)
        ys.append(sh[C:2 * C] + _apply_packed(rbk_p[c], uv, hmask, C, mode))
    y = jnp.concatenate(ys, axis=0)

    gmean = _group_matrix(GW, HEAD_DIM, 1.0 / HEAD_DIM)
    yc = y - _mm(y, gmean, NN, "lo")
    yn = yc * lax.rsqrt(_mm(yc * yc, gmean, NN, "lo") + RW_GN_EPS) * ln_g + ln_b
    o_ref[...] = ((yn + bonus) * gate).astype(BF16)


def _rwkv_mixer(h, wa, mu, lora_w, vec, *, batch, seq, tt, mode):
    T, D = h.shape
    per_seq = seq // tt
    C = RW_CHUNK
    return pl.pallas_call(
        functools.partial(_rwkv_kernel, C=C, mode=mode),
        grid=(batch, per_seq),
        in_specs=[
            pl.BlockSpec((tt, D), lambda b, i: (b * per_seq + i, 0)),
            pl.BlockSpec((D, RW_COLS), lambda b, i: (0, 0)),
            pl.BlockSpec((1, RW_COLS), lambda b, i: (0, 0)),
            pl.BlockSpec((LANES, 3 * GW), lambda b, i: (0, 0)),
            pl.BlockSpec((8, GW), lambda b, i: (0, 0)),
        ],
        out_specs=pl.BlockSpec((tt, GW), lambda b, i: (b * per_seq + i, 0)),
        out_shape=jax.ShapeDtypeStruct((T, GW), BF16),
        scratch_shapes=[pltpu.VMEM((1, RW_COLS), F32), pltpu.VMEM((GW, GW), F32),
                        pltpu.VMEM((tt // C, GW, GW), F32)],
        compiler_params=_params(("arbitrary", "arbitrary")),
        name="rwkv7_mixer",
    )(h, wa, mu, lora_w, vec)


def _place_heads_lanes(n_in, width):
    src = _iota((n_in, HG * SLAB), 0)
    dst = _iota((n_in, HG * SLAB), 1)
    ok = ((dst // SLAB) == (src // width)) & ((dst % SLAB) == (src % width)) & ((dst % SLAB) < width)
    return jnp.where(ok, 1.0, 0.0).astype(BF16)


def _slab_rows(xt, extra):
    parts = []
    for h in range(HG):
        parts.append(xt[h * HEAD_DIM:(h + 1) * HEAD_DIM])
        parts.append(extra)
    return jnp.concatenate(parts, axis=0)


def _diff_prep_kernel(h_ref, wk_ref, wqt_ref, wvt_ref, gk_ref, gqt_ref,
                      cos_ref, sin_ref, cost_ref, sint_ref, q_out, k_out, v_out, *, tile):
    tm = h_ref.shape[0]
    hb = h_ref[...]
    k = jnp.dot(hb, wk_ref[...], preferred_element_type=F32)
    qt = lax.dot_general(wqt_ref[...], hb, NT, preferred_element_type=F32)
    vt = lax.dot_general(wvt_ref[...], hb, NT, preferred_element_type=F32)
    gmean = _group_matrix(GW, DQ, 1.0 / DQ)

    kn = k * lax.rsqrt(_mm(k * k, gmean, NN, "lo") + EPS) * gk_ref[...]
    first_half = (_iota((1, GW), 1) % DQ) < (DQ // 2)
    partner = jnp.where(first_half, pltpu.roll(kn, GW - DQ // 2, axis=1), pltpu.roll(kn, DQ // 2, axis=1))
    kr = kn * cos_ref[...] + partner * sin_ref[...]
    k_out[...] = jnp.dot(kr.astype(BF16), _place_heads_lanes(GW, HEAD_DIM),
                         preferred_element_type=F32).astype(BF16)

    qn = qt * lax.rsqrt(_mm(gmean, qt * qt, NN, "lo") + EPS) * gqt_ref[...]
    first_half_t = (_iota((GW, 1), 0) % DQ) < (DQ // 2)
    partner_t = jnp.where(first_half_t, pltpu.roll(qn, GW - DQ // 2, axis=0), pltpu.roll(qn, DQ // 2, axis=0))
    qr = (qn * cost_ref[...] + partner_t * sint_ref[...]).astype(BF16)
    qs = _slab_rows(qr, jnp.zeros((SLAB - HEAD_DIM, tm), BF16))
    vb = _slab_rows(vt.astype(BF16),
                    jnp.where(_iota((VROWS - HEAD_DIM, tm), 0) == 0, 1.0, 0.0).astype(BF16))
    for c in range(tm // tile):
        q_out[c] = qs[:, c * tile:(c + 1) * tile]
        v_out[c] = vb[:, c * tile:(c + 1) * tile]


def _diff_prep(h, wk, wqt, wvt, gk, gqt, cos, sin, cost, sint, *, seq, tm, tile):
    T, D = h.shape
    per_seq = seq // tm
    nt = tm // tile
    const = lambda i: (0, 0)
    return pl.pallas_call(
        functools.partial(_diff_prep_kernel, tile=tile),
        grid=(T // tm,),
        in_specs=[
            pl.BlockSpec((tm, D), lambda i: (i, 0)),
            pl.BlockSpec((D, GW), const),
            pl.BlockSpec((GW, D), const),
            pl.BlockSpec((GW, D), const),
            pl.BlockSpec((1, GW), const),
            pl.BlockSpec((GW, tm), const),
            pl.BlockSpec((tm, GW), lambda i: (i % per_seq, 0)),
            pl.BlockSpec((tm, GW), lambda i: (i % per_seq, 0)),
            pl.BlockSpec((GW, tm), lambda i: (0, i % per_seq)),
            pl.BlockSpec((GW, tm), lambda i: (0, i % per_seq)),
        ],
        out_specs=[
            pl.BlockSpec((nt, HG * SLAB, tile), lambda i: (i, 0, 0)),
            pl.BlockSpec((tm, HG * SLAB), lambda i: (i, 0)),
            pl.BlockSpec((nt, HG * VROWS, tile), lambda i: (i, 0, 0)),
        ],
        out_shape=[
            jax.ShapeDtypeStruct((T // tile, HG * SLAB, tile), BF16),
            jax.ShapeDtypeStruct((T, HG * SLAB), BF16),
            jax.ShapeDtypeStruct((T // tile, HG * VROWS, tile), BF16),
        ],
        compiler_params=_params(("arbitrary",)),
        name="diff_prep",
    )(h, wk, wqt, wvt, gk, gqt, cos, sin, cost, sint)


def _fox_prep_kernel(h_ref, wk_ref, wf_ref, wqt_ref, wvt_ref, gk_ref, gqt_ref, fb_ref,
                     q_out, k_out, v_out, fcarry_ref, *, tile):
    tm = h_ref.shape[0]

    @pl.when(pl.program_id(1) == 0)
    def _():
        fcarry_ref[...] = jnp.zeros_like(fcarry_ref)

    hb = h_ref[...]
    k = jnp.dot(hb, wk_ref[...], preferred_element_type=F32)
    fl = jnp.dot(hb, wf_ref[...], preferred_element_type=F32)
    qt = lax.dot_general(wqt_ref[...], hb, NT, preferred_element_type=F32)
    vt = lax.dot_general(wvt_ref[...], hb, NT, preferred_element_type=F32)
    gmean = _group_matrix(GW, HEAD_DIM, 1.0 / HEAD_DIM)

    kn = (k * lax.rsqrt(_mm(k * k, gmean, NN, "lo") + EPS) * gk_ref[...]).astype(BF16)

    log_f = -_softplus(-(fl + fb_ref[...]))
    cb = min(tm, LANES)
    l_incl = jnp.where(_iota((cb, cb), 0) >= _iota((cb, cb), 1), 1.0, 0.0).astype(F32)
    carry = fcarry_ref[...]
    blocks = []
    for r in range(tm // cb):
        blocks.append(_mm_exact(l_incl, log_f[r * cb:(r + 1) * cb], left=False) + carry)
        carry = blocks[-1][cb - 1:cb]
    fcarry_ref[...] = carry
    nf = jnp.concatenate(blocks, axis=0) * (-LOG2E)
    head_lane = _iota((1, LANES), 1) < HG
    packed = jnp.zeros_like(nf)
    for i in range(3):
        p = jnp.where(head_lane, nf.astype(BF16).astype(F32), 0.0)
        packed = packed + (p if i == 0 else pltpu.roll(p, HG * i, axis=1))
        nf = nf - p
    src = _iota((LANES, HG * SLAB), 0)
    dst = _iota((LANES, HG * SLAB), 1)
    place = jnp.where((src < 3 * HG) & (dst == (src % HG) * SLAB + HEAD_DIM + src // HG), 1.0, 0.0)
    ks = (jnp.dot(kn, _place_heads_lanes(GW, HEAD_DIM), preferred_element_type=F32)
          + jnp.dot(packed.astype(BF16), place.astype(BF16), preferred_element_type=F32))
    k_out[...] = ks.astype(BF16)

    qn = (qt * lax.rsqrt(_mm(gmean, qt * qt, NN, "lo") + EPS) * gqt_ref[...]).astype(BF16)
    ones_rows = jnp.where(_iota((SLAB - HEAD_DIM, tm), 0) < 3, 1.0, 0.0).astype(BF16)
    qs = _slab_rows(qn, ones_rows)
    vb = _slab_rows(vt.astype(BF16),
                    jnp.where(_iota((VROWS - HEAD_DIM, tm), 0) == 0, 1.0, 0.0).astype(BF16))
    for c in range(tm // tile):
        q_out[c] = qs[:, c * tile:(c + 1) * tile]
        v_out[c] = vb[:, c * tile:(c + 1) * tile]


def _fox_prep(h, wk, wf, wqt, wvt, gk, gqt, fb, *, batch, seq, tm, tile):
    T, D = h.shape
    per_seq = seq // tm
    nt = tm // tile
    const = lambda b, i: (0, 0)
    row = lambda b, i: (b * per_seq + i, 0)
    return pl.pallas_call(
        functools.partial(_fox_prep_kernel, tile=tile),
        grid=(batch, per_seq),
        in_specs=[
            pl.BlockSpec((tm, D), row),
            pl.BlockSpec((D, GW), const),
            pl.BlockSpec((D, LANES), const),
            pl.BlockSpec((GW, D), const),
            pl.BlockSpec((GW, D), const),
            pl.BlockSpec((1, GW), const),
            pl.BlockSpec((GW, tm), const),
            pl.BlockSpec((1, LANES), const),
        ],
        out_specs=[
            pl.BlockSpec((nt, HG * SLAB, tile), lambda b, i: (b * per_seq + i, 0, 0)),
            pl.BlockSpec((tm, HG * SLAB), row),
            pl.BlockSpec((nt, HG * VROWS, tile), lambda b, i: (b * per_seq + i, 0, 0)),
        ],
        out_shape=[
            jax.ShapeDtypeStruct((T // tile, HG * SLAB, tile), BF16),
            jax.ShapeDtypeStruct((T, HG * SLAB), BF16),
            jax.ShapeDtypeStruct((T // tile, HG * VROWS, tile), BF16),
        ],
        scratch_shapes=[pltpu.VMEM((1, LANES), F32)],
        compiler_params=_params(("arbitrary", "arbitrary")),
        name="fox_prep",
    )(h, wk, wf, wqt, wvt, gk, gqt, fb)


def _attn_kernel(q_ref, k_ref, v_ref, aux_ref, gain_ref, o_ref, qz_s, m_s, acc_s,
                 s_a, s_b, p_a, p_b, al_a, al_b, mt_a, mt_b, *, n_maps, tile, lambda_init):
    qi = pl.program_id(1)
    row = _iota((tile, tile), 0)
    col = _iota((tile, tile), 1)
    if n_maps == 2:
        diag_ok = (row // ATT_CHUNK) <= (col // ATT_CHUNK)
    else:
        diag_ok = row <= col

    slab_row = _iota((SLAB, 1), 0)
    for h in range(HG):
        qh = q_ref[h * SLAB:(h + 1) * SLAB, :]
        if n_maps == 2:
            qz_s[h, :, 0:tile] = jnp.where(slab_row < DQ, qh, jnp.zeros_like(qh))
            qz_s[h, :, tile:2 * tile] = jnp.where((slab_row >= DQ) & (slab_row < 2 * DQ), qh,
                                                  jnp.zeros_like(qh))
        else:
            qz_s[h] = qh
    m_s[...] = jnp.full(m_s.shape, NEG_INF, F32)
    acc_s[...] = jnp.zeros(acc_s.shape, F32)
    p_b[...] = jnp.zeros(p_b.shape, BF16)
    al_b[...] = jnp.zeros(al_b.shape, F32)

    buf_a = (s_a, p_a, al_a, mt_a)
    buf_b = (s_b, p_b, al_b, mt_b)

    all_heads = tuple(range(HG))

    def scores(j, s_buf, mt_buf, heads=all_heads):
        for h in heads:
            kj = k_ref[pl.ds(pl.multiple_of(j * tile, tile), tile), h * SLAB:(h + 1) * SLAB]
            s = jnp.dot(kj, qz_s[h], preferred_element_type=F32)
            s_buf[h] = s
            for mp in range(n_maps):
                c = h * n_maps + mp
                mt_buf[c:c + 1, :] = jnp.max(s[:, mp * tile:(mp + 1) * tile], axis=0, keepdims=True)

    def value_update(j, p_buf, al_buf, heads=all_heads):
        for h in heads:
            vj = v_ref[j, h * VROWS:(h + 1) * VROWS, :]
            for mp in range(n_maps):
                c = h * n_maps + mp
                rows = slice(c * VROWS, (c + 1) * VROWS)
                pv = jnp.dot(vj, p_buf[c], preferred_element_type=F32)
                acc_s[rows, :] = al_buf[c:c + 1, :] * acc_s[rows, :] + pv

    def softmax(s_buf, p_buf, al_buf, mt_buf, masked, heads=all_heads):
        for h in heads:
            for mp in range(n_maps):
                c = h * n_maps + mp
                s = s_buf[h, :, mp * tile:(mp + 1) * tile]
                if masked:
                    s = jnp.where(diag_ok, s, NEG_INF)
                    tile_max = jnp.max(s, axis=0, keepdims=True)
                else:
                    tile_max = mt_buf[c:c + 1, :]
                m_old = m_s[c:c + 1, :]
                m_new = jnp.maximum(m_old, tile_max)
                m_s[c:c + 1, :] = m_new
                al_buf[c:c + 1, :] = jnp.exp2(m_old - m_new)
                p_buf[c] = jnp.exp2(s - m_new).astype(BF16)

    def step(j, cur, nxt):
        for h in all_heads:
            scores(j + 1, nxt[0], nxt[3], (h,))
            value_update(jnp.maximum(j - 1, 0), nxt[1], nxt[2], (h,))
            softmax(cur[0], cur[1], cur[2], cur[3], False, (h,))

    if n_maps == 2:
        aux = aux_ref[...]
        lam = (jnp.exp(jnp.sum(aux[0:1] * aux[1:2], axis=1, keepdims=True))
               - jnp.exp(jnp.sum(aux[2:3] * aux[3:4], axis=1, keepdims=True)) + lambda_init)

    def normalized(c):
        return (acc_s[c * VROWS:c * VROWS + HEAD_DIM, :]
                / acc_s[c * VROWS + HEAD_DIM:c * VROWS + HEAD_DIM + 1, :])

    def last(j, cur, nxt):
        value_update(jnp.maximum(j - 1, 0), nxt[1], nxt[2])
        softmax(cur[0], cur[1], cur[2], cur[3], True)
        outs = []
        for h in all_heads:
            value_update(j, cur[1], cur[2], (h,))
            o = normalized(h * n_maps)
            if n_maps == 2:
                o = o - lam * normalized(h * n_maps + 1)
                o = o * lax.rsqrt(jnp.mean(o * o, axis=0, keepdims=True) + EPS)
            outs.append(o)
        ot = jnp.concatenate(outs, axis=0)
        if n_maps == 2:
            ot = ot * gain_ref[...]
        o_ref[...] = ot.T.astype(BF16)

    scores(0, s_a, mt_a)

    def pair(j):
        step(j, buf_a, buf_b)
        step(j + 1, buf_b, buf_a)

    pairs_per_trip = 2 if n_maps == 1 else 1
    shift = pairs_per_trip.bit_length()

    def trip(jj, carry):
        for t in range(pairs_per_trip):
            pair(2 * pairs_per_trip * jj + 2 * t)
        return carry

    lax.fori_loop(0, lax.shift_right_logical(qi, shift), trip, 0)
    odd = (qi & 1) == 1

    if pairs_per_trip == 2:
        @pl.when((qi & 2) == 2)
        def _():
            pair(qi - (qi & 3))

    @pl.when(odd)
    def _():
        step(qi - 1, buf_a, buf_b)
        last(qi, buf_b, buf_a)

    @pl.when(jnp.logical_not(odd))
    def _():
        last(qi, buf_a, buf_b)


def _attention(q3, k2, v3, aux, gain, *, batch, seq, tile, n_maps, lambda_init):
    nq = seq // tile
    T = batch * seq
    n_chains = HG * n_maps
    scratch = [pltpu.VMEM((HG, SLAB, n_maps * tile), BF16),
               pltpu.VMEM((n_chains, tile), F32),
               pltpu.VMEM((n_chains * VROWS, tile), F32),
               pltpu.VMEM((HG, tile, n_maps * tile), F32),
               pltpu.VMEM((HG, tile, n_maps * tile), F32),
               pltpu.VMEM((n_chains, tile, tile), BF16),
               pltpu.VMEM((n_chains, tile, tile), BF16)] + [pltpu.VMEM((n_chains, tile), F32)] * 4
    return pl.pallas_call(
        functools.partial(_attn_kernel, n_maps=n_maps, tile=tile, lambda_init=lambda_init),
        grid=(batch, nq),
        in_specs=[
            pl.BlockSpec((None, HG * SLAB, tile), lambda b, i: (b * nq + i, 0, 0)),
            pl.BlockSpec((seq, HG * SLAB), lambda b, i: (b, 0)),
            pl.BlockSpec((nq, HG * VROWS, tile), lambda b, i: (b, 0, 0)),
            pl.BlockSpec(aux.shape, lambda b, i: (0, 0)),
            pl.BlockSpec(gain.shape, lambda b, i: (0, 0)),
        ],
        out_specs=pl.BlockSpec((tile, GW), lambda b, i: (b * nq + i, 0)),
        out_shape=jax.ShapeDtypeStruct((T, GW), BF16),
        scratch_shapes=scratch,
        compiler_params=_params(("arbitrary", "arbitrary")),
        name="diff_attention" if n_maps == 2 else "forgetting_attention",
    )(q3, k2, v3, aux, gain)


def _erf_gelu(x):
    return 0.5 * x * (1.0 + lax.erf(x * (1.0 / math.sqrt(2.0))))


def _gmlp_kernel(h_ref, wc_ref, sgw_ref, bias_ref, ln_ref, o_ref):
    tm = h_ref.shape[0]
    pc = _erf_gelu(jnp.dot(h_ref[...], wc_ref[...], preferred_element_type=F32))
    u = pc[:, 0:GW]
    v = pc[:, GW:2 * GW]
    mu = jnp.mean(v, axis=-1, keepdims=True)
    vc = v - mu
    var = jnp.mean(vc * vc, axis=-1, keepdims=True)
    ln = ln_ref[...]
    vn = (vc * lax.rsqrt(var + EPS) * ln[0:1] + ln[1:2]).astype(BF16)

    causal = _iota((SG_CHUNK, SG_CHUNK), 0) >= _iota((SG_CHUNK, SG_CHUNK), 1)
    w = jnp.concatenate([jnp.where(causal, sgw_ref[g], 0.0) for g in range(HG)], axis=0).astype(BF16)
    lane_head = _iota((1, GW), 1) // HEAD_DIM
    bias = bias_ref[...]
    for n in range(tm // SG_CHUNK):
        rows = slice(n * SG_CHUNK, (n + 1) * SG_CHUNK)
        full = jnp.dot(w, vn[rows], preferred_element_type=F32)
        sv = bias
        for g in range(HG):
            sv = sv + jnp.where(lane_head == g, full[g * SG_CHUNK:(g + 1) * SG_CHUNK], 0.0)
        o_ref[rows, :] = (u[rows] * sv).astype(BF16)


def _gmlp_mixer(h, wc, sg_w, bias, ln, *, tm):
    T, D = h.shape
    return pl.pallas_call(
        _gmlp_kernel,
        grid=(T // tm,),
        in_specs=[
            pl.BlockSpec((tm, D), lambda i: (i, 0)),
            pl.BlockSpec((D, 2 * GW), lambda i: (0, 0)),
            pl.BlockSpec((HG, SG_CHUNK, SG_CHUNK), lambda i: (0, 0, 0)),
            pl.BlockSpec((SG_CHUNK, GW), lambda i: (0, 0)),
            pl.BlockSpec((2, GW), lambda i: (0, 0)),
        ],
        out_specs=pl.BlockSpec((tm, GW), lambda i: (i, 0)),
        out_shape=jax.ShapeDtypeStruct((T, GW), BF16),
        compiler_params=_params(("arbitrary",)),
        name="gmlp_mixer",
    )(h, wc, sg_w, bias, ln)


TAIL = 8


FFN_COLS = 256
FFN_PARTS = 2


def _ffn_kernel(ya, yb, yc, yd, x_ref, wo_ref, g2_ref,
                wup_ref, cw_ref, cb_ref, wdn_ref, mod_ref, modn_ref, gn_ref,
                o_ref, hn_ref, act_ref, tail_ref, *, per_seq):
    i = pl.program_id(0)
    tm = x_ref.shape[0]
    dff = wdn_ref.shape[0]
    mod = mod_ref[...]

    @pl.when((i % per_seq) == 0)
    def _():
        tail_ref[...] = jnp.zeros_like(tail_ref)

    def taps(u, cw):
        return cw[0:1] * pltpu.roll(u, 2, axis=0) + cw[1:2] * pltpu.roll(u, 1, axis=0) + cw[2:3] * u

    glu = lambda a, g: (a * (g * _sigmoid(g))).astype(BF16)
    modn = modn_ref[...]

    def project(rows):
        proj = jnp.dot(ya[rows, :], wo_ref[0], preferred_element_type=F32)
        for m, y_ref in enumerate((yb, yc, yd), start=1):
            proj = proj + jnp.dot(y_ref[rows, :], wo_ref[m], preferred_element_type=F32)
        x_mid = x_ref[rows, :] + mod[2:3] * proj
        return x_mid, _modulated_norm(x_mid, g2_ref[...], mod[3:4], mod[4:5]).astype(BF16)

    def hidden(rows, hx):
        n = hx.shape[0]

        def conv_half(cols):
            u = jnp.dot(hx, wup_ref[:, cols], preferred_element_type=F32)
            cw = cw_ref[:, cols]
            cb = cb_ref[:, cols]
            prev = tail_ref[:, cols]
            tail_ref[:, cols] = u[n - TAIL:n]
            head = taps(jnp.concatenate([prev, u[0:TAIL]], axis=0), cw)[TAIL:2 * TAIL] + cb
            return taps(u, cw) + cb, head

        for j in range(dff // FFN_COLS):
            a, a_head = conv_half(slice(j * FFN_COLS, (j + 1) * FFN_COLS))
            g, g_head = conv_half(slice(dff + j * FFN_COLS, dff + (j + 1) * FFN_COLS))
            act = glu(a, g)
            fixed = jnp.concatenate([glu(a_head, g_head), act[TAIL:2 * TAIL]], axis=0)
            cols = slice(j * FFN_COLS, (j + 1) * FFN_COLS)
            act_ref[rows, cols] = act
            act_ref[rows.start:rows.start + 2 * TAIL, cols] = fixed

    def finish(rows, x_mid):
        y = jnp.dot(act_ref[rows, :], wdn_ref[...], preferred_element_type=F32)
        x = x_mid + mod[5:6] * y
        o_ref[rows, :] = x
        hn_ref[rows, :] = _modulated_norm(x, gn_ref[...], modn[0:1], modn[1:2]).astype(BF16)

    parts = [slice(r * (tm // FFN_PARTS), (r + 1) * (tm // FFN_PARTS)) for r in range(FFN_PARTS)]
    nxt = project(parts[0])
    for r, rows in enumerate(parts):
        x_mid, hx = nxt
        if r + 1 < FFN_PARTS:
            nxt = project(parts[r + 1])
        hidden(rows, hx)
        finish(rows, x_mid)


def _ffn(ys, x2, w_out4, g2, wup, cw, cb, wdn, mod_l, mod_next, g_next, *, seq, tm):
    T, D = x2.shape
    dff = wdn.shape[0]
    per_seq = seq // tm
    once = pl.Buffered(1)
    row = lambda i: (i, 0)
    return pl.pallas_call(
        functools.partial(_ffn_kernel, per_seq=per_seq),
        grid=(T // tm,),
        in_specs=[pl.BlockSpec((tm, GW), row)] * N_MIXERS + [
            pl.BlockSpec((tm, D), row),
            pl.BlockSpec((N_MIXERS, GW, D), lambda i: (0, 0, 0), pipeline_mode=once),
            pl.BlockSpec((1, D), lambda i: (0, 0)),
            pl.BlockSpec((D, 2 * dff), lambda i: (0, 0), pipeline_mode=once),
            pl.BlockSpec((3, 2 * dff), lambda i: (0, 0), pipeline_mode=once),
            pl.BlockSpec((1, 2 * dff), lambda i: (0, 0), pipeline_mode=once),
            pl.BlockSpec((dff, D), lambda i: (0, 0), pipeline_mode=once),
            pl.BlockSpec((None, 6, D), lambda i: (i // per_seq, 0, 0)),
            pl.BlockSpec((None, 6, D), lambda i: (i // per_seq, 0, 0)),
            pl.BlockSpec((1, D), lambda i: (0, 0)),
        ],
        out_specs=[pl.BlockSpec((tm, D), row), pl.BlockSpec((tm, D), row)],
        out_shape=[jax.ShapeDtypeStruct((T, D), F32), jax.ShapeDtypeStruct((T, D), BF16)],
        scratch_shapes=[pltpu.VMEM((tm, dff), BF16), pltpu.VMEM((TAIL, 2 * dff), F32)],
        compiler_params=_params(("arbitrary",)),
        name="conv_glu_ffn",
    )(*ys, x2, w_out4, g2.reshape(1, D), wup, cw, cb, wdn, mod_l, mod_next, g_next.reshape(1, D))


def _rope_tables(seq):
    inv = 1.0 / (ROPE_THETA ** (jnp.arange(0, DQ, 2, dtype=F32) / DQ))
    ang = jnp.arange(seq, dtype=F32)[:, None] * inv[None, :]
    cos = jnp.cos(ang)
    sin = jnp.sin(ang)
    cos_map = jnp.concatenate([cos, cos], axis=-1)
    sin_map = jnp.concatenate([-sin, sin], axis=-1)
    reps = GW // DQ
    return jnp.tile(cos_map, (1, reps)), jnp.tile(sin_map, (1, reps))


def kernel(x, c, ada_w, ada_b, norm1_g, norm2_g, w_in, w_out, rw_mu, rw_w0, rw_w_up, rw_a0, rw_a_up, rw_g_up, rw_k_k, rw_k_a, rw_r_k, rw_ln_g, rw_ln_b, df_lam_q1, df_lam_k1, df_lam_q2, df_lam_k2, df_q_g, df_k_g, df_sub_g, sg_w, sg_b, sg_ln_g, sg_ln_b, fx_q_g, fx_k_g, fx_f_b, ffn_up, ffn_conv, ffn_conv_b, ffn_down):
    Bn, S, D = x.shape
    L = ada_w.shape[0]
    T = Bn * S
    dff = ffn_down.shape[1]
    tm = min(ROW_TILE, S)
    tp = min(PREP_TILE, S)
    tile = min(ATT_TILE, S)

    mod = _ada_mod(c, ada_w, ada_b).reshape(L, Bn, 6, D)
    cos, sin = _rope_tables(S)
    cost, sint = cos.T, sin.T
    x2 = x.reshape(T, D)

    oa = 0
    ob = oa + RW_COLS
    oc = ob + 3 * GW
    od = oc + 2 * GW

    for l in range(L):
        lambda_init = 0.8 - 0.6 * math.exp(-0.3 * l)
        wl = w_in[l].astype(BF16)
        if l == 0:
            h = _norm_mod(x2, mod[l], norm1_g[l], seq=S, row0=0, tm=tm)

        lora_w = jnp.zeros((LANES, 3 * GW), F32)
        lora_w = lora_w.at[0:RW_DECAY_RANK, 0:GW].set(rw_w_up[l])
        lora_w = lora_w.at[RW_DECAY_RANK:RW_DECAY_RANK + RW_A_RANK, GW:2 * GW].set(rw_a_up[l])
        lora_w = lora_w.at[RW_DECAY_RANK + RW_A_RANK:LANES, 2 * GW:3 * GW].set(rw_g_up[l])
        vec = jnp.stack([rw_w0[l], rw_a0[l], rw_k_k[l], rw_k_a[l], rw_r_k[l].reshape(GW),
                         rw_ln_g[l], rw_ln_b[l], jnp.zeros((GW,), F32)])
        ya = _rwkv_mixer(h, wl[:, oa:ob], rw_mu[l].reshape(1, RW_COLS), lora_w, vec,
                         batch=Bn, seq=S, tt=min(RW_TILE, S), mode=RW_MODE)

        gk = jnp.tile(df_k_g[l], GW // DQ).reshape(1, GW)
        gqt = jnp.broadcast_to((jnp.tile(df_q_g[l], GW // DQ) * (DQ ** -0.5 * LOG2E))[:, None], (GW, tp))
        qb, kb, vb = _diff_prep(h, wl[:, ob + GW:ob + 2 * GW], wl[:, ob:ob + GW].T, wl[:, ob + 2 * GW:oc].T,
                                gk, gqt, cos, sin, cost, sint, seq=S, tm=tp, tile=tile)
        lam_vecs = jnp.stack([df_lam_q1[l], df_lam_k1[l], df_lam_q2[l], df_lam_k2[l]])
        sub_gain = jnp.broadcast_to((jnp.tile(df_sub_g[l], HG) * (1.0 - lambda_init))[:, None], (GW, tile))
        yb = _attention(qb, kb, vb, lam_vecs, sub_gain, batch=Bn, seq=S, tile=tile,
                        n_maps=2, lambda_init=lambda_init)

        sg_bias = jnp.repeat(sg_b[l].T, HEAD_DIM, axis=1)
        yc = _gmlp_mixer(h, wl[:, oc:od], sg_w[l], sg_bias,
                         jnp.stack([sg_ln_g[l], sg_ln_b[l]]), tm=tp)

        wf = jnp.zeros((D, LANES), BF16).at[:, 0:HG].set(wl[:, od + 3 * GW:od + 3 * GW + HG])
        fb = jnp.zeros((1, LANES), F32).at[0, 0:HG].set(fx_f_b[l])
        gkd = jnp.tile(fx_k_g[l], HG).reshape(1, GW)
        gqd = jnp.broadcast_to((jnp.tile(fx_q_g[l], HG) * (HEAD_DIM ** -0.5 * LOG2E))[:, None], (GW, tp))
        qd, kd, vd = _fox_prep(h, wl[:, od + GW:od + 2 * GW], wf, wl[:, od:od + GW].T,
                               wl[:, od + 2 * GW:od + 3 * GW].T, gkd, gqd, fb,
                               batch=Bn, seq=S, tm=tp, tile=tile)
        yd = _attention(qd, kd, vd, jnp.zeros((8, LANES), F32), jnp.zeros((8, LANES), F32),
                        batch=Bn, seq=S, tile=tile, n_maps=1, lambda_init=0.0)

        nl = (l + 1) % L
        x2, h = _ffn((ya, yb, yc, yd), x2, w_out[l].astype(BF16).reshape(N_MIXERS, GW, D), norm2_g[l],
                     ffn_up[l].astype(BF16), ffn_conv[l], ffn_conv_b[l].reshape(1, 2 * dff),
                     ffn_down[l].astype(BF16), mod[l], mod[nl], norm1_g[nl], seq=S, tm=tm)

    return x2.reshape(Bn, S, D)
```

```python
import functools
import math

import jax
import jax.numpy as jnp
from jax import lax
from jax.experimental import pallas as pl
from jax.experimental.pallas import tpu as pltpu

F32 = jnp.float32
BF16 = jnp.bfloat16
HIGHEST = lax.Precision.HIGHEST

N_MIXERS = 4
HEAD_DIM = 64
HG = 4
GW = HG * HEAD_DIM
DQ = HEAD_DIM // 2
RW_DECAY_RANK = 32
RW_A_RANK = 32
RW_GATE_RANK = 64
RW_COLS = 3 * GW + RW_DECAY_RANK + RW_A_RANK + RW_GATE_RANK
RW_GN_EPS = 64e-5
SG_CHUNK = 128
ATT_CHUNK = 64
ROPE_THETA = 10000.0
EPS = 1e-6
LOG2E = math.log2(math.e)
NEG_INF = -1e30

LANES = 128
V7X_VMEM_BYTES = 64 * 1024 * 1024
VMEM_LIMIT = 56 * 1024 * 1024

ROW_TILE = 512
PREP_TILE = 1024
RW_CHUNK = 64
RW_TILE = 512
ATT_TILE = 256
SLAB = 128
VROWS = HEAD_DIM + 16
RW_MODE = "lo"

NN = (((1,), (0,)), ((), ()))
NT = (((1,), (1,)), ((), ()))
TN = (((0,), (0,)), ((), ()))


def _params(sem):
    return pltpu.CompilerParams(dimension_semantics=sem, vmem_limit_bytes=VMEM_LIMIT)


def _mm(a, b, dims=NN, mode="hi"):
    if mode == "hi":
        return lax.dot_general(a.astype(F32), b.astype(F32), dims, precision=HIGHEST,
                               preferred_element_type=F32)
    if mode == "lo":
        return lax.dot_general(a.astype(BF16), b.astype(BF16), dims, preferred_element_type=F32)
    ah = a.astype(BF16)
    al = (a - ah.astype(F32)).astype(BF16)
    bh = b.astype(BF16)
    bl = (b - bh.astype(F32)).astype(BF16)
    dg = functools.partial(lax.dot_general, dimension_numbers=dims, preferred_element_type=F32)
    return dg(ah, bh) + (dg(ah, bl) + dg(al, bh))


def _mm_exact(a, b, dims=NN, left=True):
    x = a if left else b
    xh = x.astype(BF16)
    xl = (x - xh.astype(F32)).astype(BF16)
    dg = functools.partial(lax.dot_general, dimension_numbers=dims, preferred_element_type=F32)
    if left:
        bb = b.astype(BF16)
        return dg(xh, bb) + dg(xl, bb)
    ab = a.astype(BF16)
    return dg(ab, xh) + dg(ab, xl)


def _iota(shape, dim):
    return lax.broadcasted_iota(jnp.int32, shape, dim)


def _group_matrix(n, group, value=1.0):
    same = (_iota((n, n), 0) // group) == (_iota((n, n), 1) // group)
    return jnp.where(same, value, 0.0).astype(F32)


def _softplus(x):
    return jnp.maximum(x, 0.0) + jnp.log1p(jnp.exp(-jnp.abs(x)))


def _sigmoid(x):
    return 1.0 / (1.0 + jnp.exp(-x))


def _ada_kernel(c_ref, w_ref, b_ref, o_ref):
    c = c_ref[...]
    cond = c * _sigmoid(c)
    o_ref[...] = _mm(cond, w_ref[...], NN, "hi") + b_ref[...]


def _ada_mod(c, ada_w, ada_b):
    L, D, D6 = ada_w.shape
    Bn = c.shape[0]
    tn = D6 // 4
    return pl.pallas_call(
        _ada_kernel,
        grid=(L, D6 // tn),
        in_specs=[
            pl.BlockSpec((Bn, D), lambda l, j: (0, 0)),
            pl.BlockSpec((None, D, tn), lambda l, j: (l, 0, j)),
            pl.BlockSpec((None, 1, tn), lambda l, j: (l, 0, j)),
        ],
        out_specs=pl.BlockSpec((None, Bn, tn), lambda l, j: (l, 0, j)),
        out_shape=jax.ShapeDtypeStruct((L, Bn, D6), F32),
        compiler_params=_params(("arbitrary", "arbitrary")),
        name="ada_mod",
    )(c, ada_w, ada_b.reshape(L, 1, D6))


def _modulated_norm(x, g, shift, scale):
    y = x * lax.rsqrt(jnp.mean(x * x, axis=-1, keepdims=True) + EPS) * g
    return y * (1.0 + scale) + shift


def _norm_kernel(x_ref, mod_ref, g_ref, o_ref, *, row0):
    mod = mod_ref[...]
    h = _modulated_norm(x_ref[...], g_ref[...], mod[row0:row0 + 1], mod[row0 + 1:row0 + 2])
    o_ref[...] = h.astype(BF16)


def _norm_mod(x2, mod_l, g, *, seq, row0, tm):
    T, D = x2.shape
    per_seq = seq // tm
    return pl.pallas_call(
        functools.partial(_norm_kernel, row0=row0),
        grid=(T // tm,),
        in_specs=[
            pl.BlockSpec((tm, D), lambda i: (i, 0)),
            pl.BlockSpec((None, 6, D), lambda i: (i // per_seq, 0, 0)),
            pl.BlockSpec((1, D), lambda i: (0, 0)),
        ],
        out_specs=pl.BlockSpec((tm, D), lambda i: (i, 0)),
        out_shape=jax.ShapeDtypeStruct((T, D), BF16),
        compiler_params=_params(("arbitrary",)),
        name="norm_mod",
    )(x2, mod_l, g.reshape(1, D))


def _blockdiag(mp, head_lane_masks):
    return jnp.concatenate([jnp.where(m, mp, 0.0) for m in head_lane_masks], axis=0)


def _head_blocks(full, head_lane_masks, C):
    out = jnp.where(head_lane_masks[0], full[0:C], 0.0)
    for h in range(1, HG):
        out = out + jnp.where(head_lane_masks[h], full[h * C:(h + 1) * C], 0.0)
    return out


def _apply_packed(mp, rhs, head_lane_masks, C, mode):
    return _head_blocks(_mm(mp, rhs, TN, mode), head_lane_masks, C)


def _rwkv_kernel(h_ref, wa_ref, mu_ref, lora_ref, vec_ref, o_ref,
                 carry_ref, st_ref, hs_ref, *, C, mode):
    TT = h_ref.shape[0]

    @pl.when(pl.program_id(1) == 0)
    def _():
        carry_ref[...] = jnp.zeros_like(carry_ref)
        st_ref[...] = jnp.zeros_like(st_ref)

    vec = vec_ref[...]
    w0, a0, k_k, k_a, r_k, ln_g, ln_b = (vec[i:i + 1] for i in range(7))

    pa = jnp.dot(h_ref[...], wa_ref[...], preferred_element_type=F32)
    prev = pltpu.roll(pa, 1, axis=0)
    prev = jnp.where(_iota((TT, 1), 0) == 0, carry_ref[...], prev)
    carry_ref[...] = pa[TT - 1:TT]
    pa = pa + (prev - pa) * mu_ref[...]

    r = pa[:, 0:GW]
    k = pa[:, GW:2 * GW]
    v = pa[:, 2 * GW:3 * GW]
    lo = pa[:, 3 * GW:3 * GW + LANES]
    lane = _iota((1, LANES), 1)
    act = jnp.where(lane < RW_DECAY_RANK, jnp.tanh(lo),
                    jnp.where(lane < RW_DECAY_RANK + RW_A_RANK, lo, _sigmoid(lo)))
    lora = _mm_exact(act, lora_ref[...])
    w_log = -_softplus(-(w0 + lora[:, 0:GW])) - 0.5
    lw = -jnp.exp(w_log)
    a = _sigmoid(a0 + lora[:, GW:2 * GW])
    gate = lora[:, 2 * GW:3 * GW]

    gsum = _group_matrix(GW, HEAD_DIM)
    kk = k * k_k
    kk = kk / jnp.maximum(jnp.sqrt(_mm(kk * kk, gsum, NN, "lo")), 1e-12)
    k2 = k * (1.0 + (a - 1.0) * k_a)
    bonus = _mm(r * k2 * r_k, gsum, NN, "lo") * v

    aa = -kk
    b = kk * a

    lane_head = _iota((1, GW), 1) // HEAD_DIM
    hmask = [lane_head == h for h in range(HG)]
    s_idx = _iota((C, GW), 0)
    t_idx = _iota((C, GW), 1) % C
    strict = s_idx < t_idx
    incl = s_idx <= t_idx
    eye_p = jnp.where(s_idx == t_idx, 1.0, 0.0).astype(F32)
    l_incl = jnp.where(_iota((C, C), 0) >= _iota((C, C), 1), 1.0, 0.0).astype(F32)
    bd = (_iota((GW, GW), 0) // HEAD_DIM) == (_iota((GW, GW), 1) // HEAD_DIM)
    eye_k = jnp.where(_iota((GW, GW), 0) == _iota((GW, GW), 1), 1.0, 0.0).astype(F32)
    chunks = range(TT // C)
    sl = lambda x, c: x[c * C:(c + 1) * C]

    cl = jnp.concatenate([_mm_exact(l_incl, sl(lw, c), left=False) for c in chunks], axis=0)
    g_in = jnp.exp(cl)
    g_inv = jnp.exp(-cl)
    at = aa * jnp.exp(cl - lw)
    rt = r * g_in
    bt = b * g_inv
    kt = k2 * g_inv

    bks = [jnp.concatenate([sl(bt, c), sl(kt, c)], axis=0) for c in chunks]
    xs = [_mm(bks[c], jnp.concatenate([_blockdiag(sl(at, c), hmask), _blockdiag(sl(rt, c), hmask)], axis=0),
              NT, mode) for c in chunks]
    n_p = [jnp.where(strict, x[0:C, 0:GW], 0.0) for x in xs]
    ak_p = [jnp.where(strict, x[C:2 * C, 0:GW], 0.0) for x in xs]
    rbk_p = [jnp.concatenate([jnp.where(incl, x[0:C, GW:2 * GW], 0.0),
                              jnp.where(incl, x[C:2 * C, GW:2 * GW], 0.0)], axis=0) for x in xs]

    p_p = [eye_p + n for n in n_p]
    q_p = n_p
    bd_dtype = BF16 if mode == "lo" else F32
    q_bd = [_blockdiag(q, hmask).astype(bd_dtype) for q in q_p]
    for _ in range(int(math.log2(C)) - 1):
        q_p = [_mm(q, b, NN, mode) for q, b in zip(q_p, q_bd)]
        q_bd = [_blockdiag(q, hmask).astype(bd_dtype) for q in q_p]
        p_p = [p + _mm(p, b, NN, mode) for p, b in zip(p_p, q_bd)]

    wv = [_apply_packed(ak_p[c], sl(v, c), hmask, C, mode) for c in chunks]
    ta_tw = [_mm(p_p[c], jnp.concatenate([sl(at, c), wv[c]], axis=1), TN, mode) for c in chunks]
    ta = [_head_blocks(f[:, 0:GW], hmask, C) for f in ta_tw]
    tw = [_head_blocks(f[:, GW:2 * GW], hmask, C) for f in ta_tw]
    g_end = [g_in[(c + 1) * C - 1:(c + 1) * C] for c in chunks]
    m_t = [(eye_k + jnp.where(bd, _mm(ta[c], sl(bt, c), TN, mode), 0.0)) * g_end[c] for c in chunks]
    g_t = [jnp.where(bd, _mm(jnp.concatenate([tw[c], sl(v, c)], axis=0), bks[c], TN, mode), 0.0) * g_end[c]
           for c in chunks]

    ht = st_ref[...]
    for c in chunks:
        hs_ref[c] = ht
        ht = _mm(ht, m_t[c], NN, mode) + g_t[c]
    st_ref[...] = ht

    ys = []
    for c in chunks:
        sh = _mm(jnp.concatenate([ta[c], sl(rt, c)], axis=0), hs_ref[c], NT, mode)
        uv = jnp.concatenate([sh[0:C] + tw[c], sl(v, c)], axis=0)
        ys.append(sh[C:2 * C] + _apply_packed(rbk_p[c], uv, hmask, C, mode))
    y = jnp.concatenate(ys, axis=0)

    gmean = _group_matrix(GW, HEAD_DIM, 1.0 / HEAD_DIM)
    yc = y - _mm(y, gmean, NN, "lo")
    yn = yc * lax.rsqrt(_mm(yc * yc, gmean, NN, "lo") + RW_GN_EPS) * ln_g + ln_b
    o_ref[...] = ((yn + bonus) * gate).astype(BF16)


def _rwkv_mixer(h, wa, mu, lora_w, vec, *, batch, seq, tt, mode):
    T, D = h.shape
    per_seq = seq // tt
    C = RW_CHUNK
    return pl.pallas_call(
        functools.partial(_rwkv_kernel, C=C, mode=mode),
        grid=(batch, per_seq),
        in_specs=[
            pl.BlockSpec((tt, D), lambda b, i: (b * per_seq + i, 0)),
            pl.BlockSpec((D, RW_COLS), lambda b, i: (0, 0)),
            pl.BlockSpec((1, RW_COLS), lambda b, i: (0, 0)),
            pl.BlockSpec((LANES, 3 * GW), lambda b, i: (0, 0)),
            pl.BlockSpec((8, GW), lambda b, i: (0, 0)),
        ],
        out_specs=pl.BlockSpec((tt, GW), lambda b, i: (b * per_seq + i, 0)),
        out_shape=jax.ShapeDtypeStruct((T, GW), BF16),
        scratch_shapes=[pltpu.VMEM((1, RW_COLS), F32), pltpu.VMEM((GW, GW), F32),
                        pltpu.VMEM((tt // C, GW, GW), F32)],
        compiler_params=_params(("arbitrary", "arbitrary")),
        name="rwkv7_mixer",
    )(h, wa, mu, lora_w, vec)


def _place_heads_lanes(n_in, width):
    src = _iota((n_in, HG * SLAB), 0)
    dst = _iota((n_in, HG * SLAB), 1)
    ok = ((dst // SLAB) == (src // width)) & ((dst % SLAB) == (src % width)) & ((dst % SLAB) < width)
    return jnp.where(ok, 1.0, 0.0).astype(BF16)


def _slab_rows(xt, extra):
    parts = []
    for h in range(HG):
        parts.append(xt[h * HEAD_DIM:(h + 1) * HEAD_DIM])
        parts.append(extra)
    return jnp.concatenate(parts, axis=0)


def _diff_prep_kernel(h_ref, wk_ref, wqt_ref, wvt_ref, gk_ref, gqt_ref,
                      cos_ref, sin_ref, cost_ref, sint_ref, q_out, k_out, v_out, *, tile):
    tm = h_ref.shape[0]
    hb = h_ref[...]
    k = jnp.dot(hb, wk_ref[...], preferred_element_type=F32)
    qt = lax.dot_general(wqt_ref[...], hb, NT, preferred_element_type=F32)
    vt = lax.dot_general(wvt_ref[...], hb, NT, preferred_element_type=F32)
    gmean = _group_matrix(GW, DQ, 1.0 / DQ)

    kn = k * lax.rsqrt(_mm(k * k, gmean, NN, "lo") + EPS) * gk_ref[...]
    first_half = (_iota((1, GW), 1) % DQ) < (DQ // 2)
    partner = jnp.where(first_half, pltpu.roll(kn, GW - DQ // 2, axis=1), pltpu.roll(kn, DQ // 2, axis=1))
    kr = kn * cos_ref[...] + partner * sin_ref[...]
    k_out[...] = jnp.dot(kr.astype(BF16), _place_heads_lanes(GW, HEAD_DIM),
                         preferred_element_type=F32).astype(BF16)

    qn = qt * lax.rsqrt(_mm(gmean, qt * qt, NN, "lo") + EPS) * gqt_ref[...]
    first_half_t = (_iota((GW, 1), 0) % DQ) < (DQ // 2)
    partner_t = jnp.where(first_half_t, pltpu.roll(qn, GW - DQ // 2, axis=0), pltpu.roll(qn, DQ // 2, axis=0))
    qr = (qn * cost_ref[...] + partner_t * sint_ref[...]).astype(BF16)
    qs = _slab_rows(qr, jnp.zeros((SLAB - HEAD_DIM, tm), BF16))
    vb = _slab_rows(vt.astype(BF16),
                    jnp.where(_iota((VROWS - HEAD_DIM, tm), 0) == 0, 1.0, 0.0).astype(BF16))
    for c in range(tm // tile):
        q_out[c] = qs[:, c * tile:(c + 1) * tile]
        v_out[c] = vb[:, c * tile:(c + 1) * tile]


def _fox_prep_kernel(h_ref, wk_ref, wf_ref, wqt_ref, wvt_ref, gk_ref, gqt_ref, fb_ref,
                     q_out, k_out, v_out, fcarry_ref, *, tile):
    tm = h_ref.shape[0]

    @pl.when(pl.program_id(1) == 0)
    def _():
        fcarry_ref[...] = jnp.zeros_like(fcarry_ref)

    hb = h_ref[...]
    k = jnp.dot(hb, wk_ref[...], preferred_element_type=F32)
    fl = jnp.dot(hb, wf_ref[...], preferred_element_type=F32)
    qt = lax.dot_general(wqt_ref[...], hb, NT, preferred_element_type=F32)
    vt = lax.dot_general(wvt_ref[...], hb, NT, preferred_element_type=F32)
    gmean = _group_matrix(GW, HEAD_DIM, 1.0 / HEAD_DIM)

    kn = (k * lax.rsqrt(_mm(k * k, gmean, NN, "lo") + EPS) * gk_ref[...]).astype(BF16)

    log_f = -_softplus(-(fl + fb_ref[...]))
    cb = min(tm, LANES)
    l_incl = jnp.where(_iota((cb, cb), 0) >= _iota((cb, cb), 1), 1.0, 0.0).astype(F32)
    carry = fcarry_ref[...]
    blocks = []
    for r in range(tm // cb):
        blocks.append(_mm_exact(l_incl, log_f[r * cb:(r + 1) * cb], left=False) + carry)
        carry = blocks[-1][cb - 1:cb]
    fcarry_ref[...] = carry
    nf = jnp.concatenate(blocks, axis=0) * (-LOG2E)
    head_lane = _iota((1, LANES), 1) < HG
    packed = jnp.zeros_like(nf)
    for i in range(3):
        p = jnp.where(head_lane, nf.astype(BF16).astype(F32), 0.0)
        packed = packed + (p if i == 0 else pltpu.roll(p, HG * i, axis=1))
        nf = nf - p
    src = _iota((LANES, HG * SLAB), 0)
    dst = _iota((LANES, HG * SLAB), 1)
    place = jnp.where((src < 3 * HG) & (dst == (src % HG) * SLAB + HEAD_DIM + src // HG), 1.0, 0.0)
    ks = (jnp.dot(kn, _place_heads_lanes(GW, HEAD_DIM), preferred_element_type=F32)
          + jnp.dot(packed.astype(BF16), place.astype(BF16), preferred_element_type=F32))
    k_out[...] = ks.astype(BF16)

    qn = (qt * lax.rsqrt(_mm(gmean, qt * qt, NN, "lo") + EPS) * gqt_ref[...]).astype(BF16)
    ones_rows = jnp.where(_iota((SLAB - HEAD_DIM, tm), 0) < 3, 1.0, 0.0).astype(BF16)
    qs = _slab_rows(qn, ones_rows)
    vb = _slab_rows(vt.astype(BF16),
                    jnp.where(_iota((VROWS - HEAD_DIM, tm), 0) == 0, 1.0, 0.0).astype(BF16))
    for c in range(tm // tile):
        q_out[c] = qs[:, c * tile:(c + 1) * tile]
        v_out[c] = vb[:, c * tile:(c + 1) * tile]


def _attn_kernel(q_ref, k_ref, v_ref, aux_ref, gain_ref, o_ref, qz_s, m_s, acc_s,
                 s_a, s_b, p_a, p_b, al_a, al_b, mt_a, mt_b, *, n_maps, tile, lambda_init):
    qi = pl.program_id(1)
    row = _iota((tile, tile), 0)
    col = _iota((tile, tile), 1)
    if n_maps == 2:
        diag_ok = (row // ATT_CHUNK) <= (col // ATT_CHUNK)
    else:
        diag_ok = row <= col

    slab_row = _iota((SLAB, 1), 0)
    for h in range(HG):
        qh = q_ref[h * SLAB:(h + 1) * SLAB, :]
        if n_maps == 2:
            qz_s[h, :, 0:tile] = jnp.where(slab_row < DQ, qh, jnp.zeros_like(qh))
            qz_s[h, :, tile:2 * tile] = jnp.where((slab_row >= DQ) & (slab_row < 2 * DQ), qh,
                                                  jnp.zeros_like(qh))
        else:
            qz_s[h] = qh
    m_s[...] = jnp.full(m_s.shape, NEG_INF, F32)
    acc_s[...] = jnp.zeros(acc_s.shape, F32)
    p_b[...] = jnp.zeros(p_b.shape, BF16)
    al_b[...] = jnp.zeros(al_b.shape, F32)

    buf_a = (s_a, p_a, al_a, mt_a)
    buf_b = (s_b, p_b, al_b, mt_b)

    all_heads = tuple(range(HG))

    def scores(j, s_buf, mt_buf, heads=all_heads):
        for h in heads:
            kj = k_ref[pl.ds(pl.multiple_of(j * tile, tile), tile), h * SLAB:(h + 1) * SLAB]
            s = jnp.dot(kj, qz_s[h], preferred_element_type=F32)
            s_buf[h] = s
            for mp in range(n_maps):
                c = h * n_maps + mp
                mt_buf[c:c + 1, :] = jnp.max(s[:, mp * tile:(mp + 1) * tile], axis=0, keepdims=True)

    def value_update(j, p_buf, al_buf, heads=all_heads):
        for h in heads:
            vj = v_ref[j, h * VROWS:(h + 1) * VROWS, :]
            for mp in range(n_maps):
                c = h * n_maps + mp
                rows = slice(c * VROWS, (c + 1) * VROWS)
                pv = jnp.dot(vj, p_buf[c], preferred_element_type=F32)
                acc_s[rows, :] = al_buf[c:c + 1, :] * acc_s[rows, :] + pv

    def softmax(s_buf, p_buf, al_buf, mt_buf, masked, heads=all_heads):
        for h in heads:
            for mp in range(n_maps):
                c = h * n_maps + mp
                s = s_buf[h, :, mp * tile:(mp + 1) * tile]
                if masked:
                    s = jnp.where(diag_ok, s, NEG_INF)
                    tile_max = jnp.max(s, axis=0, keepdims=True)
                else:
                    tile_max = mt_buf[c:c + 1, :]
                m_old = m_s[c:c + 1, :]
                m_new = jnp.maximum(m_old, tile_max)
                m_s[c:c + 1, :] = m_new
                al_buf[c:c + 1, :] = jnp.exp2(m_old - m_new)
                p_buf[c] = jnp.exp2(s - m_new).astype(BF16)

    def step(j, cur, nxt):
        for h in all_heads:
            scores(j + 1, nxt[0], nxt[3], (h,))
            value_update(jnp.maximum(j - 1, 0), nxt[1], nxt[2], (h,))
            softmax(cur[0], cur[1], cur[2], cur[3], False, (h,))

    if n_maps == 2:
        aux = aux_ref[...]
        lam = (jnp.exp(jnp.sum(aux[0:1] * aux[1:2], axis=1, keepdims=True))
               - jnp.exp(jnp.sum(aux[2:3] * aux[3:4], axis=1, keepdims=True)) + lambda_init)

    def normalized(c):
        return (acc_s[c * VROWS:c * VROWS + HEAD_DIM, :]
                / acc_s[c * VROWS + HEAD_DIM:c * VROWS + HEAD_DIM + 1, :])

    def last(j, cur, nxt):
        value_update(jnp.maximum(j - 1, 0), nxt[1], nxt[2])
        softmax(cur[0], cur[1], cur[2], cur[3], True)
        outs = []
        for h in all_heads:
            value_update(j, cur[1], cur[2], (h,))
            o = normalized(h * n_maps)
            if n_maps == 2:
                o = o - lam * normalized(h * n_maps + 1)
                o = o * lax.rsqrt(jnp.mean(o * o, axis=0, keepdims=True) + EPS)
            outs.append(o)
        ot = jnp.concatenate(outs, axis=0)
        if n_maps == 2:
            ot = ot * gain_ref[...]
        o_ref[...] = ot.T.astype(BF16)

    scores(0, s_a, mt_a)

    def pair(j):
        step(j, buf_a, buf_b)
        step(j + 1, buf_b, buf_a)

    pairs_per_trip = 2 if n_maps == 1 else 1
    shift = pairs_per_trip.bit_length()

    def trip(jj, carry):
        for t in range(pairs_per_trip):
            pair(2 * pairs_per_trip * jj + 2 * t)
        return carry

    lax.fori_loop(0, lax.shift_right_logical(qi, shift), trip, 0)
    odd = (qi & 1) == 1

    if pairs_per_trip == 2:
        @pl.when((qi & 2) == 2)
        def _():
            pair(qi - (qi & 3))

    @pl.when(odd)
    def _():
        step(qi - 1, buf_a, buf_b)
        last(qi, buf_b, buf_a)

    @pl.when(jnp.logical_not(odd))
    def _():
        last(qi, buf_a, buf_b)


def _attention(q3, k2, v3, aux, gain, *, batch, seq, tile, n_maps, lambda_init):
    nq = seq // tile
    T = batch * seq
    n_chains = HG * n_maps
    scratch = [pltpu.VMEM((HG, SLAB, n_maps * tile), BF16),
               pltpu.VMEM((n_chains, tile), F32),
               pltpu.VMEM((n_chains * VROWS, tile), F32),
               pltpu.VMEM((HG, tile, n_maps * tile), F32),
               pltpu.VMEM((HG, tile, n_maps * tile), F32),
               pltpu.VMEM((n_chains, tile, tile), BF16),
               pltpu.VMEM((n_chains, tile, tile), BF16)] + [pltpu.VMEM((n_chains, tile), F32)] * 4
    return pl.pallas_call(
        functools.partial(_attn_kernel, n_maps=n_maps, tile=tile, lambda_init=lambda_init),
        grid=(batch, nq),
        in_specs=[
            pl.BlockSpec((None, HG * SLAB, tile), lambda b, i: (b * nq + i, 0, 0)),
            pl.BlockSpec((seq, HG * SLAB), lambda b, i: (b, 0)),
            pl.BlockSpec((nq, HG * VROWS, tile), lambda b, i: (b, 0, 0)),
            pl.BlockSpec(aux.shape, lambda b, i: (0, 0)),
            pl.BlockSpec(gain.shape, lambda b, i: (0, 0)),
        ],
        out_specs=pl.BlockSpec((tile, GW), lambda b, i: (b * nq + i, 0)),
        out_shape=jax.ShapeDtypeStruct((T, GW), BF16),
        scratch_shapes=scratch,
        compiler_params=_params(("arbitrary", "arbitrary")),
        name="diff_attention" if n_maps == 2 else "forgetting_attention",
    )(q3, k2, v3, aux, gain)


def _erf_gelu(x):
    return 0.5 * x * (1.0 + lax.erf(x * (1.0 / math.sqrt(2.0))))


def _gmlp_kernel(h_ref, wc_ref, sgw_ref, bias_ref, ln_ref, o_ref):
    tm = h_ref.shape[0]
    pc = _erf_gelu(jnp.dot(h_ref[...], wc_ref[...], preferred_element_type=F32))
    u = pc[:, 0:GW]
    v = pc[:, GW:2 * GW]
    mu = jnp.mean(v, axis=-1, keepdims=True)
    vc = v - mu
    var = jnp.mean(vc * vc, axis=-1, keepdims=True)
    ln = ln_ref[...]
    vn = (vc * lax.rsqrt(var + EPS) * ln[0:1] + ln[1:2]).astype(BF16)

    causal = _iota((SG_CHUNK, SG_CHUNK), 0) >= _iota((SG_CHUNK, SG_CHUNK), 1)
    w = jnp.concatenate([jnp.where(causal, sgw_ref[g], 0.0) for g in range(HG)], axis=0).astype(BF16)
    lane_head = _iota((1, GW), 1) // HEAD_DIM
    bias = bias_ref[...]
    for n in range(tm // SG_CHUNK):
        rows = slice(n * SG_CHUNK, (n + 1) * SG_CHUNK)
        full = jnp.dot(w, vn[rows], preferred_element_type=F32)
        sv = bias
        for g in range(HG):
            sv = sv + jnp.where(lane_head == g, full[g * SG_CHUNK:(g + 1) * SG_CHUNK], 0.0)
        o_ref[rows, :] = (u[rows] * sv).astype(BF16)


N_DIFF_IN, N_FOX_IN, N_GMLP_IN = 9, 7, 4


def _mixer_prep_kernel(*refs, tile):
    h_ref = refs[0]
    a = 1
    diff_in = refs[a:a + N_DIFF_IN]
    fox_in = refs[a + N_DIFF_IN:a + N_DIFF_IN + N_FOX_IN]
    gmlp_in = refs[a + N_DIFF_IN + N_FOX_IN:a + N_DIFF_IN + N_FOX_IN + N_GMLP_IN]
    o = a + N_DIFF_IN + N_FOX_IN + N_GMLP_IN
    diff_out, fox_out, gmlp_out, fcarry_ref = refs[o:o + 3], refs[o + 3:o + 6], refs[o + 6], refs[o + 7]
    _diff_prep_kernel(h_ref, *diff_in, *diff_out, tile=tile)
    _fox_prep_kernel(h_ref, *fox_in, *fox_out, fcarry_ref, tile=tile)
    _gmlp_kernel(h_ref, *gmlp_in, gmlp_out)


def _mixer_prep(h, diff_args, fox_args, gmlp_args, *, batch, seq, tm, tile):
    T, D = h.shape
    per_seq = seq // tm
    nt = tm // tile
    const2 = lambda b, i: (0, 0)
    row = lambda b, i: (b * per_seq + i, 0)
    tiles = lambda b, i: (b * per_seq + i, 0, 0)
    pos_rows = lambda b, i: (i, 0)
    pos_cols = lambda b, i: (0, i)
    attn_out_specs = [pl.BlockSpec((nt, HG * SLAB, tile), tiles), pl.BlockSpec((tm, HG * SLAB), row),
                      pl.BlockSpec((nt, HG * VROWS, tile), tiles)]
    attn_out_shape = [jax.ShapeDtypeStruct((T // tile, HG * SLAB, tile), BF16),
                      jax.ShapeDtypeStruct((T, HG * SLAB), BF16),
                      jax.ShapeDtypeStruct((T // tile, HG * VROWS, tile), BF16)]
    in_specs = [
        pl.BlockSpec((tm, D), row),
        pl.BlockSpec((D, GW), const2), pl.BlockSpec((GW, D), const2), pl.BlockSpec((GW, D), const2),
        pl.BlockSpec((1, GW), const2), pl.BlockSpec((GW, tm), const2),
        pl.BlockSpec((tm, GW), pos_rows), pl.BlockSpec((tm, GW), pos_rows),
        pl.BlockSpec((GW, tm), pos_cols), pl.BlockSpec((GW, tm), pos_cols),
        pl.BlockSpec((D, GW), const2), pl.BlockSpec((D, LANES), const2),
        pl.BlockSpec((GW, D), const2), pl.BlockSpec((GW, D), const2),
        pl.BlockSpec((1, GW), const2), pl.BlockSpec((GW, tm), const2), pl.BlockSpec((1, LANES), const2),
        pl.BlockSpec((D, 2 * GW), const2), pl.BlockSpec((HG, SG_CHUNK, SG_CHUNK), lambda b, i: (0, 0, 0)),
        pl.BlockSpec((SG_CHUNK, GW), const2), pl.BlockSpec((2, GW), const2),
    ]
    assert (len(diff_args), len(fox_args), len(gmlp_args)) == (N_DIFF_IN, N_FOX_IN, N_GMLP_IN)
    outs = pl.pallas_call(
        functools.partial(_mixer_prep_kernel, tile=tile),
        grid=(batch, per_seq),
        in_specs=in_specs,
        out_specs=attn_out_specs + attn_out_specs + [pl.BlockSpec((tm, GW), row)],
        out_shape=attn_out_shape + attn_out_shape + [jax.ShapeDtypeStruct((T, GW), BF16)],
        scratch_shapes=[pltpu.VMEM((1, LANES), F32)],
        compiler_params=_params(("arbitrary", "arbitrary")),
        name="mixer_prep",
    )(h, *diff_args, *fox_args, *gmlp_args)
    return outs[0:3], outs[3:6], outs[6]


TAIL = 8


FFN_COLS = 256
FFN_PARTS = 2


def _ffn_kernel(ya, yb, yc, yd, x_ref, wo_ref, g2_ref,
                wup_ref, cw_ref, cb_ref, wdn_ref, mod_ref, modn_ref, gn_ref,
                o_ref, hn_ref, act_ref, tail_ref, *, per_seq):
    i = pl.program_id(0)
    tm = x_ref.shape[0]
    dff = wdn_ref.shape[0]
    mod = mod_ref[...]

    @pl.when((i % per_seq) == 0)
    def _():
        tail_ref[...] = jnp.zeros_like(tail_ref)

    def taps(u, cw):
        return cw[0:1] * pltpu.roll(u, 2, axis=0) + cw[1:2] * pltpu.roll(u, 1, axis=0) + cw[2:3] * u

    glu = lambda a, g: (a * (g * _sigmoid(g))).astype(BF16)
    modn = modn_ref[...]

    def project(rows):
        proj = jnp.dot(ya[rows, :], wo_ref[0], preferred_element_type=F32)
        for m, y_ref in enumerate((yb, yc, yd), start=1):
            proj = proj + jnp.dot(y_ref[rows, :], wo_ref[m], preferred_element_type=F32)
        x_mid = x_ref[rows, :] + mod[2:3] * proj
        return x_mid, _modulated_norm(x_mid, g2_ref[...], mod[3:4], mod[4:5]).astype(BF16)

    def hidden(rows, hx):
        n = hx.shape[0]

        def conv_half(cols):
            u = jnp.dot(hx, wup_ref[:, cols], preferred_element_type=F32)
            cw = cw_ref[:, cols]
            cb = cb_ref[:, cols]
            prev = tail_ref[:, cols]
            tail_ref[:, cols] = u[n - TAIL:n]
            head = taps(jnp.concatenate([prev, u[0:TAIL]], axis=0), cw)[TAIL:2 * TAIL] + cb
            return taps(u, cw) + cb, head

        for j in range(dff // FFN_COLS):
            a, a_head = conv_half(slice(j * FFN_COLS, (j + 1) * FFN_COLS))
            g, g_head = conv_half(slice(dff + j * FFN_COLS, dff + (j + 1) * FFN_COLS))
            act = glu(a, g)
            fixed = jnp.concatenate([glu(a_head, g_head), act[TAIL:2 * TAIL]], axis=0)
            cols = slice(j * FFN_COLS, (j + 1) * FFN_COLS)
            act_ref[rows, cols] = act
            act_ref[rows.start:rows.start + 2 * TAIL, cols] = fixed

    def finish(rows, x_mid):
        y = jnp.dot(act_ref[rows, :], wdn_ref[...], preferred_element_type=F32)
        x = x_mid + mod[5:6] * y
        o_ref[rows, :] = x
        hn_ref[rows, :] = _modulated_norm(x, gn_ref[...], modn[0:1], modn[1:2]).astype(BF16)

    parts = [slice(r * (tm // FFN_PARTS), (r + 1) * (tm // FFN_PARTS)) for r in range(FFN_PARTS)]
    nxt = project(parts[0])
    for r, rows in enumerate(parts):
        x_mid, hx = nxt
        if r + 1 < FFN_PARTS:
            nxt = project(parts[r + 1])
        hidden(rows, hx)
        finish(rows, x_mid)


def _ffn(ys, x2, w_out4, g2, wup, cw, cb, wdn, mod_l, mod_next, g_next, *, seq, tm):
    T, D = x2.shape
    dff = wdn.shape[0]
    per_seq = seq // tm
    once = pl.Buffered(1)
    row = lambda i: (i, 0)
    return pl.pallas_call(
        functools.partial(_ffn_kernel, per_seq=per_seq),
        grid=(T // tm,),
        in_specs=[pl.BlockSpec((tm, GW), row)] * N_MIXERS + [
            pl.BlockSpec((tm, D), row),
            pl.BlockSpec((N_MIXERS, GW, D), lambda i: (0, 0, 0), pipeline_mode=once),
            pl.BlockSpec((1, D), lambda i: (0, 0)),
            pl.BlockSpec((D, 2 * dff), lambda i: (0, 0), pipeline_mode=once),
            pl.BlockSpec((3, 2 * dff), lambda i: (0, 0), pipeline_mode=once),
            pl.BlockSpec((1, 2 * dff), lambda i: (0, 0), pipeline_mode=once),
            pl.BlockSpec((dff, D), lambda i: (0, 0), pipeline_mode=once),
            pl.BlockSpec((None, 6, D), lambda i: (i // per_seq, 0, 0)),
            pl.BlockSpec((None, 6, D), lambda i: (i // per_seq, 0, 0)),
            pl.BlockSpec((1, D), lambda i: (0, 0)),
        ],
        out_specs=[pl.BlockSpec((tm, D), row), pl.BlockSpec((tm, D), row)],
        out_shape=[jax.ShapeDtypeStruct((T, D), F32), jax.ShapeDtypeStruct((T, D), BF16)],
        scratch_shapes=[pltpu.VMEM((tm, dff), BF16), pltpu.VMEM((TAIL, 2 * dff), F32)],
        compiler_params=_params(("arbitrary",)),
        name="conv_glu_ffn",
    )(*ys, x2, w_out4, g2.reshape(1, D), wup, cw, cb, wdn, mod_l, mod_next, g_next.reshape(1, D))


def _rope_tables(seq):
    inv = 1.0 / (ROPE_THETA ** (jnp.arange(0, DQ, 2, dtype=F32) / DQ))
    ang = jnp.arange(seq, dtype=F32)[:, None] * inv[None, :]
    cos = jnp.cos(ang)
    sin = jnp.sin(ang)
    cos_map = jnp.concatenate([cos, cos], axis=-1)
    sin_map = jnp.concatenate([-sin, sin], axis=-1)
    reps = GW // DQ
    return jnp.tile(cos_map, (1, reps)), jnp.tile(sin_map, (1, reps))


def kernel(x, c, ada_w, ada_b, norm1_g, norm2_g, w_in, w_out, rw_mu, rw_w0, rw_w_up, rw_a0, rw_a_up, rw_g_up, rw_k_k, rw_k_a, rw_r_k, rw_ln_g, rw_ln_b, df_lam_q1, df_lam_k1, df_lam_q2, df_lam_k2, df_q_g, df_k_g, df_sub_g, sg_w, sg_b, sg_ln_g, sg_ln_b, fx_q_g, fx_k_g, fx_f_b, ffn_up, ffn_conv, ffn_conv_b, ffn_down):
    Bn, S, D = x.shape
    L = ada_w.shape[0]
    T = Bn * S
    dff = ffn_down.shape[1]
    tm = min(ROW_TILE, S)
    tp = min(PREP_TILE, S)
    tile = min(ATT_TILE, S)

    mod = _ada_mod(c, ada_w, ada_b).reshape(L, Bn, 6, D)
    cos, sin = _rope_tables(S)
    cost, sint = cos.T, sin.T
    x2 = x.reshape(T, D)

    oa = 0
    ob = oa + RW_COLS
    oc = ob + 3 * GW
    od = oc + 2 * GW

    for l in range(L):
        lambda_init = 0.8 - 0.6 * math.exp(-0.3 * l)
        wl = w_in[l].astype(BF16)
        if l == 0:
            h = _norm_mod(x2, mod[l], norm1_g[l], seq=S, row0=0, tm=tm)

        lora_w = jnp.zeros((LANES, 3 * GW), F32)
        lora_w = lora_w.at[0:RW_DECAY_RANK, 0:GW].set(rw_w_up[l])
        lora_w = lora_w.at[RW_DECAY_RANK:RW_DECAY_RANK + RW_A_RANK, GW:2 * GW].set(rw_a_up[l])
        lora_w = lora_w.at[RW_DECAY_RANK + RW_A_RANK:LANES, 2 * GW:3 * GW].set(rw_g_up[l])
        vec = jnp.stack([rw_w0[l], rw_a0[l], rw_k_k[l], rw_k_a[l], rw_r_k[l].reshape(GW),
                         rw_ln_g[l], rw_ln_b[l], jnp.zeros((GW,), F32)])
        ya = _rwkv_mixer(h, wl[:, oa:ob], rw_mu[l].reshape(1, RW_COLS), lora_w, vec,
                         batch=Bn, seq=S, tt=min(RW_TILE, S), mode=RW_MODE)

        gk = jnp.tile(df_k_g[l], GW // DQ).reshape(1, GW)
        gqt = jnp.broadcast_to((jnp.tile(df_q_g[l], GW // DQ) * (DQ ** -0.5 * LOG2E))[:, None], (GW, tp))
        diff_args = (wl[:, ob + GW:ob + 2 * GW], wl[:, ob:ob + GW].T, wl[:, ob + 2 * GW:oc].T,
                     gk, gqt, cos, sin, cost, sint)

        sg_bias = jnp.repeat(sg_b[l].T, HEAD_DIM, axis=1)
        gmlp_args = (wl[:, oc:od], sg_w[l], sg_bias, jnp.stack([sg_ln_g[l], sg_ln_b[l]]))

        wf = jnp.zeros((D, LANES), BF16).at[:, 0:HG].set(wl[:, od + 3 * GW:od + 3 * GW + HG])
        fb = jnp.zeros((1, LANES), F32).at[0, 0:HG].set(fx_f_b[l])
        gkd = jnp.tile(fx_k_g[l], HG).reshape(1, GW)
        gqd = jnp.broadcast_to((jnp.tile(fx_q_g[l], HG) * (HEAD_DIM ** -0.5 * LOG2E))[:, None], (GW, tp))
        fox_args = (wl[:, od + GW:od + 2 * GW], wf, wl[:, od:od + GW].T, wl[:, od + 2 * GW:od + 3 * GW].T,
                    gkd, gqd, fb)

        (qb, kb, vb), (qd, kd, vd), yc = _mixer_prep(h, diff_args, fox_args, gmlp_args,
                                                     batch=Bn, seq=S, tm=tp, tile=tile)
        lam_vecs = jnp.stack([df_lam_q1[l], df_lam_k1[l], df_lam_q2[l], df_lam_k2[l]])
        sub_gain = jnp.broadcast_to((jnp.tile(df_sub_g[l], HG) * (1.0 - lambda_init))[:, None], (GW, tile))
        yb = _attention(qb, kb, vb, lam_vecs, sub_gain, batch=Bn, seq=S, tile=tile,
                        n_maps=2, lambda_init=lambda_init)
        yd = _attention(qd, kd, vd, jnp.zeros((8, LANES), F32), jnp.zeros((8, LANES), F32),
                        batch=Bn, seq=S, tile=tile, n_maps=1, lambda_init=0.0)

        nl = (l + 1) % L
        x2, h = _ffn((ya, yb, yc, yd), x2, w_out[l].astype(BF16).reshape(N_MIXERS, GW, D), norm2_g[l],
                     ffn_up[l].astype(BF16), ffn_conv[l], ffn_conv_b[l].reshape(1, 2 * dff),
                     ffn_down[l].astype(BF16), mod[l], mod[nl], norm1_g[nl], seq=S, tm=tm)

    return x2.reshape(Bn, S, D)
```

```python
import functools
import math

import jax
import jax.numpy as jnp
from jax import lax
from jax.experimental import pallas as pl
from jax.experimental.pallas import tpu as pltpu

F32 = jnp.float32
BF16 = jnp.bfloat16
HIGHEST = lax.Precision.HIGHEST

N_MIXERS = 4
HEAD_DIM = 64
HG = 4
GW = HG * HEAD_DIM
DQ = HEAD_DIM // 2
RW_DECAY_RANK = 32
RW_A_RANK = 32
RW_GATE_RANK = 64
RW_COLS = 3 * GW + RW_DECAY_RANK + RW_A_RANK + RW_GATE_RANK
RW_GN_EPS = 64e-5
SG_CHUNK = 128
ATT_CHUNK = 64
ROPE_THETA = 10000.0
EPS = 1e-6
LOG2E = math.log2(math.e)
NEG_INF = -1e30

LANES = 128
V7X_VMEM_BYTES = 64 * 1024 * 1024
VMEM_LIMIT = 56 * 1024 * 1024

ROW_TILE = 512
PREP_TILE = 1024
RW_CHUNK = 64
RW_TILE = 512
ATT_TILE = 256
SLAB = 128
VROWS = HEAD_DIM + 16
RW_MODE = "lo"

NN = (((1,), (0,)), ((), ()))
NT = (((1,), (1,)), ((), ()))
TN = (((0,), (0,)), ((), ()))


def _params(sem):
    return pltpu.CompilerParams(dimension_semantics=sem, vmem_limit_bytes=VMEM_LIMIT)


def _mm(a, b, dims=NN, mode="hi"):
    if mode == "hi":
        return lax.dot_general(a.astype(F32), b.astype(F32), dims, precision=HIGHEST,
                               preferred_element_type=F32)
    if mode == "lo":
        return lax.dot_general(a.astype(BF16), b.astype(BF16), dims, preferred_element_type=F32)
    ah = a.astype(BF16)
    al = (a - ah.astype(F32)).astype(BF16)
    bh = b.astype(BF16)
    bl = (b - bh.astype(F32)).astype(BF16)
    dg = functools.partial(lax.dot_general, dimension_numbers=dims, preferred_element_type=F32)
    return dg(ah, bh) + (dg(ah, bl) + dg(al, bh))


def _mm_exact(a, b, dims=NN, left=True):
    x = a if left else b
    xh = x.astype(BF16)
    xl = (x - xh.astype(F32)).astype(BF16)
    dg = functools.partial(lax.dot_general, dimension_numbers=dims, preferred_element_type=F32)
    if left:
        bb = b.astype(BF16)
        return dg(xh, bb) + dg(xl, bb)
    ab = a.astype(BF16)
    return dg(ab, xh) + dg(ab, xl)


def _iota(shape, dim):
    return lax.broadcasted_iota(jnp.int32, shape, dim)


def _group_matrix(n, group, value=1.0):
    same = (_iota((n, n), 0) // group) == (_iota((n, n), 1) // group)
    return jnp.where(same, value, 0.0).astype(F32)


def _softplus(x):
    return jnp.maximum(x, 0.0) + jnp.log1p(jnp.exp(-jnp.abs(x)))


def _sigmoid(x):
    return 1.0 / (1.0 + jnp.exp(-x))


def _ada_kernel(c_ref, w_ref, b_ref, o_ref):
    c = c_ref[...]
    cond = c * _sigmoid(c)
    o_ref[...] = _mm(cond, w_ref[...], NN, "hi") + b_ref[...]


def _ada_mod(c, ada_w, ada_b):
    L, D, D6 = ada_w.shape
    Bn = c.shape[0]
    tn = D6 // 4
    return pl.pallas_call(
        _ada_kernel,
        grid=(L, D6 // tn),
        in_specs=[
            pl.BlockSpec((Bn, D), lambda l, j: (0, 0)),
            pl.BlockSpec((None, D, tn), lambda l, j: (l, 0, j)),
            pl.BlockSpec((None, 1, tn), lambda l, j: (l, 0, j)),
        ],
        out_specs=pl.BlockSpec((None, Bn, tn), lambda l, j: (l, 0, j)),
        out_shape=jax.ShapeDtypeStruct((L, Bn, D6), F32),
        compiler_params=_params(("arbitrary", "arbitrary")),
        name="ada_mod",
    )(c, ada_w, ada_b.reshape(L, 1, D6))


def _modulated_norm(x, g, shift, scale):
    y = x * lax.rsqrt(jnp.mean(x * x, axis=-1, keepdims=True) + EPS) * g
    return y * (1.0 + scale) + shift


def _norm_kernel(x_ref, mod_ref, g_ref, o_ref, *, row0):
    mod = mod_ref[...]
    h = _modulated_norm(x_ref[...], g_ref[...], mod[row0:row0 + 1], mod[row0 + 1:row0 + 2])
    o_ref[...] = h.astype(BF16)


def _norm_mod(x2, mod_l, g, *, seq, row0, tm):
    T, D = x2.shape
    per_seq = seq // tm
    return pl.pallas_call(
        functools.partial(_norm_kernel, row0=row0),
        grid=(T // tm,),
        in_specs=[
            pl.BlockSpec((tm, D), lambda i: (i, 0)),
            pl.BlockSpec((None, 6, D), lambda i: (i // per_seq, 0, 0)),
            pl.BlockSpec((1, D), lambda i: (0, 0)),
        ],
        out_specs=pl.BlockSpec((tm, D), lambda i: (i, 0)),
        out_shape=jax.ShapeDtypeStruct((T, D), BF16),
        compiler_params=_params(("arbitrary",)),
        name="norm_mod",
    )(x2, mod_l, g.reshape(1, D))


def _blockdiag(mp, head_lane_masks):
    return jnp.concatenate([jnp.where(m, mp, 0.0) for m in head_lane_masks], axis=0)


def _head_blocks(full, head_lane_masks, C):
    out = jnp.where(head_lane_masks[0], full[0:C], 0.0)
    for h in range(1, HG):
        out = out + jnp.where(head_lane_masks[h], full[h * C:(h + 1) * C], 0.0)
    return out


def _apply_packed(mp, rhs, head_lane_masks, C, mode):
    return _head_blocks(_mm(mp, rhs, TN, mode), head_lane_masks, C)


def _rwkv_kernel(h_ref, wa_ref, mu_ref, lora_ref, vec_ref, o_ref,
                 carry_ref, st_ref, hs_ref, *, C, mode):
    TT = h_ref.shape[0]

    @pl.when(pl.program_id(1) == 0)
    def _():
        carry_ref[...] = jnp.zeros_like(carry_ref)
        st_ref[...] = jnp.zeros_like(st_ref)

    vec = vec_ref[...]
    w0, a0, k_k, k_a, r_k, ln_g, ln_b = (vec[i:i + 1] for i in range(7))

    pa = jnp.dot(h_ref[...], wa_ref[...], preferred_element_type=F32)
    prev = pltpu.roll(pa, 1, axis=0)
    prev = jnp.where(_iota((TT, 1), 0) == 0, carry_ref[...], prev)
    carry_ref[...] = pa[TT - 1:TT]
    pa = pa + (prev - pa) * mu_ref[...]

    r = pa[:, 0:GW]
    k = pa[:, GW:2 * GW]
    v = pa[:, 2 * GW:3 * GW]
    lo = pa[:, 3 * GW:3 * GW + LANES]
    lane = _iota((1, LANES), 1)
    act = jnp.where(lane < RW_DECAY_RANK, jnp.tanh(lo),
                    jnp.where(lane < RW_DECAY_RANK + RW_A_RANK, lo, _sigmoid(lo)))
    lora = _mm_exact(act, lora_ref[...])
    w_log = -_softplus(-(w0 + lora[:, 0:GW])) - 0.5
    lw = -jnp.exp(w_log)
    a = _sigmoid(a0 + lora[:, GW:2 * GW])
    gate = lora[:, 2 * GW:3 * GW]

    gsum = _group_matrix(GW, HEAD_DIM)
    kk = k * k_k
    kk = kk / jnp.maximum(jnp.sqrt(_mm(kk * kk, gsum, NN, "lo")), 1e-12)
    k2 = k * (1.0 + (a - 1.0) * k_a)
    bonus = _mm(r * k2 * r_k, gsum, NN, "lo") * v

    aa = -kk
    b = kk * a

    lane_head = _iota((1, GW), 1) // HEAD_DIM
    hmask = [lane_head == h for h in range(HG)]
    s_idx = _iota((C, GW), 0)
    t_idx = _iota((C, GW), 1) % C
    strict = s_idx < t_idx
    incl = s_idx <= t_idx
    eye_p = jnp.where(s_idx == t_idx, 1.0, 0.0).astype(F32)
    l_incl = jnp.where(_iota((C, C), 0) >= _iota((C, C), 1), 1.0, 0.0).astype(F32)
    bd = (_iota((GW, GW), 0) // HEAD_DIM) == (_iota((GW, GW), 1) // HEAD_DIM)
    eye_k = jnp.where(_iota((GW, GW), 0) == _iota((GW, GW), 1), 1.0, 0.0).astype(F32)
    chunks = range(TT // C)
    sl = lambda x, c: x[c * C:(c + 1) * C]

    cl = jnp.concatenate([_mm_exact(l_incl, sl(lw, c), left=False) for c in chunks], axis=0)
    g_in = jnp.exp(cl)
    g_inv = jnp.exp(-cl)
    at = aa * jnp.exp(cl - lw)
    rt = r * g_in
    bt = b * g_inv
    kt = k2 * g_inv

    bks = [jnp.concatenate([sl(bt, c), sl(kt, c)], axis=0) for c in chunks]
    xs = [_mm(bks[c], jnp.concatenate([_blockdiag(sl(at, c), hmask), _blockdiag(sl(rt, c), hmask)], axis=0),
              NT, mode) for c in chunks]
    n_p = [jnp.where(strict, x[0:C, 0:GW], 0.0) for x in xs]
    ak_p = [jnp.where(strict, x[C:2 * C, 0:GW], 0.0) for x in xs]
    rbk_p = [jnp.concatenate([jnp.where(incl, x[0:C, GW:2 * GW], 0.0),
                              jnp.where(incl, x[C:2 * C, GW:2 * GW], 0.0)], axis=0) for x in xs]

    p_p = [eye_p + n for n in n_p]
    q_p = n_p
    bd_dtype = BF16 if mode == "lo" else F32
    q_bd = [_blockdiag(q, hmask).astype(bd_dtype) for q in q_p]
    for _ in range(int(math.log2(C)) - 1):
        q_p = [_mm(q, b, NN, mode) for q, b in zip(q_p, q_bd)]
        q_bd = [_blockdiag(q, hmask).astype(bd_dtype) for q in q_p]
        p_p = [p + _mm(p, b, NN, mode) for p, b in zip(p_p, q_bd)]

    wv = [_apply_packed(ak_p[c], sl(v, c), hmask, C, mode) for c in chunks]
    ta_tw = [_mm(p_p[c], jnp.concatenate([sl(at, c), wv[c]], axis=1), TN, mode) for c in chunks]
    ta = [_head_blocks(f[:, 0:GW], hmask, C) for f in ta_tw]
    tw = [_head_blocks(f[:, GW:2 * GW], hmask, C) for f in ta_tw]
    g_end = [g_in[(c + 1) * C - 1:(c + 1) * C] for c in chunks]
    m_t = [(eye_k + jnp.where(bd, _mm(ta[c], sl(bt, c), TN, mode), 0.0)) * g_end[c] for c in chunks]
    g_t = [jnp.where(bd, _mm(jnp.concatenate([tw[c], sl(v, c)], axis=0), bks[c], TN, mode), 0.0) * g_end[c]
           for c in chunks]

    ht = st_ref[...]
    for c in chunks:
        hs_ref[c] = ht
        ht = _mm(ht, m_t[c], NN, mode) + g_t[c]
    st_ref[...] = ht

    ys = []
    for c in chunks:
        sh = _mm(jnp.concatenate([ta[c], sl(rt, c)], axis=0), hs_ref[c], NT, mode)
        uv = jnp.concatenate([sh[0:C] + tw[c], sl(v, c)], axis=0)
        ys.append(sh[C:2 * C] + _apply_packed(rbk_p[c], uv, hmask, C, mode))
    y = jnp.concatenate(ys, axis=0)

    gmean = _group_matrix(GW, HEAD_DIM, 1.0 / HEAD_DIM)
    yc = y - _mm(y, gmean, NN, "lo")
    yn = yc * lax.rsqrt(_mm(yc * yc, gmean, NN, "lo") + RW_GN_EPS) * ln_g + ln_b
    o_ref[...] = ((yn + bonus) * gate).astype(BF16)


def _rwkv_mixer(h, wa, mu, lora_w, vec, *, batch, seq, tt, mode):
    T, D = h.shape
    per_seq = seq // tt
    C = RW_CHUNK
    return pl.pallas_call(
        functools.partial(_rwkv_kernel, C=C, mode=mode),
        grid=(batch, per_seq),
        in_specs=[
            pl.BlockSpec((tt, D), lambda b, i: (b * per_seq + i, 0)),
            pl.BlockSpec((D, RW_COLS), lambda b, i: (0, 0)),
            pl.BlockSpec((1, RW_COLS), lambda b, i: (0, 0)),
            pl.BlockSpec((LANES, 3 * GW), lambda b, i: (0, 0)),
            pl.BlockSpec((8, GW), lambda b, i: (0, 0)),
        ],
        out_specs=pl.BlockSpec((tt, GW), lambda b, i: (b * per_seq + i, 0)),
        out_shape=jax.ShapeDtypeStruct((T, GW), BF16),
        scratch_shapes=[pltpu.VMEM((1, RW_COLS), F32), pltpu.VMEM((GW, GW), F32),
                        pltpu.VMEM((tt // C, GW, GW), F32)],
        compiler_params=_params(("arbitrary", "arbitrary")),
        name="rwkv7_mixer",
    )(h, wa, mu, lora_w, vec)


def _place_heads_lanes(n_in, width):
    src = _iota((n_in, HG * SLAB), 0)
    dst = _iota((n_in, HG * SLAB), 1)
    ok = ((dst // SLAB) == (src // width)) & ((dst % SLAB) == (src % width)) & ((dst % SLAB) < width)
    return jnp.where(ok, 1.0, 0.0).astype(BF16)


def _slab_rows(xt, extra):
    parts = []
    for h in range(HG):
        parts.append(xt[h * HEAD_DIM:(h + 1) * HEAD_DIM])
        parts.append(extra)
    return jnp.concatenate(parts, axis=0)


def _diff_prep_kernel(h_ref, wk_ref, wqt_ref, wvt_ref, gk_ref, gqt_ref,
                      cos_ref, sin_ref, cost_ref, sint_ref, q_out, k_out, v_out, *, tile):
    tm = h_ref.shape[0]
    hb = h_ref[...]
    k = jnp.dot(hb, wk_ref[...], preferred_element_type=F32)
    qt = lax.dot_general(wqt_ref[...], hb, NT, preferred_element_type=F32)
    vt = lax.dot_general(wvt_ref[...], hb, NT, preferred_element_type=F32)
    gmean = _group_matrix(GW, DQ, 1.0 / DQ)

    kn = k * lax.rsqrt(_mm(k * k, gmean, NN, "lo") + EPS) * gk_ref[...]
    first_half = (_iota((1, GW), 1) % DQ) < (DQ // 2)
    partner = jnp.where(first_half, pltpu.roll(kn, GW - DQ // 2, axis=1), pltpu.roll(kn, DQ // 2, axis=1))
    kr = kn * cos_ref[...] + partner * sin_ref[...]
    k_out[...] = jnp.dot(kr.astype(BF16), _place_heads_lanes(GW, HEAD_DIM),
                         preferred_element_type=F32).astype(BF16)

    qn = qt * lax.rsqrt(_mm(gmean, qt * qt, NN, "lo") + EPS) * gqt_ref[...]
    first_half_t = (_iota((GW, 1), 0) % DQ) < (DQ // 2)
    partner_t = jnp.where(first_half_t, pltpu.roll(qn, GW - DQ // 2, axis=0), pltpu.roll(qn, DQ // 2, axis=0))
    qr = (qn * cost_ref[...] + partner_t * sint_ref[...]).astype(BF16)
    qs = _slab_rows(qr, jnp.zeros((SLAB - HEAD_DIM, tm), BF16))
    vb = _slab_rows(vt.astype(BF16),
                    jnp.where(_iota((VROWS - HEAD_DIM, tm), 0) == 0, 1.0, 0.0).astype(BF16))
    for c in range(tm // tile):
        q_out[c] = qs[:, c * tile:(c + 1) * tile]
        v_out[c] = vb[:, c * tile:(c + 1) * tile]


def _fox_prep_kernel(h_ref, wk_ref, wf_ref, wqt_ref, wvt_ref, gk_ref, gqt_ref, fb_ref,
                     q_out, k_out, v_out, fcarry_ref, *, tile):
    tm = h_ref.shape[0]

    @pl.when(pl.program_id(1) == 0)
    def _():
        fcarry_ref[...] = jnp.zeros_like(fcarry_ref)

    hb = h_ref[...]
    k = jnp.dot(hb, wk_ref[...], preferred_element_type=F32)
    fl = jnp.dot(hb, wf_ref[...], preferred_element_type=F32)
    qt = lax.dot_general(wqt_ref[...], hb, NT, preferred_element_type=F32)
    vt = lax.dot_general(wvt_ref[...], hb, NT, preferred_element_type=F32)
    gmean = _group_matrix(GW, HEAD_DIM, 1.0 / HEAD_DIM)

    kn = (k * lax.rsqrt(_mm(k * k, gmean, NN, "lo") + EPS) * gk_ref[...]).astype(BF16)

    log_f = -_softplus(-(fl + fb_ref[...]))
    cb = min(tm, LANES)
    l_incl = jnp.where(_iota((cb, cb), 0) >= _iota((cb, cb), 1), 1.0, 0.0).astype(F32)
    carry = fcarry_ref[...]
    blocks = []
    for r in range(tm // cb):
        blocks.append(_mm_exact(l_incl, log_f[r * cb:(r + 1) * cb], left=False) + carry)
        carry = blocks[-1][cb - 1:cb]
    fcarry_ref[...] = carry
    nf = jnp.concatenate(blocks, axis=0) * (-LOG2E)
    head_lane = _iota((1, LANES), 1) < HG
    packed = jnp.zeros_like(nf)
    for i in range(3):
        p = jnp.where(head_lane, nf.astype(BF16).astype(F32), 0.0)
        packed = packed + (p if i == 0 else pltpu.roll(p, HG * i, axis=1))
        nf = nf - p
    src = _iota((LANES, HG * SLAB), 0)
    dst = _iota((LANES, HG * SLAB), 1)
    place = jnp.where((src < 3 * HG) & (dst == (src % HG) * SLAB + HEAD_DIM + src // HG), 1.0, 0.0)
    ks = (jnp.dot(kn, _place_heads_lanes(GW, HEAD_DIM), preferred_element_type=F32)
          + jnp.dot(packed.astype(BF16), place.astype(BF16), preferred_element_type=F32))
    k_out[...] = ks.astype(BF16)

    qn = (qt * lax.rsqrt(_mm(gmean, qt * qt, NN, "lo") + EPS) * gqt_ref[...]).astype(BF16)
    ones_rows = jnp.where(_iota((SLAB - HEAD_DIM, tm), 0) < 3, 1.0, 0.0).astype(BF16)
    qs = _slab_rows(qn, ones_rows)
    vb = _slab_rows(vt.astype(BF16),
                    jnp.where(_iota((VROWS - HEAD_DIM, tm), 0) == 0, 1.0, 0.0).astype(BF16))
    for c in range(tm // tile):
        q_out[c] = qs[:, c * tile:(c + 1) * tile]
        v_out[c] = vb[:, c * tile:(c + 1) * tile]


def _attn_kernel(q_ref, k_ref, v_ref, aux_ref, gain_ref, o_ref, qz_s, m_s, acc_s,
                 s_a, s_b, p_a, p_b, al_a, al_b, mt_a, mt_b, *, n_maps, tile, lambda_init):
    qi = pl.program_id(1)
    row = _iota((tile, tile), 0)
    col = _iota((tile, tile), 1)
    if n_maps == 2:
        diag_ok = (row // ATT_CHUNK) <= (col // ATT_CHUNK)
    else:
        diag_ok = row <= col

    slab_row = _iota((SLAB, 1), 0)
    for h in range(HG):
        qh = q_ref[h * SLAB:(h + 1) * SLAB, :]
        if n_maps == 2:
            qz_s[h, :, 0:tile] = jnp.where(slab_row < DQ, qh, jnp.zeros_like(qh))
            qz_s[h, :, tile:2 * tile] = jnp.where((slab_row >= DQ) & (slab_row < 2 * DQ), qh,
                                                  jnp.zeros_like(qh))
        else:
            qz_s[h] = qh
    m_s[...] = jnp.full(m_s.shape, NEG_INF, F32)
    acc_s[...] = jnp.zeros(acc_s.shape, F32)
    p_b[...] = jnp.zeros(p_b.shape, BF16)
    al_b[...] = jnp.zeros(al_b.shape, F32)

    buf_a = (s_a, p_a, al_a, mt_a)
    buf_b = (s_b, p_b, al_b, mt_b)

    all_heads = tuple(range(HG))

    def scores(j, s_buf, mt_buf, heads=all_heads):
        for h in heads:
            kj = k_ref[pl.ds(pl.multiple_of(j * tile, tile), tile), h * SLAB:(h + 1) * SLAB]
            s = jnp.dot(kj, qz_s[h], preferred_element_type=F32)
            s_buf[h] = s
            for mp in range(n_maps):
                c = h * n_maps + mp
                mt_buf[c:c + 1, :] = jnp.max(s[:, mp * tile:(mp + 1) * tile], axis=0, keepdims=True)

    def value_update(j, p_buf, al_buf, heads=all_heads):
        for h in heads:
            vj = v_ref[j, h * VROWS:(h + 1) * VROWS, :]
            for mp in range(n_maps):
                c = h * n_maps + mp
                rows = slice(c * VROWS, (c + 1) * VROWS)
                pv = jnp.dot(vj, p_buf[c], preferred_element_type=F32)
                acc_s[rows, :] = al_buf[c:c + 1, :] * acc_s[rows, :] + pv

    def softmax(s_buf, p_buf, al_buf, mt_buf, masked, heads=all_heads):
        for h in heads:
            for mp in range(n_maps):
                c = h * n_maps + mp
                for q0 in range(0, tile, LANES):
                    qs = slice(q0, q0 + LANES)
                    s = s_buf[h, :, mp * tile + q0:mp * tile + q0 + LANES]
                    if masked:
                        s = jnp.where(diag_ok[:, qs], s, NEG_INF)
                        tile_max = jnp.max(s, axis=0, keepdims=True)
                    else:
                        tile_max = mt_buf[c:c + 1, qs]
                    m_old = m_s[c:c + 1, qs]
                    m_new = jnp.maximum(m_old, tile_max)
                    m_s[c:c + 1, qs] = m_new
                    al_buf[c:c + 1, qs] = jnp.exp2(m_old - m_new)
                    p_buf[c, :, qs] = jnp.exp2(s - m_new).astype(BF16)

    def step(j, cur, nxt):
        for h in all_heads:
            scores(j + 1, nxt[0], nxt[3], (h,))
            value_update(jnp.maximum(j - 1, 0), nxt[1], nxt[2], (h,))
            softmax(cur[0], cur[1], cur[2], cur[3], False, (h,))

    if n_maps == 2:
        aux = aux_ref[...]
        lam = (jnp.exp(jnp.sum(aux[0:1] * aux[1:2], axis=1, keepdims=True))
               - jnp.exp(jnp.sum(aux[2:3] * aux[3:4], axis=1, keepdims=True)) + lambda_init)

    def normalized(c):
        return (acc_s[c * VROWS:c * VROWS + HEAD_DIM, :]
                / acc_s[c * VROWS + HEAD_DIM:c * VROWS + HEAD_DIM + 1, :])

    def last(j, cur, nxt):
        value_update(jnp.maximum(j - 1, 0), nxt[1], nxt[2])
        softmax(cur[0], cur[1], cur[2], cur[3], True)
        outs = []
        for h in all_heads:
            value_update(j, cur[1], cur[2], (h,))
            o = normalized(h * n_maps)
            if n_maps == 2:
                o = o - lam * normalized(h * n_maps + 1)
                o = o * lax.rsqrt(jnp.mean(o * o, axis=0, keepdims=True) + EPS)
            outs.append(o)
        ot = jnp.concatenate(outs, axis=0)
        if n_maps == 2:
            ot = ot * gain_ref[...]
        o_ref[...] = ot.T.astype(BF16)

    scores(0, s_a, mt_a)

    def pair(j):
        step(j, buf_a, buf_b)
        step(j + 1, buf_b, buf_a)

    pairs_per_trip = 2 if n_maps == 1 else 1
    shift = pairs_per_trip.bit_length()

    def trip(jj, carry):
        for t in range(pairs_per_trip):
            pair(2 * pairs_per_trip * jj + 2 * t)
        return carry

    lax.fori_loop(0, lax.shift_right_logical(qi, shift), trip, 0)
    odd = (qi & 1) == 1

    if pairs_per_trip == 2:
        @pl.when((qi & 2) == 2)
        def _():
            pair(qi - (qi & 3))

    @pl.when(odd)
    def _():
        step(qi - 1, buf_a, buf_b)
        last(qi, buf_b, buf_a)

    @pl.when(jnp.logical_not(odd))
    def _():
        last(qi, buf_a, buf_b)


def _attention(q3, k2, v3, aux, gain, *, batch, seq, tile, n_maps, lambda_init):
    nq = seq // tile
    T = batch * seq
    n_chains = HG * n_maps
    scratch = [pltpu.VMEM((HG, SLAB, n_maps * tile), BF16),
               pltpu.VMEM((n_chains, tile), F32),
               pltpu.VMEM((n_chains * VROWS, tile), F32),
               pltpu.VMEM((HG, tile, n_maps * tile), F32),
               pltpu.VMEM((HG, tile, n_maps * tile), F32),
               pltpu.VMEM((n_chains, tile, tile), BF16),
               pltpu.VMEM((n_chains, tile, tile), BF16)] + [pltpu.VMEM((n_chains, tile), F32)] * 4
    return pl.pallas_call(
        functools.partial(_attn_kernel, n_maps=n_maps, tile=tile, lambda_init=lambda_init),
        grid=(batch, nq),
        in_specs=[
            pl.BlockSpec((None, HG * SLAB, tile), lambda b, i: (b * nq + i, 0, 0)),
            pl.BlockSpec((seq, HG * SLAB), lambda b, i: (b, 0)),
            pl.BlockSpec((nq, HG * VROWS, tile), lambda b, i: (b, 0, 0)),
            pl.BlockSpec(aux.shape, lambda b, i: (0, 0)),
            pl.BlockSpec(gain.shape, lambda b, i: (0, 0)),
        ],
        out_specs=pl.BlockSpec((tile, GW), lambda b, i: (b * nq + i, 0)),
        out_shape=jax.ShapeDtypeStruct((T, GW), BF16),
        scratch_shapes=scratch,
        compiler_params=_params(("arbitrary", "arbitrary")),
        name="diff_attention" if n_maps == 2 else "forgetting_attention",
    )(q3, k2, v3, aux, gain)


def _erf_gelu(x):
    return 0.5 * x * (1.0 + lax.erf(x * (1.0 / math.sqrt(2.0))))


def _gmlp_kernel(h_ref, wc_ref, sgw_ref, bias_ref, ln_ref, o_ref):
    tm = h_ref.shape[0]
    pc = _erf_gelu(jnp.dot(h_ref[...], wc_ref[...], preferred_element_type=F32))
    u = pc[:, 0:GW]
    v = pc[:, GW:2 * GW]
    mu = jnp.mean(v, axis=-1, keepdims=True)
    vc = v - mu
    var = jnp.mean(vc * vc, axis=-1, keepdims=True)
    ln = ln_ref[...]
    vn = (vc * lax.rsqrt(var + EPS) * ln[0:1] + ln[1:2]).astype(BF16)

    causal = _iota((SG_CHUNK, SG_CHUNK), 0) >= _iota((SG_CHUNK, SG_CHUNK), 1)
    w = jnp.concatenate([jnp.where(causal, sgw_ref[g], 0.0) for g in range(HG)], axis=0).astype(BF16)
    lane_head = _iota((1, GW), 1) // HEAD_DIM
    bias = bias_ref[...]
    for n in range(tm // SG_CHUNK):
        rows = slice(n * SG_CHUNK, (n + 1) * SG_CHUNK)
        full = jnp.dot(w, vn[rows], preferred_element_type=F32)
        sv = bias
        for g in range(HG):
            sv = sv + jnp.where(lane_head == g, full[g * SG_CHUNK:(g + 1) * SG_CHUNK], 0.0)
        o_ref[rows, :] = (u[rows] * sv).astype(BF16)


N_DIFF_IN, N_FOX_IN, N_GMLP_IN = 9, 7, 4


def _mixer_prep_kernel(*refs, tile):
    h_ref = refs[0]
    a = 1
    diff_in = refs[a:a + N_DIFF_IN]
    fox_in = refs[a + N_DIFF_IN:a + N_DIFF_IN + N_FOX_IN]
    gmlp_in = refs[a + N_DIFF_IN + N_FOX_IN:a + N_DIFF_IN + N_FOX_IN + N_GMLP_IN]
    o = a + N_DIFF_IN + N_FOX_IN + N_GMLP_IN
    diff_out, fox_out, gmlp_out, fcarry_ref = refs[o:o + 3], refs[o + 3:o + 6], refs[o + 6], refs[o + 7]
    _diff_prep_kernel(h_ref, *diff_in, *diff_out, tile=tile)
    _fox_prep_kernel(h_ref, *fox_in, *fox_out, fcarry_ref, tile=tile)
    _gmlp_kernel(h_ref, *gmlp_in, gmlp_out)


def _mixer_prep(h, diff_args, fox_args, gmlp_args, *, batch, seq, tm, tile):
    T, D = h.shape
    per_seq = seq // tm
    nt = tm // tile
    const2 = lambda b, i: (0, 0)
    row = lambda b, i: (b * per_seq + i, 0)
    tiles = lambda b, i: (b * per_seq + i, 0, 0)
    pos_rows = lambda b, i: (i, 0)
    pos_cols = lambda b, i: (0, i)
    attn_out_specs = [pl.BlockSpec((nt, HG * SLAB, tile), tiles), pl.BlockSpec((tm, HG * SLAB), row),
                      pl.BlockSpec((nt, HG * VROWS, tile), tiles)]
    attn_out_shape = [jax.ShapeDtypeStruct((T // tile, HG * SLAB, tile), BF16),
                      jax.ShapeDtypeStruct((T, HG * SLAB), BF16),
                      jax.ShapeDtypeStruct((T // tile, HG * VROWS, tile), BF16)]
    in_specs = [
        pl.BlockSpec((tm, D), row),
        pl.BlockSpec((D, GW), const2), pl.BlockSpec((GW, D), const2), pl.BlockSpec((GW, D), const2),
        pl.BlockSpec((1, GW), const2), pl.BlockSpec((GW, tm), const2),
        pl.BlockSpec((tm, GW), pos_rows), pl.BlockSpec((tm, GW), pos_rows),
        pl.BlockSpec((GW, tm), pos_cols), pl.BlockSpec((GW, tm), pos_cols),
        pl.BlockSpec((D, GW), const2), pl.BlockSpec((D, LANES), const2),
        pl.BlockSpec((GW, D), const2), pl.BlockSpec((GW, D), const2),
        pl.BlockSpec((1, GW), const2), pl.BlockSpec((GW, tm), const2), pl.BlockSpec((1, LANES), const2),
        pl.BlockSpec((D, 2 * GW), const2), pl.BlockSpec((HG, SG_CHUNK, SG_CHUNK), lambda b, i: (0, 0, 0)),
        pl.BlockSpec((SG_CHUNK, GW), const2), pl.BlockSpec((2, GW), const2),
    ]
    assert (len(diff_args), len(fox_args), len(gmlp_args)) == (N_DIFF_IN, N_FOX_IN, N_GMLP_IN)
    outs = pl.pallas_call(
        functools.partial(_mixer_prep_kernel, tile=tile),
        grid=(batch, per_seq),
        in_specs=in_specs,
        out_specs=attn_out_specs + attn_out_specs + [pl.BlockSpec((tm, GW), row)],
        out_shape=attn_out_shape + attn_out_shape + [jax.ShapeDtypeStruct((T, GW), BF16)],
        scratch_shapes=[pltpu.VMEM((1, LANES), F32)],
        compiler_params=_params(("arbitrary", "arbitrary")),
        name="mixer_prep",
    )(h, *diff_args, *fox_args, *gmlp_args)
    return outs[0:3], outs[3:6], outs[6]


TAIL = 8


FFN_COLS = 256
FFN_PARTS = 2


def _ffn_kernel(ya, yb, yc, yd, x_ref, wo_ref, g2_ref,
                wup_ref, cw_ref, cb_ref, wdn_ref, mod_ref, modn_ref, gn_ref,
                o_ref, hn_ref, act_ref, tail_ref, *, per_seq):
    i = pl.program_id(0)
    tm = x_ref.shape[0]
    dff = wdn_ref.shape[0]
    mod = mod_ref[...]

    @pl.when((i % per_seq) == 0)
    def _():
        tail_ref[...] = jnp.zeros_like(tail_ref)

    def taps(u, cw):
        return cw[0:1] * pltpu.roll(u, 2, axis=0) + cw[1:2] * pltpu.roll(u, 1, axis=0) + cw[2:3] * u

    glu = lambda a, g: (a * (g * _sigmoid(g))).astype(BF16)
    modn = modn_ref[...]

    def project(rows):
        proj = jnp.dot(ya[rows, :], wo_ref[0], preferred_element_type=F32)
        for m, y_ref in enumerate((yb, yc, yd), start=1):
            proj = proj + jnp.dot(y_ref[rows, :], wo_ref[m], preferred_element_type=F32)
        x_mid = x_ref[rows, :] + mod[2:3] * proj
        return x_mid, _modulated_norm(x_mid, g2_ref[...], mod[3:4], mod[4:5]).astype(BF16)

    def hidden(rows, hx):
        n = hx.shape[0]

        def conv_half(cols):
            u = jnp.dot(hx, wup_ref[:, cols], preferred_element_type=F32)
            cw = cw_ref[:, cols]
            cb = cb_ref[:, cols]
            prev = tail_ref[:, cols]
            tail_ref[:, cols] = u[n - TAIL:n]
            head = taps(jnp.concatenate([prev, u[0:TAIL]], axis=0), cw)[TAIL:2 * TAIL] + cb
            return taps(u, cw) + cb, head

        for j in range(dff // FFN_COLS):
            a, a_head = conv_half(slice(j * FFN_COLS, (j + 1) * FFN_COLS))
            g, g_head = conv_half(slice(dff + j * FFN_COLS, dff + (j + 1) * FFN_COLS))
            act = glu(a, g)
            fixed = jnp.concatenate([glu(a_head, g_head), act[TAIL:2 * TAIL]], axis=0)
            cols = slice(j * FFN_COLS, (j + 1) * FFN_COLS)
            act_ref[rows, cols] = act
            act_ref[rows.start:rows.start + 2 * TAIL, cols] = fixed

    def finish(rows, x_mid):
        y = jnp.dot(act_ref[rows, :], wdn_ref[...], preferred_element_type=F32)
        x = x_mid + mod[5:6] * y
        o_ref[rows, :] = x
        hn_ref[rows, :] = _modulated_norm(x, gn_ref[...], modn[0:1], modn[1:2]).astype(BF16)

    parts = [slice(r * (tm // FFN_PARTS), (r + 1) * (tm // FFN_PARTS)) for r in range(FFN_PARTS)]
    nxt = project(parts[0])
    for r, rows in enumerate(parts):
        x_mid, hx = nxt
        if r + 1 < FFN_PARTS:
            nxt = project(parts[r + 1])
        hidden(rows, hx)
        finish(rows, x_mid)


def _ffn(ys, x2, w_out4, g2, wup, cw, cb, wdn, mod_l, mod_next, g_next, *, seq, tm):
    T, D = x2.shape
    dff = wdn.shape[0]
    per_seq = seq // tm
    once = pl.Buffered(1)
    row = lambda i: (i, 0)
    return pl.pallas_call(
        functools.partial(_ffn_kernel, per_seq=per_seq),
        grid=(T // tm,),
        in_specs=[pl.BlockSpec((tm, GW), row)] * N_MIXERS + [
            pl.BlockSpec((tm, D), row),
            pl.BlockSpec((N_MIXERS, GW, D), lambda i: (0, 0, 0), pipeline_mode=once),
            pl.BlockSpec((1, D), lambda i: (0, 0)),
            pl.BlockSpec((D, 2 * dff), lambda i: (0, 0), pipeline_mode=once),
            pl.BlockSpec((3, 2 * dff), lambda i: (0, 0), pipeline_mode=once),
            pl.BlockSpec((1, 2 * dff), lambda i: (0, 0), pipeline_mode=once),
            pl.BlockSpec((dff, D), lambda i: (0, 0), pipeline_mode=once),
            pl.BlockSpec((None, 6, D), lambda i: (i // per_seq, 0, 0)),
            pl.BlockSpec((None, 6, D), lambda i: (i // per_seq, 0, 0)),
            pl.BlockSpec((1, D), lambda i: (0, 0)),
        ],
        out_specs=[pl.BlockSpec((tm, D), row), pl.BlockSpec((tm, D), row)],
        out_shape=[jax.ShapeDtypeStruct((T, D), F32), jax.ShapeDtypeStruct((T, D), BF16)],
        scratch_shapes=[pltpu.VMEM((tm, dff), BF16), pltpu.VMEM((TAIL, 2 * dff), F32)],
        compiler_params=_params(("arbitrary",)),
        name="conv_glu_ffn",
    )(*ys, x2, w_out4, g2.reshape(1, D), wup, cw, cb, wdn, mod_l, mod_next, g_next.reshape(1, D))


def _rope_tables(seq):
    inv = 1.0 / (ROPE_THETA ** (jnp.arange(0, DQ, 2, dtype=F32) / DQ))
    ang = jnp.arange(seq, dtype=F32)[:, None] * inv[None, :]
    cos = jnp.cos(ang)
    sin = jnp.sin(ang)
    cos_map = jnp.concatenate([cos, cos], axis=-1)
    sin_map = jnp.concatenate([-sin, sin], axis=-1)
    reps = GW // DQ
    return jnp.tile(cos_map, (1, reps)), jnp.tile(sin_map, (1, reps))


def kernel(x, c, ada_w, ada_b, norm1_g, norm2_g, w_in, w_out, rw_mu, rw_w0, rw_w_up, rw_a0, rw_a_up, rw_g_up, rw_k_k, rw_k_a, rw_r_k, rw_ln_g, rw_ln_b, df_lam_q1, df_lam_k1, df_lam_q2, df_lam_k2, df_q_g, df_k_g, df_sub_g, sg_w, sg_b, sg_ln_g, sg_ln_b, fx_q_g, fx_k_g, fx_f_b, ffn_up, ffn_conv, ffn_conv_b, ffn_down):
    Bn, S, D = x.shape
    L = ada_w.shape[0]
    T = Bn * S
    dff = ffn_down.shape[1]
    tm = min(ROW_TILE, S)
    tp = min(PREP_TILE, S)
    tile = min(ATT_TILE, S)

    mod = _ada_mod(c, ada_w, ada_b).reshape(L, Bn, 6, D)
    cos, sin = _rope_tables(S)
    cost, sint = cos.T, sin.T
    x2 = x.reshape(T, D)

    oa = 0
    ob = oa + RW_COLS
    oc = ob + 3 * GW
    od = oc + 2 * GW

    for l in range(L):
        lambda_init = 0.8 - 0.6 * math.exp(-0.3 * l)
        wl = w_in[l].astype(BF16)
        if l == 0:
            h = _norm_mod(x2, mod[l], norm1_g[l], seq=S, row0=0, tm=tm)

        lora_w = jnp.zeros((LANES, 3 * GW), F32)
        lora_w = lora_w.at[0:RW_DECAY_RANK, 0:GW].set(rw_w_up[l])
        lora_w = lora_w.at[RW_DECAY_RANK:RW_DECAY_RANK + RW_A_RANK, GW:2 * GW].set(rw_a_up[l])
        lora_w = lora_w.at[RW_DECAY_RANK + RW_A_RANK:LANES, 2 * GW:3 * GW].set(rw_g_up[l])
        vec = jnp.stack([rw_w0[l], rw_a0[l], rw_k_k[l], rw_k_a[l], rw_r_k[l].reshape(GW),
                         rw_ln_g[l], rw_ln_b[l], jnp.zeros((GW,), F32)])
        ya = _rwkv_mixer(h, wl[:, oa:ob], rw_mu[l].reshape(1, RW_COLS), lora_w, vec,
                         batch=Bn, seq=S, tt=min(RW_TILE, S), mode=RW_MODE)

        gk = jnp.tile(df_k_g[l], GW // DQ).reshape(1, GW)
        gqt = jnp.broadcast_to((jnp.tile(df_q_g[l], GW // DQ) * (DQ ** -0.5 * LOG2E))[:, None], (GW, tp))
        diff_args = (wl[:, ob + GW:ob + 2 * GW], wl[:, ob:ob + GW].T, wl[:, ob + 2 * GW:oc].T,
                     gk, gqt, cos, sin, cost, sint)

        sg_bias = jnp.repeat(sg_b[l].T, HEAD_DIM, axis=1)
        gmlp_args = (wl[:, oc:od], sg_w[l], sg_bias, jnp.stack([sg_ln_g[l], sg_ln_b[l]]))

        wf = jnp.zeros((D, LANES), BF16).at[:, 0:HG].set(wl[:, od + 3 * GW:od + 3 * GW + HG])
        fb = jnp.zeros((1, LANES), F32).at[0, 0:HG].set(fx_f_b[l])
        gkd = jnp.tile(fx_k_g[l], HG).reshape(1, GW)
        gqd = jnp.broadcast_to((jnp.tile(fx_q_g[l], HG) * (HEAD_DIM ** -0.5 * LOG2E))[:, None], (GW, tp))
        fox_args = (wl[:, od + GW:od + 2 * GW], wf, wl[:, od:od + GW].T, wl[:, od + 2 * GW:od + 3 * GW].T,
                    gkd, gqd, fb)

        (qb, kb, vb), (qd, kd, vd), yc = _mixer_prep(h, diff_args, fox_args, gmlp_args,
                                                     batch=Bn, seq=S, tm=tp, tile=tile)
        lam_vecs = jnp.stack([df_lam_q1[l], df_lam_k1[l], df_lam_q2[l], df_lam_k2[l]])
        sub_gain = jnp.broadcast_to((jnp.tile(df_sub_g[l], HG) * (1.0 - lambda_init))[:, None], (GW, tile))
        yb = _attention(qb, kb, vb, lam_vecs, sub_gain, batch=Bn, seq=S, tile=tile,
                        n_maps=2, lambda_init=lambda_init)
        yd = _attention(qd, kd, vd, jnp.zeros((8, LANES), F32), jnp.zeros((8, LANES), F32),
                        batch=Bn, seq=S, tile=tile, n_maps=1, lambda_init=0.0)

        nl = (l + 1) % L
        x2, h = _ffn((ya, yb, yc, yd), x2, w_out[l].astype(BF16).reshape(N_MIXERS, GW, D), norm2_g[l],
                     ffn_up[l].astype(BF16), ffn_conv[l], ffn_conv_b[l].reshape(1, 2 * dff),
                     ffn_down[l].astype(BF16), mod[l], mod[nl], norm1_g[nl], seq=S, tm=tm)

    return x2.reshape(Bn, S, D)
```
